```python
import math
import jax
import jax.numpy as jnp
from jax import lax
import numpy as np

D_MODEL = 1024
BATCH = 8
SEQ = 2048
DEPTH = 4

MEM_LEN = 256
POOL_WIDTH = D_MODEL // 2
POOL_GROUPS = 4
POOL_WINDOWS = (2, 4, 8, 16)
POOL_GROUP_IN = POOL_WIDTH // POOL_GROUPS
POOL_GROUP_OUT = D_MODEL // POOL_GROUPS
CONV_WIDTH = D_MODEL // 2
CONV_KERNEL = 31
IN_COLS = POOL_WIDTH + 2 * CONV_WIDTH + 2 * D_MODEL
N_XHEADS = 4
XHEAD_DIM = D_MODEL // N_XHEADS
D_FF = 2816
N_EXPERTS = 8
TOP_K = 2
N_DENSE = (DEPTH + 1) // 2
N_MOE = DEPTH // 2
EPS = 1e-6

kernel_name = "hybrid_pool_conformer_xattn_moe_trunk"


def rmsnorm(x, g):
    xf = x.astype(jnp.float32)
    y = xf * lax.rsqrt(jnp.mean(xf * xf, axis=-1, keepdims=True) + EPS)
    return (y * g.astype(jnp.float32)).astype(x.dtype)


def layernorm(x, g, b):
    xf = x.astype(jnp.float32)
    mu = jnp.mean(xf, axis=-1, keepdims=True)
    var = jnp.mean(jnp.square(xf - mu), axis=-1, keepdims=True)
    y = (xf - mu) * lax.rsqrt(var + EPS)
    return (y * g.astype(jnp.float32) + b.astype(jnp.float32)).astype(x.dtype)


def causal_multiscale_pool(u, w_pool, pool_scale):
    B, S, _ = u.shape
    ug = u.reshape(B, S, POOL_GROUPS, POOL_GROUP_IN).astype(jnp.float32)
    cs = jnp.cumsum(ug, axis=1)
    t = jnp.arange(1, S + 1, dtype=jnp.float32)
    outs = []
    for g, w in enumerate(POOL_WINDOWS):
        c = cs[:, :, g]
        lag = jnp.pad(c, ((0, 0), (w, 0), (0, 0)))[:, :S]
        cnt = jnp.minimum(t, float(w))[None, :, None]
        outs.append((c - lag) / cnt - ug[:, :, g])
    pooled = jnp.stack(outs, axis=2).astype(u.dtype)
    y = jnp.einsum('bsgc,gcd->bsgd', pooled, w_pool)
    return y.reshape(B, S, D_MODEL) * pool_scale


def conformer_conv(v, w_dw, b_dw, ln_g, ln_b, w_pw):
    a, b = jnp.split(v, 2, axis=-1)
    u = a * jax.nn.sigmoid(b)
    u = lax.conv_general_dilated(
        u, w_dw, window_strides=(1,), padding=[(CONV_KERNEL - 1, 0)],
        dimension_numbers=('NWC', 'WIO', 'NWC'),
        feature_group_count=CONV_WIDTH) + b_dw
    u = jax.nn.silu(layernorm(u, ln_g, ln_b))
    return u @ w_pw


def memory_cross_attention(h, m, w_q, w_kv, w_o):
    B, S, _ = h.shape
    M = m.shape[1]
    q = (h @ w_q).reshape(B, S, N_XHEADS, XHEAD_DIM)
    k, v = jnp.split(m @ w_kv, 2, axis=-1)
    k = k.reshape(B, M, N_XHEADS, XHEAD_DIM)
    v = v.reshape(B, M, N_XHEADS, XHEAD_DIM)
    s = jnp.einsum('bshd,bmhd->bhsm', q, k).astype(jnp.float32) * (XHEAD_DIM ** -0.5)
    p = jax.nn.softmax(s, axis=-1).astype(v.dtype)
    o = jnp.einsum('bhsm,bmhd->bshd', p, v).reshape(B, S, D_MODEL)
    return o @ w_o


def swiglu(h, w_gu, w_down):
    g, u = jnp.split(h @ w_gu, 2, axis=-1)
    return (jax.nn.silu(g) * u) @ w_down


def moe_swiglu(h, w_router, b_router, w_gu, w_down):
    B, S, D = h.shape
    tok = h.reshape(B * S, D)
    logits = tok.astype(jnp.float32) @ w_router.astype(jnp.float32) + b_router.astype(jnp.float32)
    top_v, top_i = lax.top_k(logits, TOP_K)
    gates = jax.nn.softmax(top_v, axis=-1)
    combine = jnp.sum(jax.nn.one_hot(top_i, N_EXPERTS, dtype=jnp.float32) * gates[..., None], axis=1)
    out = jnp.zeros((B * S, D), jnp.float32)
    for e in range(N_EXPERTS):
        out = out + combine[:, e:e + 1] * swiglu(tok, w_gu[e], w_down[e]).astype(jnp.float32)
    return out.reshape(B, S, D).astype(h.dtype)


def setup_inputs(seed: int = 0) -> dict:
    key = jax.random.key(seed)
    ks = iter(jax.random.split(key, 32))
    f32 = jnp.float32

    def w(shape, fan_in):
        return jax.random.normal(next(ks), shape, f32) * (fan_in ** -0.5)

    def gain(shape):
        return 1.0 + 0.01 * jax.random.normal(next(ks), shape, f32)

    def bias(shape):
        return 0.01 * jax.random.normal(next(ks), shape, f32)

    return {
        "x": jax.random.normal(next(ks), (BATCH, SEQ, D_MODEL), f32),
        "mem": jax.random.normal(next(ks), (BATCH, MEM_LEN, D_MODEL), f32),
        "g_mix": gain((DEPTH, D_MODEL)),
        "w_in": w((DEPTH, D_MODEL, IN_COLS), D_MODEL),
        "w_pool": w((DEPTH, POOL_GROUPS, POOL_GROUP_IN, POOL_GROUP_OUT), POOL_GROUP_IN),
        "pool_scale": 1.0 + 0.02 * jax.random.normal(next(ks), (DEPTH, D_MODEL), f32),
        "w_dw": w((DEPTH, CONV_KERNEL, 1, CONV_WIDTH), CONV_KERNEL),
        "b_dw": bias((DEPTH, CONV_WIDTH)),
        "conv_ln_g": gain((DEPTH, CONV_WIDTH)),
        "conv_ln_b": bias((DEPTH, CONV_WIDTH)),
        "w_conv_out": w((DEPTH, CONV_WIDTH, D_MODEL), CONV_WIDTH),
        "w_mix_out": w((DEPTH, D_MODEL, D_MODEL), D_MODEL),
        "g_xattn": gain((DEPTH, D_MODEL)),
        "g_mem": gain((DEPTH, D_MODEL)),
        "w_xq": w((DEPTH, D_MODEL, D_MODEL), D_MODEL),
        "w_xkv": w((DEPTH, D_MODEL, 2 * D_MODEL), D_MODEL),
        "w_xo": w((DEPTH, D_MODEL, D_MODEL), D_MODEL),
        "g_ffn": gain((DEPTH, D_MODEL)),
        "w_ffn_gu": w((N_DENSE, D_MODEL, 2 * D_FF), D_MODEL),
        "w_ffn_down": w((N_DENSE, D_FF, D_MODEL), D_FF),
        "w_router": w((N_MOE, D_MODEL, N_EXPERTS), D_MODEL),
        "b_router": bias((N_MOE, N_EXPERTS)),
        "w_moe_gu": w((N_MOE, N_EXPERTS, D_MODEL, 2 * D_FF), D_MODEL),
        "w_moe_down": w((N_MOE, N_EXPERTS, D_FF, D_MODEL), D_FF),
        "g_final": gain((D_MODEL,)),
    }


def reference(x, mem, g_mix, w_in, w_pool, pool_scale, w_dw, b_dw, conv_ln_g, conv_ln_b,
              w_conv_out, w_mix_out, g_xattn, g_mem, w_xq, w_xkv, w_xo, g_ffn,
              w_ffn_gu, w_ffn_down, w_router, b_router, w_moe_gu, w_moe_down, g_final):
    split_at = [POOL_WIDTH, POOL_WIDTH + 2 * CONV_WIDTH]
    for i in range(DEPTH):
        h = rmsnorm(x, g_mix[i])
        u_pool, v_conv, gate_logits = jnp.split(h @ w_in[i], split_at, axis=-1)
        gate_pool, gate_conv = jnp.split(jax.nn.sigmoid(gate_logits), 2, axis=-1)
        y_pool = causal_multiscale_pool(u_pool, w_pool[i], pool_scale[i])
        y_conv = conformer_conv(v_conv, w_dw[i], b_dw[i], conv_ln_g[i], conv_ln_b[i], w_conv_out[i])
        x = x + (gate_pool * y_pool + gate_conv * y_conv) @ w_mix_out[i]
        h = rmsnorm(x, g_xattn[i])
        m = rmsnorm(mem, g_mem[i])
        x = x + memory_cross_attention(h, m, w_xq[i], w_xkv[i], w_xo[i])
        h = rmsnorm(x, g_ffn[i])
        if i % 2 == 0:
            x = x + swiglu(h, w_ffn_gu[i // 2], w_ffn_down[i // 2])
        else:
            x = x + moe_swiglu(h, w_router[i // 2], b_router[i // 2], w_moe_gu[i // 2], w_moe_down[i // 2])
    return rmsnorm(x, g_final)
```

```python
import functools

import jax
import jax.numpy as jnp
from jax import lax
from jax.experimental import pallas as pl
from jax.experimental.pallas import tpu as pltpu

D_MODEL = 1024
POOL_WIDTH = 512
POOL_GROUPS = 4
POOL_WINDOWS = (2, 4, 8, 16)
POOL_GROUP_IN = POOL_WIDTH // POOL_GROUPS
POOL_GROUP_OUT = D_MODEL // POOL_GROUPS
CONV_WIDTH = 512
CONV_KERNEL = 31
IN_COLS = POOL_WIDTH + 2 * CONV_WIDTH + 2 * D_MODEL
N_XHEADS = 4
XHEAD_DIM = D_MODEL // N_XHEADS
D_FF = 2816
N_EXPERTS = 8
EPS = 1e-6

COL_POOL = 0
COL_GLU_A = POOL_WIDTH
COL_GLU_B = POOL_WIDTH + CONV_WIDTH
COL_GATE_POOL = POOL_WIDTH + 2 * CONV_WIDTH
COL_GATE_CONV = COL_GATE_POOL + D_MODEL

LANES = 128
POOL_HALO = 16
CONV_HALO = 32
SEQ_TILE = 512
TOK_TILE = 512
MOE_TOK_TILE = 1024
FF_CHUNK = 1408
CONV_ROW_CHUNK = 64
VMEM_LIMIT = 56 * 1024 * 1024
NEG_BIG = -1e30

_F32 = jnp.float32
_BF16 = jnp.bfloat16


def _const_spec(shape):
    zeros = (0,) * len(shape)
    return pl.BlockSpec(shape, lambda *_: zeros, pipeline_mode=pl.Buffered(1))


def _rms(x, g):
    return x * lax.rsqrt(jnp.mean(x * x, axis=-1, keepdims=True) + EPS) * g


def _dot(a, b):
    return jnp.dot(a, b, preferred_element_type=_F32)


def _mixer_kernel(x_ref, g_ref, w_in_ref, w_pool_ref, pscale_ref, w_dw_ref, b_dw_ref,
                  ln_g_ref, ln_b_ref, w_pw_ref, w_out_ref, o_ref,
                  up_ext, cv_ext, cv_out, mix_ref):
    s = pl.program_id(1)
    ts = x_ref.shape[0]

    @pl.when(s == 0)
    def _():
        up_ext[0:POOL_HALO, :] = jnp.zeros((POOL_HALO, POOL_WIDTH), _F32)
        cv_ext[0:CONV_HALO, :] = jnp.zeros((CONV_HALO, CONV_WIDTH), _F32)

    @pl.when(s > 0)
    def _():
        up_ext[0:POOL_HALO, :] = up_ext[ts:ts + POOL_HALO, :]
        cv_ext[0:CONV_HALO, :] = cv_ext[ts:ts + CONV_HALO, :]

    x = x_ref[...]
    h = _rms(x, g_ref[...]).astype(_BF16)

    up_ext[POOL_HALO:POOL_HALO + ts, :] = _dot(h, w_in_ref[:, COL_POOL:COL_POOL + POOL_WIDTH])
    pos = (s * ts + 1 + lax.broadcasted_iota(jnp.int32, (ts, 1), 0)).astype(_F32)
    for g, w in enumerate(POOL_WINDOWS):
        c0 = g * POOL_GROUP_IN
        cur = up_ext[POOL_HALO:POOL_HALO + ts, c0:c0 + POOL_GROUP_IN]
        win = cur
        for j in range(1, w):
            win = win + up_ext[POOL_HALO - j:POOL_HALO - j + ts, c0:c0 + POOL_GROUP_IN]
        pooled = win / jnp.minimum(pos, float(w)) - cur
        d0 = g * POOL_GROUP_OUT
        y_pool = _dot(pooled.astype(_BF16), w_pool_ref[g]) * pscale_ref[:, d0:d0 + POOL_GROUP_OUT]
        gate = jax.nn.sigmoid(_dot(h, w_in_ref[:, COL_GATE_POOL + d0:COL_GATE_POOL + d0 + POOL_GROUP_OUT]))
        mix_ref[:, d0:d0 + POOL_GROUP_OUT] = gate * y_pool

    glu_a = _dot(h, w_in_ref[:, COL_GLU_A:COL_GLU_A + CONV_WIDTH])
    glu_b = _dot(h, w_in_ref[:, COL_GLU_B:COL_GLU_B + CONV_WIDTH])
    cv_ext[CONV_HALO:CONV_HALO + ts, :] = glu_a * jax.nn.sigmoid(glu_b)

    tap0 = CONV_HALO - (CONV_KERNEL - 1)
    for r0 in range(0, ts, CONV_ROW_CHUNK):
        for c0 in range(0, CONV_WIDTH, LANES):
            acc = jnp.broadcast_to(b_dw_ref[:, c0:c0 + LANES], (CONV_ROW_CHUNK, LANES))
            for k in range(CONV_KERNEL):
                acc = acc + w_dw_ref[k:k + 1, c0:c0 + LANES] * cv_ext[r0 + tap0 + k:r0 + tap0 + k + CONV_ROW_CHUNK, c0:c0 + LANES]
            cv_out[r0:r0 + CONV_ROW_CHUNK, c0:c0 + LANES] = acc

    u = cv_out[...]
    mu = jnp.mean(u, axis=-1, keepdims=True)
    uc = u - mu
    var = jnp.mean(uc * uc, axis=-1, keepdims=True)
    un = uc * lax.rsqrt(var + EPS) * ln_g_ref[...] + ln_b_ref[...]
    un = un * jax.nn.sigmoid(un)
    y_conv = _dot(un.astype(_BF16), w_pw_ref[...])
    gate_conv = jax.nn.sigmoid(_dot(h, w_in_ref[:, COL_GATE_CONV:COL_GATE_CONV + D_MODEL]))
    mix = mix_ref[...] + gate_conv * y_conv

    o_ref[...] = x + _dot(mix.astype(_BF16), w_out_ref[...])


def _mixer(x, g, w_in, w_pool, pscale, w_dw, b_dw, ln_g, ln_b, w_pw, w_out):
    B, S, D = x.shape
    ts = SEQ_TILE
    tile = pl.BlockSpec((None, ts, D), lambda b, s: (b, s, 0))
    return pl.pallas_call(
        _mixer_kernel,
        grid=(B, S // ts),
        in_specs=[tile, _const_spec(g.shape), _const_spec(w_in.shape), _const_spec(w_pool.shape),
                  _const_spec(pscale.shape), _const_spec(w_dw.shape), _const_spec(b_dw.shape),
                  _const_spec(ln_g.shape), _const_spec(ln_b.shape), _const_spec(w_pw.shape),
                  _const_spec(w_out.shape)],
        out_specs=tile,
        out_shape=jax.ShapeDtypeStruct(x.shape, x.dtype),
        scratch_shapes=[pltpu.VMEM((POOL_HALO + ts, POOL_WIDTH), _F32),
                        pltpu.VMEM((CONV_HALO + ts, CONV_WIDTH), _F32),
                        pltpu.VMEM((ts, CONV_WIDTH), _F32),
                        pltpu.VMEM((ts, D), _F32)],
        compiler_params=pltpu.CompilerParams(
            dimension_semantics=("parallel", "arbitrary"), vmem_limit_bytes=VMEM_LIMIT),
        name="mixer",
    )(x, g, w_in, w_pool, pscale, w_dw, b_dw, ln_g, ln_b, w_pw, w_out)


def _kv_kernel(m_ref, g_ref, w_ref, k_ref, v_ref):
    m = _rms(m_ref[...], g_ref[...]).astype(_BF16)
    k_ref[...] = (_dot(m, w_ref[:, 0:D_MODEL]) * (XHEAD_DIM ** -0.5)).astype(_BF16)
    v_ref[...] = _dot(m, w_ref[:, D_MODEL:2 * D_MODEL]).astype(_BF16)


def _kv_proj(mem, g, w_kv):
    B, M, D = mem.shape
    blk = pl.BlockSpec((None, M, D), lambda b: (b, 0, 0))
    return pl.pallas_call(
        _kv_kernel,
        grid=(B,),
        in_specs=[blk, _const_spec(g.shape), _const_spec(w_kv.shape)],
        out_specs=[blk, blk],
        out_shape=[jax.ShapeDtypeStruct(mem.shape, _BF16)] * 2,
        compiler_params=pltpu.CompilerParams(
            dimension_semantics=("parallel",), vmem_limit_bytes=VMEM_LIMIT),
        name="kv_proj",
    )(mem, g, w_kv)


def _xattn_kernel(x_ref, g_ref, wq_ref, k_ref, v_ref, wo_ref, o_ref, att_ref):
    x = x_ref[...]
    h = _rms(x, g_ref[...]).astype(_BF16)
    q = _dot(h, wq_ref[...]).astype(_BF16)
    for hd in range(N_XHEADS):
        c0 = hd * XHEAD_DIM
        sc = lax.dot_general(q[:, c0:c0 + XHEAD_DIM], k_ref[:, c0:c0 + XHEAD_DIM],
                             (((1,), (1,)), ((), ())), preferred_element_type=_F32)
        p = jnp.exp(sc - jnp.max(sc, axis=-1, keepdims=True))
        denom = jnp.sum(p, axis=-1, keepdims=True)
        att = _dot(p.astype(_BF16), v_ref[:, c0:c0 + XHEAD_DIM]) / denom
        att_ref[:, c0:c0 + XHEAD_DIM] = att.astype(_BF16)
    o_ref[...] = x + _dot(att_ref[...], wo_ref[...])


def _xattn(x, g, w_q, k, v, w_o):
    B, S, D = x.shape
    M = k.shape[1]
    ts = SEQ_TILE
    tile = pl.BlockSpec((None, ts, D), lambda b, s: (b, s, 0))
    kv_blk = pl.BlockSpec((None, M, D), lambda b, s: (b, 0, 0))
    return pl.pallas_call(
        _xattn_kernel,
        grid=(B, S // ts),
        in_specs=[tile, _const_spec(g.shape), _const_spec(w_q.shape), kv_blk, kv_blk,
                  _const_spec(w_o.shape)],
        out_specs=tile,
        out_shape=jax.ShapeDtypeStruct(x.shape, x.dtype),
        scratch_shapes=[pltpu.VMEM((ts, D), _BF16)],
        compiler_params=pltpu.CompilerParams(
            dimension_semantics=("parallel", "parallel"), vmem_limit_bytes=VMEM_LIMIT),
        name="xattn",
    )(x, g, w_q, k, v, w_o)


def _swiglu_chunks(h, w_gu_ref, w_down_ref):
    acc = None
    for c0 in range(0, D_FF, FF_CHUNK):
        gate = _dot(h, w_gu_ref[:, c0:c0 + FF_CHUNK])
        up = _dot(h, w_gu_ref[:, D_FF + c0:D_FF + c0 + FF_CHUNK])
        act = (gate * jax.nn.sigmoid(gate) * up).astype(_BF16)
        part = _dot(act, w_down_ref[c0:c0 + FF_CHUNK, :])
        acc = part if acc is None else acc + part
    return acc


def _ffn_kernel(x_ref, g_ref, w_gu_ref, w_down_ref, o_ref):
    x = x_ref[...]
    h = _rms(x, g_ref[...]).astype(_BF16)
    o_ref[...] = x + _swiglu_chunks(h, w_gu_ref, w_down_ref)


def _ffn(x2, g, w_gu, w_down):
    T, D = x2.shape
    tm = TOK_TILE
    tile = pl.BlockSpec((tm, D), lambda i: (i, 0))
    return pl.pallas_call(
        _ffn_kernel,
        grid=(T // tm,),
        in_specs=[tile, _const_spec(g.shape), _const_spec(w_gu.shape), _const_spec(w_down.shape)],
        out_specs=tile,
        out_shape=jax.ShapeDtypeStruct(x2.shape, x2.dtype),
        compiler_params=pltpu.CompilerParams(
            dimension_semantics=("parallel",), vmem_limit_bytes=VMEM_LIMIT),
        name="ffn",
    )(x2, g, w_gu, w_down)


def _moe_kernel(x_ref, g_ref, w_r_ref, b_r_ref, w_g_ref, w_u_ref, w_down_ref, o_ref, h_ref, comb_ref):
    e = pl.program_id(1)
    c = pl.program_id(2)
    lane = lax.broadcasted_iota(jnp.int32, comb_ref.shape, 1)

    @pl.when((e == 0) & (c == 0))
    def _():
        x = x_ref[...]
        hf = _rms(x, g_ref[...])
        h_ref[...] = hf.astype(_BF16)
        logits = jnp.dot(hf, w_r_ref[...], preferred_element_type=_F32,
                         precision=lax.Precision.HIGHEST) + b_r_ref[...]
        v1 = jnp.max(logits, axis=-1, keepdims=True)
        i1 = jnp.min(jnp.where(logits == v1, lane, LANES), axis=-1, keepdims=True)
        rest = jnp.where(lane == i1, NEG_BIG, logits)
        v2 = jnp.max(rest, axis=-1, keepdims=True)
        i2 = jnp.min(jnp.where(rest == v2, lane, LANES), axis=-1, keepdims=True)
        z = jnp.exp(v2 - v1)
        g1 = 1.0 / (1.0 + z)
        g2 = z / (1.0 + z)
        comb_ref[...] = jnp.where(lane == i1, g1, 0.0) + jnp.where(lane == i2, g2, 0.0)
        o_ref[...] = x

    h = h_ref[...]
    gate = _dot(h, w_g_ref[...])
    up = _dot(h, w_u_ref[...])
    act = (gate * jax.nn.sigmoid(gate) * up).astype(_BF16)
    ce = jnp.sum(jnp.where(lane == e, comb_ref[...], 0.0), axis=-1, keepdims=True)
    o_ref[...] += ce * _dot(act, w_down_ref[...])


def _moe(x2, g, w_r, b_r, w_gu, w_down):
    T, D = x2.shape
    E = w_gu.shape[0]
    tm = MOE_TOK_TILE
    n_chunks = D_FF // FF_CHUNK
    tile = pl.BlockSpec((tm, D), lambda i, e, c: (i, 0))
    return pl.pallas_call(
        _moe_kernel,
        grid=(T // tm, E, n_chunks),
        in_specs=[tile, _const_spec(g.shape), _const_spec(w_r.shape), _const_spec(b_r.shape),
                  pl.BlockSpec((None, D, FF_CHUNK), lambda i, e, c: (e, 0, c)),
                  pl.BlockSpec((None, D, FF_CHUNK), lambda i, e, c: (e, 0, n_chunks + c)),
                  pl.BlockSpec((None, FF_CHUNK, D), lambda i, e, c: (e, c, 0))],
        out_specs=tile,
        out_shape=jax.ShapeDtypeStruct(x2.shape, x2.dtype),
        scratch_shapes=[pltpu.VMEM((tm, D), _BF16), pltpu.VMEM((tm, LANES), _F32)],
        compiler_params=pltpu.CompilerParams(
            dimension_semantics=("parallel", "arbitrary", "arbitrary"), vmem_limit_bytes=VMEM_LIMIT),
        name="moe",
    )(x2, g, w_r, b_r, w_gu, w_gu, w_down)


def _final_norm_kernel(x_ref, g_ref, o_ref):
    o_ref[...] = _rms(x_ref[...], g_ref[...])


def _final_norm(x2, g):
    T, D = x2.shape
    tm = MOE_TOK_TILE
    tile = pl.BlockSpec((tm, D), lambda i: (i, 0))
    return pl.pallas_call(
        _final_norm_kernel,
        grid=(T // tm,),
        in_specs=[tile, _const_spec(g.shape)],
        out_specs=tile,
        out_shape=jax.ShapeDtypeStruct(x2.shape, x2.dtype),
        compiler_params=pltpu.CompilerParams(dimension_semantics=("parallel",)),
        name="final_norm",
    )(x2, g)


def kernel(x, mem, g_mix, w_in, w_pool, pool_scale, w_dw, b_dw, conv_ln_g, conv_ln_b, w_conv_out, w_mix_out, g_xattn, g_mem, w_xq, w_xkv, w_xo, g_ffn, w_ffn_gu, w_ffn_down, w_router, b_router, w_moe_gu, w_moe_down, g_final):
    B, S, D = x.shape
    depth = g_mix.shape[0]
    bf = lambda a: a.astype(_BF16)
    row = lambda a: a.reshape(1, -1)

    w_r_pad = jnp.pad(w_router, ((0, 0), (0, 0), (0, LANES - N_EXPERTS)))
    b_r_pad = jnp.pad(b_router, ((0, 0), (0, LANES - N_EXPERTS)), constant_values=NEG_BIG)

    for i in range(depth):
        x = _mixer(x, row(g_mix[i]), bf(w_in[i]), bf(w_pool[i]), row(pool_scale[i]),
                   w_dw[i].reshape(CONV_KERNEL, CONV_WIDTH), row(b_dw[i]), row(conv_ln_g[i]),
                   row(conv_ln_b[i]), bf(w_conv_out[i]), bf(w_mix_out[i]))
        k, v = _kv_proj(mem, row(g_mem[i]), bf(w_xkv[i]))
        x = _xattn(x, row(g_xattn[i]), bf(w_xq[i]), k, v, bf(w_xo[i]))
        x2 = x.reshape(B * S, D)
        if i % 2 == 0:
            x2 = _ffn(x2, row(g_ffn[i]), bf(w_ffn_gu[i // 2]), bf(w_ffn_down[i // 2]))
        else:
            x2 = _moe(x2, row(g_ffn[i]), w_r_pad[i // 2], row(b_r_pad[i // 2]),
                      bf(w_moe_gu[i // 2]), bf(w_moe_down[i // 2]))
        x = x2.reshape(B, S, D)
    return _final_norm(x.reshape(B * S, D), row(g_final)).reshape(B, S, D)
```

```python
import jax
import jax.numpy as jnp
import numpy as np
from jax import lax
from jax.experimental import pallas as pl
from jax.experimental.pallas import tpu as pltpu

D_MODEL = 1024
POOL_WIDTH = 512
POOL_GROUPS = 4
POOL_WINDOWS = (2, 4, 8, 16)
POOL_GROUP_IN = POOL_WIDTH // POOL_GROUPS
POOL_GROUP_OUT = D_MODEL // POOL_GROUPS
CONV_WIDTH = 512
CONV_KERNEL = 31
IN_COLS = POOL_WIDTH + 2 * CONV_WIDTH + 2 * D_MODEL
N_XHEADS = 4
XHEAD_DIM = D_MODEL // N_XHEADS
D_FF = 2816
N_EXPERTS = 8
EPS = 1e-6

COL_POOL = 0
COL_GLU_A = POOL_WIDTH
COL_GLU_B = POOL_WIDTH + CONV_WIDTH
COL_GATE_POOL = POOL_WIDTH + 2 * CONV_WIDTH
COL_GATE_CONV = COL_GATE_POOL + D_MODEL

LANES = 128
POOL_HALO = 16
CONV_HALO = 32
SEQ_TILE = 512
TOK_TILE = 512
NORM_TILE = 1024
ROUTE_TILE = 512
DISPATCH_TILE = 512
EXPERT_ROW_TILE = 512
DMA_UNROLL = 8
TOP_K = 2
META_G1, META_G2, META_E1, META_E2, META_R1, META_R2 = range(6)
FF_CHUNK = 1408
CONV_ROW_CHUNK = 64
VMEM_LIMIT = 56 * 1024 * 1024
NEG_BIG = -1e30

_F32 = jnp.float32
_BF16 = jnp.bfloat16
_U32 = jnp.uint32
HIGH_HALF_MASK = np.uint32(0xFFFF0000)


def _const_spec(shape):
    zeros = (0,) * len(shape)
    return pl.BlockSpec(shape, lambda *_: zeros, pipeline_mode=pl.Buffered(1))


def _rms(x, g):
    return x * lax.rsqrt(jnp.mean(x * x, axis=-1, keepdims=True) + EPS) * g


def _dot(a, b):
    return jnp.dot(a, b, preferred_element_type=_F32)


def _mixer_kernel(x_ref, g_ref, w_in_ref, w_pool_ref, pscale_ref, w_dw_ref, b_dw_ref,
                  ln_g_ref, ln_b_ref, w_pw_ref, w_out_ref, o_ref,
                  up_ext, cv_ext, cv_out, mix_ref):
    s = pl.program_id(1)
    ts = x_ref.shape[0]

    @pl.when(s == 0)
    def _():
        up_ext[0:POOL_HALO, :] = jnp.zeros((POOL_HALO, POOL_WIDTH), _F32)
        cv_ext[0:CONV_HALO, :] = jnp.zeros((CONV_HALO, CONV_WIDTH), _F32)

    @pl.when(s > 0)
    def _():
        up_ext[0:POOL_HALO, :] = up_ext[ts:ts + POOL_HALO, :]
        cv_ext[0:CONV_HALO, :] = cv_ext[ts:ts + CONV_HALO, :]

    x = x_ref[...]
    h = _rms(x, g_ref[...]).astype(_BF16)

    up_ext[POOL_HALO:POOL_HALO + ts, :] = _dot(h, w_in_ref[:, COL_POOL:COL_POOL + POOL_WIDTH])
    pos = (s * ts + 1 + lax.broadcasted_iota(jnp.int32, (ts, 1), 0)).astype(_F32)
    for g, w in enumerate(POOL_WINDOWS):
        c0 = g * POOL_GROUP_IN
        cur = up_ext[POOL_HALO:POOL_HALO + ts, c0:c0 + POOL_GROUP_IN]
        win = cur
        for j in range(1, w):
            win = win + up_ext[POOL_HALO - j:POOL_HALO - j + ts, c0:c0 + POOL_GROUP_IN]
        pooled = win / jnp.minimum(pos, float(w)) - cur
        d0 = g * POOL_GROUP_OUT
        y_pool = _dot(pooled.astype(_BF16), w_pool_ref[g]) * pscale_ref[:, d0:d0 + POOL_GROUP_OUT]
        gate = jax.nn.sigmoid(_dot(h, w_in_ref[:, COL_GATE_POOL + d0:COL_GATE_POOL + d0 + POOL_GROUP_OUT]))
        mix_ref[:, d0:d0 + POOL_GROUP_OUT] = gate * y_pool

    glu_a = _dot(h, w_in_ref[:, COL_GLU_A:COL_GLU_A + CONV_WIDTH])
    glu_b = _dot(h, w_in_ref[:, COL_GLU_B:COL_GLU_B + CONV_WIDTH])
    cv_ext[CONV_HALO:CONV_HALO + ts, :] = glu_a * jax.nn.sigmoid(glu_b)

    tap0 = CONV_HALO - (CONV_KERNEL - 1)
    for r0 in range(0, ts, CONV_ROW_CHUNK):
        for c0 in range(0, CONV_WIDTH, LANES):
            acc = jnp.broadcast_to(b_dw_ref[:, c0:c0 + LANES], (CONV_ROW_CHUNK, LANES))
            for k in range(CONV_KERNEL):
                acc = acc + w_dw_ref[k:k + 1, c0:c0 + LANES] * cv_ext[r0 + tap0 + k:r0 + tap0 + k + CONV_ROW_CHUNK, c0:c0 + LANES]
            cv_out[r0:r0 + CONV_ROW_CHUNK, c0:c0 + LANES] = acc

    u = cv_out[...]
    mu = jnp.mean(u, axis=-1, keepdims=True)
    uc = u - mu
    var = jnp.mean(uc * uc, axis=-1, keepdims=True)
    un = uc * lax.rsqrt(var + EPS) * ln_g_ref[...] + ln_b_ref[...]
    un = un * jax.nn.sigmoid(un)
    y_conv = _dot(un.astype(_BF16), w_pw_ref[...])
    gate_conv = jax.nn.sigmoid(_dot(h, w_in_ref[:, COL_GATE_CONV:COL_GATE_CONV + D_MODEL]))
    mix = mix_ref[...] + gate_conv * y_conv

    o_ref[...] = x + _dot(mix.astype(_BF16), w_out_ref[...])


def _mixer(x, g, w_in, w_pool, pscale, w_dw, b_dw, ln_g, ln_b, w_pw, w_out):
    B, S, D = x.shape
    ts = SEQ_TILE
    tile = pl.BlockSpec((None, ts, D), lambda b, s: (b, s, 0))
    return pl.pallas_call(
        _mixer_kernel,
        grid=(B, S // ts),
        in_specs=[tile, _const_spec(g.shape), _const_spec(w_in.shape), _const_spec(w_pool.shape),
                  _const_spec(pscale.shape), _const_spec(w_dw.shape), _const_spec(b_dw.shape),
                  _const_spec(ln_g.shape), _const_spec(ln_b.shape), _const_spec(w_pw.shape),
                  _const_spec(w_out.shape)],
        out_specs=tile,
        out_shape=jax.ShapeDtypeStruct(x.shape, x.dtype),
        scratch_shapes=[pltpu.VMEM((POOL_HALO + ts, POOL_WIDTH), _F32),
                        pltpu.VMEM((CONV_HALO + ts, CONV_WIDTH), _F32),
                        pltpu.VMEM((ts, CONV_WIDTH), _F32),
                        pltpu.VMEM((ts, D), _F32)],
        compiler_params=pltpu.CompilerParams(
            dimension_semantics=("parallel", "arbitrary"), vmem_limit_bytes=VMEM_LIMIT),
        name="mixer",
    )(x, g, w_in, w_pool, pscale, w_dw, b_dw, ln_g, ln_b, w_pw, w_out)


def _kv_kernel(m_ref, g_ref, w_ref, k_ref, v_ref):
    m = _rms(m_ref[...], g_ref[...]).astype(_BF16)
    k_ref[...] = (_dot(m, w_ref[:, 0:D_MODEL]) * (XHEAD_DIM ** -0.5)).astype(_BF16)
    v_ref[...] = _dot(m, w_ref[:, D_MODEL:2 * D_MODEL]).astype(_BF16)


def _kv_proj(mem, g, w_kv):
    B, M, D = mem.shape
    blk = pl.BlockSpec((None, M, D), lambda b: (b, 0, 0))
    return pl.pallas_call(
        _kv_kernel,
        grid=(B,),
        in_specs=[blk, _const_spec(g.shape), _const_spec(w_kv.shape)],
        out_specs=[blk, blk],
        out_shape=[jax.ShapeDtypeStruct(mem.shape, _BF16)] * 2,
        compiler_params=pltpu.CompilerParams(
            dimension_semantics=("parallel",), vmem_limit_bytes=VMEM_LIMIT),
        name="kv_proj",
    )(mem, g, w_kv)


def _xattn_kernel(x_ref, g_ref, wq_ref, k_ref, v_ref, wo_ref, o_ref, att_ref):
    x = x_ref[...]
    h = _rms(x, g_ref[...]).astype(_BF16)
    q = _dot(h, wq_ref[...]).astype(_BF16)
    for hd in range(N_XHEADS):
        c0 = hd * XHEAD_DIM
        sc = lax.dot_general(q[:, c0:c0 + XHEAD_DIM], k_ref[:, c0:c0 + XHEAD_DIM],
                             (((1,), (1,)), ((), ())), preferred_element_type=_F32)
        p = jnp.exp(sc - jnp.max(sc, axis=-1, keepdims=True))
        denom = jnp.sum(p, axis=-1, keepdims=True)
        att = _dot(p.astype(_BF16), v_ref[:, c0:c0 + XHEAD_DIM]) / denom
        att_ref[:, c0:c0 + XHEAD_DIM] = att.astype(_BF16)
    o_ref[...] = x + _dot(att_ref[...], wo_ref[...])


def _xattn(x, g, w_q, k, v, w_o):
    B, S, D = x.shape
    M = k.shape[1]
    ts = SEQ_TILE
    tile = pl.BlockSpec((None, ts, D), lambda b, s: (b, s, 0))
    kv_blk = pl.BlockSpec((None, M, D), lambda b, s: (b, 0, 0))
    return pl.pallas_call(
        _xattn_kernel,
        grid=(B, S // ts),
        in_specs=[tile, _const_spec(g.shape), _const_spec(w_q.shape), kv_blk, kv_blk,
                  _const_spec(w_o.shape)],
        out_specs=tile,
        out_shape=jax.ShapeDtypeStruct(x.shape, x.dtype),
        scratch_shapes=[pltpu.VMEM((ts, D), _BF16)],
        compiler_params=pltpu.CompilerParams(
            dimension_semantics=("parallel", "parallel"), vmem_limit_bytes=VMEM_LIMIT),
        name="xattn",
    )(x, g, w_q, k, v, w_o)


def _swiglu_chunks(h, w_gu_ref, w_down_ref):
    acc = None
    for c0 in range(0, D_FF, FF_CHUNK):
        gate = _dot(h, w_gu_ref[:, c0:c0 + FF_CHUNK])
        up = _dot(h, w_gu_ref[:, D_FF + c0:D_FF + c0 + FF_CHUNK])
        act = (gate * jax.nn.sigmoid(gate) * up).astype(_BF16)
        part = _dot(act, w_down_ref[c0:c0 + FF_CHUNK, :])
        acc = part if acc is None else acc + part
    return acc


def _ffn_kernel(x_ref, g_ref, w_gu_ref, w_down_ref, o_ref):
    x = x_ref[...]
    h = _rms(x, g_ref[...]).astype(_BF16)
    o_ref[...] = x + _swiglu_chunks(h, w_gu_ref, w_down_ref)


def _ffn(x2, g, w_gu, w_down):
    T, D = x2.shape
    tm = TOK_TILE
    tile = pl.BlockSpec((tm, D), lambda i: (i, 0))
    return pl.pallas_call(
        _ffn_kernel,
        grid=(T // tm,),
        in_specs=[tile, _const_spec(g.shape), _const_spec(w_gu.shape), _const_spec(w_down.shape)],
        out_specs=tile,
        out_shape=jax.ShapeDtypeStruct(x2.shape, x2.dtype),
        compiler_params=pltpu.CompilerParams(
            dimension_semantics=("parallel",), vmem_limit_bytes=VMEM_LIMIT),
        name="ffn",
    )(x2, g, w_gu, w_down)


def _pack_halves(y):
    half = y.shape[1] // 2
    lo = lax.bitcast_convert_type(y[:, :half].astype(_BF16).astype(_F32), _U32)
    hi = lax.bitcast_convert_type(y[:, half:].astype(_BF16).astype(_F32), _U32)
    return (lo >> 16) | (hi & HIGH_HALF_MASK)


def _unpack_halves(p):
    lo = lax.bitcast_convert_type(p << 16, _F32)
    hi = lax.bitcast_convert_type(p & HIGH_HALF_MASK, _F32)
    return lo, hi


def _route_kernel(x_ref, g_ref, w_r_ref, b_r_ref, hp_ref, meta_ref, cnt_ref, carry_ref):
    i = pl.program_id(0)
    tm = x_ref.shape[0]

    @pl.when(i == 0)
    def _():
        carry_ref[...] = jnp.zeros_like(carry_ref)

    hf = _rms(x_ref[...], g_ref[...])
    hp_ref[...] = _pack_halves(hf)

    lane = lax.broadcasted_iota(jnp.int32, (tm, LANES), 1)
    logits = jnp.dot(hf, w_r_ref[...], preferred_element_type=_F32,
                     precision=lax.Precision.HIGHEST) + b_r_ref[...]
    v1 = jnp.max(logits, axis=-1, keepdims=True)
    i1 = jnp.min(jnp.where(logits == v1, lane, LANES), axis=-1, keepdims=True)
    rest = jnp.where(lane == i1, NEG_BIG, logits)
    v2 = jnp.max(rest, axis=-1, keepdims=True)
    i2 = jnp.min(jnp.where(rest == v2, lane, LANES), axis=-1, keepdims=True)
    z = jnp.exp(v2 - v1)
    g1 = 1.0 / (1.0 + z)
    g2 = z / (1.0 + z)

    sel1 = lane == i1
    sel2 = lane == i2
    chosen = jnp.where(sel1 | sel2, 1.0, 0.0)
    earlier = (lax.broadcasted_iota(jnp.int32, (tm, tm), 0)
               > lax.broadcasted_iota(jnp.int32, (tm, tm), 1))
    before = _dot(jnp.where(earlier, 1.0, 0.0).astype(_BF16), chosen.astype(_BF16)) + carry_ref[...]
    r1 = jnp.sum(jnp.where(sel1, before, 0.0), axis=-1, keepdims=True)
    r2 = jnp.sum(jnp.where(sel2, before, 0.0), axis=-1, keepdims=True)
    carry_ref[...] += jnp.sum(chosen, axis=0, keepdims=True)
    cnt_ref[...] = carry_ref[...]

    meta = jnp.where(lane == META_G1, g1, 0.0)
    meta = jnp.where(lane == META_G2, g2, meta)
    meta = jnp.where(lane == META_E1, i1.astype(_F32), meta)
    meta = jnp.where(lane == META_E2, i2.astype(_F32), meta)
    meta = jnp.where(lane == META_R1, r1, meta)
    meta = jnp.where(lane == META_R2, r2, meta)
    meta_ref[...] = meta


def _route(x2, g, w_r, b_r):
    T, D = x2.shape
    tm = ROUTE_TILE
    return pl.pallas_call(
        _route_kernel,
        grid=(T // tm,),
        in_specs=[pl.BlockSpec((tm, D), lambda i: (i, 0)), _const_spec(g.shape),
                  _const_spec(w_r.shape), _const_spec(b_r.shape)],
        out_specs=[pl.BlockSpec((tm, D // 2), lambda i: (i, 0)),
                   pl.BlockSpec((tm, LANES), lambda i: (i, 0)),
                   pl.BlockSpec((1, LANES), lambda i: (0, 0))],
        out_shape=[jax.ShapeDtypeStruct((T, D // 2), _U32),
                   jax.ShapeDtypeStruct((T, LANES), _F32),
                   jax.ShapeDtypeStruct((1, LANES), _F32)],
        scratch_shapes=[pltpu.VMEM((1, LANES), _F32)],
        compiler_params=pltpu.CompilerParams(
            dimension_semantics=("arbitrary",), vmem_limit_bytes=VMEM_LIMIT),
        name="moe_route",
    )(x2, g, w_r, b_r)


def _row_copy(src_ref, src_row, dst_ref, dst_row, sem):
    return pltpu.make_async_copy(src_ref.at[pl.ds(src_row, 1)], dst_ref.at[pl.ds(dst_row, 1)], sem)


def _dispatch_kernel(pos1_ref, pos2_ref, hp_ref, init_ref, hs_ref, sem):
    del init_ref
    tm = hp_ref.shape[0]

    def issue(t, carry):
        _row_copy(hp_ref, t, hs_ref, pos1_ref[0, t], sem).start()
        _row_copy(hp_ref, t, hs_ref, pos2_ref[0, t], sem).start()
        return carry

    lax.fori_loop(0, tm, issue, 0, unroll=DMA_UNROLL)
    for _ in range(2):
        pltpu.make_async_copy(hp_ref, hs_ref.at[pl.ds(0, tm)], sem).wait()


def _dispatch(pos1, pos2, hp, n_rows):
    T, W = hp.shape
    tm = DISPATCH_TILE
    smem_blk = pl.BlockSpec((None, 1, tm), lambda i: (i, 0, 0), memory_space=pltpu.SMEM)
    init = jnp.zeros((n_rows, W), hp.dtype)
    return pl.pallas_call(
        _dispatch_kernel,
        grid=(T // tm,),
        in_specs=[smem_blk, smem_blk, pl.BlockSpec((tm, W), lambda i: (i, 0)),
                  pl.BlockSpec(memory_space=pl.ANY)],
        out_specs=pl.BlockSpec(memory_space=pl.ANY),
        out_shape=jax.ShapeDtypeStruct((n_rows, W), hp.dtype),
        scratch_shapes=[pltpu.SemaphoreType.DMA],
        input_output_aliases={3: 0},
        compiler_params=pltpu.CompilerParams(
            dimension_semantics=("arbitrary",), vmem_limit_bytes=VMEM_LIMIT),
        name="moe_dispatch",
    )(pos1.reshape(T // tm, 1, tm), pos2.reshape(T // tm, 1, tm), hp, init)


def _expert_kernel(te_ref, tv_ref, hs_ref, w_gu_ref, w_down_ref, ys_ref):
    del te_ref
    i = pl.program_id(0)
    half = D_MODEL // 2

    @pl.when(tv_ref[i] == 0)
    def _():
        ys_ref[...] = jnp.zeros_like(ys_ref)

    @pl.when(tv_ref[i] != 0)
    def _():
        lo, hi = _unpack_halves(hs_ref[...])
        lo = lo.astype(_BF16)
        hi = hi.astype(_BF16)
        acc = None
        for c0 in range(0, D_FF, FF_CHUNK):
            gate = (_dot(lo, w_gu_ref[0:half, c0:c0 + FF_CHUNK])
                    + _dot(hi, w_gu_ref[half:, c0:c0 + FF_CHUNK]))
            up = (_dot(lo, w_gu_ref[0:half, D_FF + c0:D_FF + c0 + FF_CHUNK])
                  + _dot(hi, w_gu_ref[half:, D_FF + c0:D_FF + c0 + FF_CHUNK]))
            act = (gate * jax.nn.sigmoid(gate) * up).astype(_BF16)
            part = _dot(act, w_down_ref[c0:c0 + FF_CHUNK, :])
            acc = part if acc is None else acc + part
        ys_ref[...] = _pack_halves(acc)


def _experts(tile_expert, tile_valid, hs, w_gu, w_down):
    n_rows, W = hs.shape
    rt = EXPERT_ROW_TILE
    D = w_gu.shape[1]
    grid_spec = pltpu.PrefetchScalarGridSpec(
        num_scalar_prefetch=2,
        grid=(n_rows // rt,),
        in_specs=[pl.BlockSpec((rt, W), lambda i, te, tv: (i, 0)),
                  pl.BlockSpec((None, D, 2 * D_FF), lambda i, te, tv: (te[i], 0, 0)),
                  pl.BlockSpec((None, D_FF, D), lambda i, te, tv: (te[i], 0, 0))],
        out_specs=pl.BlockSpec((rt, W), lambda i, te, tv: (i, 0)),
    )
    return pl.pallas_call(
        _expert_kernel,
        grid_spec=grid_spec,
        out_shape=jax.ShapeDtypeStruct((n_rows, W), hs.dtype),
        compiler_params=pltpu.CompilerParams(
            dimension_semantics=("arbitrary",), vmem_limit_bytes=VMEM_LIMIT),
        name="moe_experts",
    )(tile_expert, tile_valid, hs, w_gu, w_down)


def _combine_kernel(pos1_ref, pos2_ref, x_ref, meta_ref, ys_ref, o_ref, buf1, buf2, sem):
    tm = x_ref.shape[0]
    half = x_ref.shape[1] // 2

    def issue(t, carry):
        _row_copy(ys_ref, pos1_ref[0, t], buf1, t, sem).start()
        _row_copy(ys_ref, pos2_ref[0, t], buf2, t, sem).start()
        return carry

    lax.fori_loop(0, tm, issue, 0, unroll=DMA_UNROLL)
    for buf in (buf1, buf2):
        pltpu.make_async_copy(ys_ref.at[pl.ds(0, tm)], buf, sem).wait()

    g1 = meta_ref[:, META_G1:META_G1 + 1]
    g2 = meta_ref[:, META_G2:META_G2 + 1]
    lo1, hi1 = _unpack_halves(buf1[...])
    lo2, hi2 = _unpack_halves(buf2[...])
    o_ref[:, :half] = x_ref[:, :half] + g1 * lo1 + g2 * lo2
    o_ref[:, half:] = x_ref[:, half:] + g1 * hi1 + g2 * hi2


def _combine(pos1, pos2, x2, meta, ys):
    T, D = x2.shape
    tm = DISPATCH_TILE
    smem_blk = pl.BlockSpec((None, 1, tm), lambda i: (i, 0, 0), memory_space=pltpu.SMEM)
    tile = pl.BlockSpec((tm, D), lambda i: (i, 0))
    return pl.pallas_call(
        _combine_kernel,
        grid=(T // tm,),
        in_specs=[smem_blk, smem_blk, tile, pl.BlockSpec((tm, LANES), lambda i: (i, 0)),
                  pl.BlockSpec(memory_space=pl.ANY)],
        out_specs=tile,
        out_shape=jax.ShapeDtypeStruct(x2.shape, x2.dtype),
        scratch_shapes=[pltpu.VMEM((tm, D // 2), _U32), pltpu.VMEM((tm, D // 2), _U32),
                        pltpu.SemaphoreType.DMA],
        compiler_params=pltpu.CompilerParams(
            dimension_semantics=("arbitrary",), vmem_limit_bytes=VMEM_LIMIT),
        name="moe_combine",
    )(pos1.reshape(T // tm, 1, tm), pos2.reshape(T // tm, 1, tm), x2, meta, ys)


def _moe(x2, g, w_r, b_r, w_gu, w_down):
    T, D = x2.shape
    E = w_gu.shape[0]
    rt = EXPERT_ROW_TILE
    n_tiles = (TOP_K * T) // rt + E
    hp, meta, counts = _route(x2, g, w_r, b_r)

    counts = counts[0, :E].astype(jnp.int32)
    group_tiles = (counts + rt - 1) // rt
    tile_end = jnp.cumsum(group_tiles)
    base = (tile_end - group_tiles) * rt
    e1 = meta[:, META_E1].astype(jnp.int32)
    e2 = meta[:, META_E2].astype(jnp.int32)
    pos1 = base[e1] + meta[:, META_R1].astype(jnp.int32)
    pos2 = base[e2] + meta[:, META_R2].astype(jnp.int32)
    tile_ids = jnp.arange(n_tiles, dtype=jnp.int32)
    tile_expert = jnp.minimum(jnp.searchsorted(tile_end, tile_ids, side="right"), E - 1).astype(jnp.int32)
    tile_valid = (tile_ids < tile_end[E - 1]).astype(jnp.int32)

    hs = _dispatch(pos1, pos2, hp, n_tiles * rt)
    ys = _experts(tile_expert, tile_valid, hs, w_gu, w_down)
    return _combine(pos1, pos2, x2, meta, ys)


def _final_norm_kernel(x_ref, g_ref, o_ref):
    o_ref[...] = _rms(x_ref[...], g_ref[...])


def _final_norm(x2, g):
    T, D = x2.shape
    tm = NORM_TILE
    tile = pl.BlockSpec((tm, D), lambda i: (i, 0))
    return pl.pallas_call(
        _final_norm_kernel,
        grid=(T // tm,),
        in_specs=[tile, _const_spec(g.shape)],
        out_specs=tile,
        out_shape=jax.ShapeDtypeStruct(x2.shape, x2.dtype),
        compiler_params=pltpu.CompilerParams(dimension_semantics=("parallel",)),
        name="final_norm",
    )(x2, g)


def kernel(x, mem, g_mix, w_in, w_pool, pool_scale, w_dw, b_dw, conv_ln_g, conv_ln_b, w_conv_out, w_mix_out, g_xattn, g_mem, w_xq, w_xkv, w_xo, g_ffn, w_ffn_gu, w_ffn_down, w_router, b_router, w_moe_gu, w_moe_down, g_final):
    B, S, D = x.shape
    depth = g_mix.shape[0]
    bf = lambda a: a.astype(_BF16)
    row = lambda a: a.reshape(1, -1)

    w_r_pad = jnp.pad(w_router, ((0, 0), (0, 0), (0, LANES - N_EXPERTS)))
    b_r_pad = jnp.pad(b_router, ((0, 0), (0, LANES - N_EXPERTS)), constant_values=NEG_BIG)

    for i in range(depth):
        x = _mixer(x, row(g_mix[i]), bf(w_in[i]), bf(w_pool[i]), row(pool_scale[i]),
                   w_dw[i].reshape(CONV_KERNEL, CONV_WIDTH), row(b_dw[i]), row(conv_ln_g[i]),
                   row(conv_ln_b[i]), bf(w_conv_out[i]), bf(w_mix_out[i]))
        k, v = _kv_proj(mem, row(g_mem[i]), bf(w_xkv[i]))
        x = _xattn(x, row(g_xattn[i]), bf(w_xq[i]), k, v, bf(w_xo[i]))
        x2 = x.reshape(B * S, D)
        if i % 2 == 0:
            x2 = _ffn(x2, row(g_ffn[i]), bf(w_ffn_gu[i // 2]), bf(w_ffn_down[i // 2]))
        else:
            x2 = _moe(x2, row(g_ffn[i]), w_r_pad[i // 2], row(b_r_pad[i // 2]),
                      bf(w_moe_gu[i // 2]), bf(w_moe_down[i // 2]))
        x = x2.reshape(B, S, D)
    return _final_norm(x.reshape(B * S, D), row(g_final)).reshape(B, S, D)
```

```python
import functools

import jax
import jax.numpy as jnp
import numpy as np
from jax import lax
from jax.experimental import pallas as pl
from jax.experimental.pallas import tpu as pltpu

D_MODEL = 1024
POOL_WIDTH = 512
POOL_GROUPS = 4
POOL_WINDOWS = (2, 4, 8, 16)
POOL_GROUP_IN = POOL_WIDTH // POOL_GROUPS
POOL_GROUP_OUT = D_MODEL // POOL_GROUPS
CONV_WIDTH = 512
CONV_KERNEL = 31
IN_COLS = POOL_WIDTH + 2 * CONV_WIDTH + 2 * D_MODEL
N_XHEADS = 4
XHEAD_DIM = D_MODEL // N_XHEADS
D_FF = 2816
N_EXPERTS = 8
TOP_K = 2
EPS = 1e-6

COL_POOL = 0
COL_GLU_A = POOL_WIDTH
COL_GLU_B = POOL_WIDTH + CONV_WIDTH
COL_GATE_POOL = POOL_WIDTH + 2 * CONV_WIDTH
COL_GATE_CONV = COL_GATE_POOL + D_MODEL

LANES = 128
SUBLANES = 8
POOL_HALO = 16
CONV_HALO = 32
SEQ_TILE = 512
TOK_TILE = 512
FF_CHUNK = 1408
CONV_ROW_CHUNK = 64
ROUTE_TILE = 512
DISPATCH_TILE = 512
EXPERT_ROW_TILE = 512
DMA_UNROLL = 8
VMEM_LIMIT = 56 * 1024 * 1024
NEG_BIG = -1e30
META_G1, META_G2, META_E1, META_E2, META_R1, META_R2 = range(6)

_F32 = jnp.float32
_BF16 = jnp.bfloat16
_U32 = jnp.uint32
HIGH_HALF_MASK = np.uint32(0xFFFF0000)


def _layer_spec(arr, layer):
    tail = (0,) * (arr.ndim - 1)
    return pl.BlockSpec((None,) + arr.shape[1:], lambda *_: (layer,) + tail,
                        pipeline_mode=pl.Buffered(1))


def _rms(x, g):
    return x * lax.rsqrt(jnp.mean(x * x, axis=-1, keepdims=True) + EPS) * g


def _dot(a, b):
    return jnp.dot(a, b, preferred_element_type=_F32)


def _mixer_kernel(x_ref, g_ref, w_in_ref, w_pool_ref, pscale_ref, w_dw_ref, b_dw_ref,
                  ln_g_ref, ln_b_ref, w_pw_ref, w_out_ref, o_ref,
                  up_ext, cv_ext, cv_shift, cv_out, mix_ref):
    s = pl.program_id(1)
    ts = x_ref.shape[0]

    @pl.when(s == 0)
    def _():
        up_ext[0:POOL_HALO, :] = jnp.zeros((POOL_HALO, POOL_WIDTH), _F32)
        cv_ext[0:CONV_HALO, :] = jnp.zeros((CONV_HALO, CONV_WIDTH), _F32)

    @pl.when(s > 0)
    def _():
        up_ext[0:POOL_HALO, :] = up_ext[ts:ts + POOL_HALO, :]
        cv_ext[0:CONV_HALO, :] = cv_ext[ts:ts + CONV_HALO, :]

    x = x_ref[...]
    h = _rms(x, g_ref[...]).astype(_BF16)

    up_ext[POOL_HALO:POOL_HALO + ts, :] = _dot(h, w_in_ref[:, COL_POOL:COL_POOL + POOL_WIDTH])
    pos = (s * ts + 1 + lax.broadcasted_iota(jnp.int32, (ts, 1), 0)).astype(_F32)
    for g, w in enumerate(POOL_WINDOWS):
        c0 = g * POOL_GROUP_IN
        cur = up_ext[POOL_HALO:POOL_HALO + ts, c0:c0 + POOL_GROUP_IN]
        win = cur
        for j in range(1, w):
            win = win + up_ext[POOL_HALO - j:POOL_HALO - j + ts, c0:c0 + POOL_GROUP_IN]
        pooled = win / jnp.minimum(pos, float(w)) - cur
        d0 = g * POOL_GROUP_OUT
        y_pool = _dot(pooled.astype(_BF16), w_pool_ref[g]) * pscale_ref[:, d0:d0 + POOL_GROUP_OUT]
        gate = jax.nn.sigmoid(_dot(h, w_in_ref[:, COL_GATE_POOL + d0:COL_GATE_POOL + d0 + POOL_GROUP_OUT]))
        mix_ref[:, d0:d0 + POOL_GROUP_OUT] = gate * y_pool

    glu_a = _dot(h, w_in_ref[:, COL_GLU_A:COL_GLU_A + CONV_WIDTH])
    glu_b = _dot(h, w_in_ref[:, COL_GLU_B:COL_GLU_B + CONV_WIDTH])
    cv_ext[CONV_HALO:CONV_HALO + ts, :] = glu_a * jax.nn.sigmoid(glu_b)

    shift_rows = cv_shift.shape[1]
    for r in range(1, SUBLANES):
        cv_shift[r - 1] = cv_ext[r:r + shift_rows, :]

    tap0 = CONV_HALO - (CONV_KERNEL - 1)
    for c0 in range(0, CONV_WIDTH, LANES):
        for r0 in range(0, ts, CONV_ROW_CHUNK):
            acc = jnp.broadcast_to(b_dw_ref[:, c0:c0 + LANES], (CONV_ROW_CHUNK, LANES))
            for k in range(CONV_KERNEL):
                off = tap0 + k
                mis = off % SUBLANES
                row = r0 + off - mis
                if mis == 0:
                    win = cv_ext[row:row + CONV_ROW_CHUNK, c0:c0 + LANES]
                else:
                    win = cv_shift[mis - 1, row:row + CONV_ROW_CHUNK, c0:c0 + LANES]
                acc = acc + w_dw_ref[k:k + 1, c0:c0 + LANES] * win
            cv_out[r0:r0 + CONV_ROW_CHUNK, c0:c0 + LANES] = acc

    u = cv_out[...]
    mu = jnp.mean(u, axis=-1, keepdims=True)
    uc = u - mu
    var = jnp.mean(uc * uc, axis=-1, keepdims=True)
    un = uc * lax.rsqrt(var + EPS) * ln_g_ref[...] + ln_b_ref[...]
    un = un * jax.nn.sigmoid(un)
    y_conv = _dot(un.astype(_BF16), w_pw_ref[...])
    gate_conv = jax.nn.sigmoid(_dot(h, w_in_ref[:, COL_GATE_CONV:COL_GATE_CONV + D_MODEL]))
    mix = mix_ref[...] + gate_conv * y_conv

    o_ref[...] = x + _dot(mix.astype(_BF16), w_out_ref[...])


def _mixer(layer, x, g, w_in, w_pool, pscale, w_dw, b_dw, ln_g, ln_b, w_pw, w_out):
    B, S, D = x.shape
    ts = SEQ_TILE
    shift_rows = ts + CONV_HALO - SUBLANES
    tile = pl.BlockSpec((None, ts, D), lambda b, s: (b, s, 0))
    params = (g, w_in, w_pool, pscale, w_dw, b_dw, ln_g, ln_b, w_pw, w_out)
    return pl.pallas_call(
        _mixer_kernel,
        grid=(B, S // ts),
        in_specs=[tile] + [_layer_spec(p, layer) for p in params],
        out_specs=tile,
        out_shape=jax.ShapeDtypeStruct(x.shape, x.dtype),
        scratch_shapes=[pltpu.VMEM((POOL_HALO + ts, POOL_WIDTH), _F32),
                        pltpu.VMEM((CONV_HALO + ts, CONV_WIDTH), _F32),
                        pltpu.VMEM((SUBLANES - 1, shift_rows, CONV_WIDTH), _F32),
                        pltpu.VMEM((ts, CONV_WIDTH), _F32),
                        pltpu.VMEM((ts, D), _F32)],
        compiler_params=pltpu.CompilerParams(
            dimension_semantics=("parallel", "arbitrary"), vmem_limit_bytes=VMEM_LIMIT),
        name="mixer",
    )(x, *params)


def _kv_kernel(m_ref, g_ref, w_ref, k_ref, v_ref):
    m = _rms(m_ref[...], g_ref[...]).astype(_BF16)
    k_ref[...] = (_dot(m, w_ref[:, 0:D_MODEL]) * (XHEAD_DIM ** -0.5)).astype(_BF16)
    v_ref[...] = _dot(m, w_ref[:, D_MODEL:2 * D_MODEL]).astype(_BF16)


def _kv_proj(layer, mem, g, w_kv):
    B, M, D = mem.shape
    blk = pl.BlockSpec((None, M, D), lambda b: (b, 0, 0))
    return pl.pallas_call(
        _kv_kernel,
        grid=(B,),
        in_specs=[blk, _layer_spec(g, layer), _layer_spec(w_kv, layer)],
        out_specs=[blk, blk],
        out_shape=[jax.ShapeDtypeStruct(mem.shape, _BF16)] * 2,
        compiler_params=pltpu.CompilerParams(
            dimension_semantics=("parallel",), vmem_limit_bytes=VMEM_LIMIT),
        name="kv_proj",
    )(mem, g, w_kv)


def _xattn_kernel(x_ref, g_ref, wq_ref, k_ref, v_ref, wo_ref, o_ref, att_ref):
    x = x_ref[...]
    h = _rms(x, g_ref[...]).astype(_BF16)
    q = _dot(h, wq_ref[...]).astype(_BF16)
    for hd in range(N_XHEADS):
        c0 = hd * XHEAD_DIM
        sc = lax.dot_general(q[:, c0:c0 + XHEAD_DIM], k_ref[:, c0:c0 + XHEAD_DIM],
                             (((1,), (1,)), ((), ())), preferred_element_type=_F32)
        p = jnp.exp(sc - jnp.max(sc, axis=-1, keepdims=True))
        denom = jnp.sum(p, axis=-1, keepdims=True)
        att = _dot(p.astype(_BF16), v_ref[:, c0:c0 + XHEAD_DIM]) / denom
        att_ref[:, c0:c0 + XHEAD_DIM] = att.astype(_BF16)
    o_ref[...] = x + _dot(att_ref[...], wo_ref[...])


def _xattn(layer, x, g, w_q, k, v, w_o):
    B, S, D = x.shape
    M = k.shape[1]
    ts = SEQ_TILE
    tile = pl.BlockSpec((None, ts, D), lambda b, s: (b, s, 0))
    kv_blk = pl.BlockSpec((None, M, D), lambda b, s: (b, 0, 0))
    return pl.pallas_call(
        _xattn_kernel,
        grid=(B, S // ts),
        in_specs=[tile, _layer_spec(g, layer), _layer_spec(w_q, layer), kv_blk, kv_blk,
                  _layer_spec(w_o, layer)],
        out_specs=tile,
        out_shape=jax.ShapeDtypeStruct(x.shape, x.dtype),
        scratch_shapes=[pltpu.VMEM((ts, D), _BF16)],
        compiler_params=pltpu.CompilerParams(
            dimension_semantics=("parallel", "parallel"), vmem_limit_bytes=VMEM_LIMIT),
        name="xattn",
    )(x, g, w_q, k, v, w_o)


def _ffn_kernel(x_ref, g_ref, w_gu_ref, w_down_ref, o_ref):
    x = x_ref[...]
    h = _rms(x, g_ref[...]).astype(_BF16)
    acc = x
    for c0 in range(0, D_FF, FF_CHUNK):
        gate = _dot(h, w_gu_ref[:, c0:c0 + FF_CHUNK])
        up = _dot(h, w_gu_ref[:, D_FF + c0:D_FF + c0 + FF_CHUNK])
        act = (gate * jax.nn.sigmoid(gate) * up).astype(_BF16)
        acc = acc + _dot(act, w_down_ref[c0:c0 + FF_CHUNK, :])
    o_ref[...] = acc


def _ffn(layer, ffn_layer, x2, g, w_gu, w_down):
    T, D = x2.shape
    tm = TOK_TILE
    tile = pl.BlockSpec((tm, D), lambda i: (i, 0))
    return pl.pallas_call(
        _ffn_kernel,
        grid=(T // tm,),
        in_specs=[tile, _layer_spec(g, layer), _layer_spec(w_gu, ffn_layer),
                  _layer_spec(w_down, ffn_layer)],
        out_specs=tile,
        out_shape=jax.ShapeDtypeStruct(x2.shape, x2.dtype),
        compiler_params=pltpu.CompilerParams(
            dimension_semantics=("parallel",), vmem_limit_bytes=VMEM_LIMIT),
        name="ffn",
    )(x2, g, w_gu, w_down)


def _pack_halves(y):
    half = y.shape[1] // 2
    lo = lax.bitcast_convert_type(y[:, :half].astype(_BF16).astype(_F32), _U32)
    hi = lax.bitcast_convert_type(y[:, half:].astype(_BF16).astype(_F32), _U32)
    return (lo >> 16) | (hi & HIGH_HALF_MASK)


def _unpack_halves(p):
    lo = lax.bitcast_convert_type(p << 16, _F32)
    hi = lax.bitcast_convert_type(p & HIGH_HALF_MASK, _F32)
    return lo, hi


def _route_kernel(x_ref, g_ref, w_r_ref, b_r_ref, hp_ref, meta_ref, cnt_ref, carry_ref):
    i = pl.program_id(0)
    tm = x_ref.shape[0]

    @pl.when(i == 0)
    def _():
        carry_ref[...] = jnp.zeros_like(carry_ref)

    hf = _rms(x_ref[...], g_ref[...])
    hp_ref[...] = _pack_halves(hf)

    lane = lax.broadcasted_iota(jnp.int32, (tm, LANES), 1)
    logits = jnp.dot(hf, w_r_ref[...], preferred_element_type=_F32,
                     precision=lax.Precision.HIGHEST) + b_r_ref[...]
    v1 = jnp.max(logits, axis=-1, keepdims=True)
    i1 = jnp.min(jnp.where(logits == v1, lane, LANES), axis=-1, keepdims=True)
    rest = jnp.where(lane == i1, NEG_BIG, logits)
    v2 = jnp.max(rest, axis=-1, keepdims=True)
    i2 = jnp.min(jnp.where(rest == v2, lane, LANES), axis=-1, keepdims=True)
    z = jnp.exp(v2 - v1)
    g1 = 1.0 / (1.0 + z)
    g2 = z / (1.0 + z)

    sel1 = lane == i1
    sel2 = lane == i2
    chosen = jnp.where(sel1 | sel2, 1.0, 0.0)
    earlier = (lax.broadcasted_iota(jnp.int32, (tm, tm), 0)
               > lax.broadcasted_iota(jnp.int32, (tm, tm), 1))
    before = _dot(jnp.where(earlier, 1.0, 0.0).astype(_BF16), chosen.astype(_BF16)) + carry_ref[...]
    r1 = jnp.sum(jnp.where(sel1, before, 0.0), axis=-1, keepdims=True)
    r2 = jnp.sum(jnp.where(sel2, before, 0.0), axis=-1, keepdims=True)
    carry_ref[...] += jnp.sum(chosen, axis=0, keepdims=True)
    cnt_ref[...] = carry_ref[...]

    meta = jnp.where(lane == META_G1, g1, 0.0)
    meta = jnp.where(lane == META_G2, g2, meta)
    meta = jnp.where(lane == META_E1, i1.astype(_F32), meta)
    meta = jnp.where(lane == META_E2, i2.astype(_F32), meta)
    meta = jnp.where(lane == META_R1, r1, meta)
    meta = jnp.where(lane == META_R2, r2, meta)
    meta_ref[...] = meta


def _route(layer, moe_layer, x2, g, w_r, b_r):
    T, D = x2.shape
    tm = ROUTE_TILE
    return pl.pallas_call(
        _route_kernel,
        grid=(T // tm,),
        in_specs=[pl.BlockSpec((tm, D), lambda i: (i, 0)), _layer_spec(g, layer),
                  _layer_spec(w_r, moe_layer), _layer_spec(b_r, moe_layer)],
        out_specs=[pl.BlockSpec((tm, D // 2), lambda i: (i, 0)),
                   pl.BlockSpec((tm, LANES), lambda i: (i, 0)),
                   pl.BlockSpec((1, LANES), lambda i: (0, 0))],
        out_shape=[jax.ShapeDtypeStruct((T, D // 2), _U32),
                   jax.ShapeDtypeStruct((T, LANES), _F32),
                   jax.ShapeDtypeStruct((1, LANES), _F32)],
        scratch_shapes=[pltpu.VMEM((1, LANES), _F32)],
        compiler_params=pltpu.CompilerParams(
            dimension_semantics=("arbitrary",), vmem_limit_bytes=VMEM_LIMIT),
        name="moe_route",
    )(x2, g, w_r, b_r)


def _row_copy(src_ref, src_row, dst_ref, dst_row, sem):
    return pltpu.make_async_copy(src_ref.at[pl.ds(src_row, 1)], dst_ref.at[pl.ds(dst_row, 1)], sem)


def _dispatch_kernel(fill_row_ref, fill_on_ref, pos1_ref, pos2_ref, hp_ref, hs_ref,
                     zero_buf, row_sem, fill_sem):
    tm = hp_ref.shape[0]
    rt = zero_buf.shape[0]

    @pl.when(pl.program_id(0) == 0)
    def _():
        zero_buf[...] = jnp.zeros_like(zero_buf)

        def fill(j):
            row = pl.multiple_of(fill_row_ref[j], rt)
            return pltpu.make_async_copy(zero_buf, hs_ref.at[pl.ds(row, rt)], fill_sem)

        for j in range(fill_row_ref.shape[0]):
            @pl.when(fill_on_ref[j] != 0)
            def _():
                fill(j).start()
        for j in range(fill_row_ref.shape[0]):
            @pl.when(fill_on_ref[j] != 0)
            def _():
                fill(j).wait()

    def issue(t, carry):
        _row_copy(hp_ref, t, hs_ref, pos1_ref[0, t], row_sem).start()
        _row_copy(hp_ref, t, hs_ref, pos2_ref[0, t], row_sem).start()
        return carry

    lax.fori_loop(0, tm, issue, 0, unroll=DMA_UNROLL)
    for _ in range(TOP_K):
        pltpu.make_async_copy(hp_ref, hs_ref.at[pl.ds(0, tm)], row_sem).wait()


def _dispatch(fill_row, fill_on, pos1, pos2, hp, n_rows):
    T, W = hp.shape
    tm = DISPATCH_TILE
    smem_blk = pl.BlockSpec((None, 1, tm), lambda i, *_: (i, 0, 0), memory_space=pltpu.SMEM)
    grid_spec = pltpu.PrefetchScalarGridSpec(
        num_scalar_prefetch=2,
        grid=(T // tm,),
        in_specs=[smem_blk, smem_blk, pl.BlockSpec((tm, W), lambda i, *_: (i, 0))],
        out_specs=pl.BlockSpec(memory_space=pl.ANY),
        scratch_shapes=[pltpu.VMEM((EXPERT_ROW_TILE, W), _U32),
                        pltpu.SemaphoreType.DMA, pltpu.SemaphoreType.DMA],
    )
    return pl.pallas_call(
        _dispatch_kernel,
        grid_spec=grid_spec,
        out_shape=jax.ShapeDtypeStruct((n_rows, W), hp.dtype),
        compiler_params=pltpu.CompilerParams(
            dimension_semantics=("arbitrary",), vmem_limit_bytes=VMEM_LIMIT),
        name="moe_dispatch",
    )(fill_row, fill_on, pos1.reshape(T // tm, 1, tm), pos2.reshape(T // tm, 1, tm), hp)


def _expert_kernel(te_ref, tv_ref, hs_ref, w_gu_ref, w_down_ref, ys_ref):
    del te_ref
    i = pl.program_id(0)
    half = D_MODEL // 2

    @pl.when(tv_ref[i] == 0)
    def _():
        ys_ref[...] = jnp.zeros_like(ys_ref)

    @pl.when(tv_ref[i] != 0)
    def _():
        lo, hi = _unpack_halves(hs_ref[...])
        lo = lo.astype(_BF16)
        hi = hi.astype(_BF16)
        acc = None
        for c0 in range(0, D_FF, FF_CHUNK):
            gate = (_dot(lo, w_gu_ref[0:half, c0:c0 + FF_CHUNK])
                    + _dot(hi, w_gu_ref[half:, c0:c0 + FF_CHUNK]))
            up = (_dot(lo, w_gu_ref[0:half, D_FF + c0:D_FF + c0 + FF_CHUNK])
                  + _dot(hi, w_gu_ref[half:, D_FF + c0:D_FF + c0 + FF_CHUNK]))
            act = (gate * jax.nn.sigmoid(gate) * up).astype(_BF16)
            part = _dot(act, w_down_ref[c0:c0 + FF_CHUNK, :])
            acc = part if acc is None else acc + part
        ys_ref[...] = _pack_halves(acc)


def _experts(moe_layer, tile_expert, tile_valid, hs, w_gu, w_down):
    n_rows, W = hs.shape
    rt = EXPERT_ROW_TILE
    D = w_gu.shape[2]
    grid_spec = pltpu.PrefetchScalarGridSpec(
        num_scalar_prefetch=2,
        grid=(n_rows // rt,),
        in_specs=[pl.BlockSpec((rt, W), lambda i, te, tv: (i, 0)),
                  pl.BlockSpec((None, None, D, 2 * D_FF), lambda i, te, tv: (moe_layer, te[i], 0, 0)),
                  pl.BlockSpec((None, None, D_FF, D), lambda i, te, tv: (moe_layer, te[i], 0, 0))],
        out_specs=pl.BlockSpec((rt, W), lambda i, te, tv: (i, 0)),
    )
    return pl.pallas_call(
        _expert_kernel,
        grid_spec=grid_spec,
        out_shape=jax.ShapeDtypeStruct((n_rows, W), hs.dtype),
        compiler_params=pltpu.CompilerParams(
            dimension_semantics=("arbitrary",), vmem_limit_bytes=VMEM_LIMIT),
        name="moe_experts",
    )(tile_expert, tile_valid, hs, w_gu, w_down)


def _combine_kernel(pos1_ref, pos2_ref, x_ref, meta_ref, g_final_ref, ys_ref, o_ref,
                    buf1, buf2, sem, *, final_norm):
    tm = x_ref.shape[0]
    half = x_ref.shape[1] // 2

    def issue(t, carry):
        _row_copy(ys_ref, pos1_ref[0, t], buf1, t, sem).start()
        _row_copy(ys_ref, pos2_ref[0, t], buf2, t, sem).start()
        return carry

    lax.fori_loop(0, tm, issue, 0, unroll=DMA_UNROLL)
    for buf in (buf1, buf2):
        pltpu.make_async_copy(ys_ref.at[pl.ds(0, tm)], buf, sem).wait()

    g1 = meta_ref[:, META_G1:META_G1 + 1]
    g2 = meta_ref[:, META_G2:META_G2 + 1]
    lo1, hi1 = _unpack_halves(buf1[...])
    lo2, hi2 = _unpack_halves(buf2[...])
    out_lo = x_ref[:, :half] + g1 * lo1 + g2 * lo2
    out_hi = x_ref[:, half:] + g1 * hi1 + g2 * hi2
    if final_norm:
        ms = (jnp.sum(out_lo * out_lo, axis=-1, keepdims=True)
              + jnp.sum(out_hi * out_hi, axis=-1, keepdims=True)) * (1.0 / x_ref.shape[1])
        scale = lax.rsqrt(ms + EPS)
        out_lo = out_lo * scale * g_final_ref[:, :half]
        out_hi = out_hi * scale * g_final_ref[:, half:]
    o_ref[:, :half] = out_lo
    o_ref[:, half:] = out_hi


def _combine(pos1, pos2, x2, meta, ys, g_final, final_norm):
    T, D = x2.shape
    tm = DISPATCH_TILE
    smem_blk = pl.BlockSpec((None, 1, tm), lambda i: (i, 0, 0), memory_space=pltpu.SMEM)
    tile = pl.BlockSpec((tm, D), lambda i: (i, 0))
    return pl.pallas_call(
        functools.partial(_combine_kernel, final_norm=final_norm),
        grid=(T // tm,),
        in_specs=[smem_blk, smem_blk, tile, pl.BlockSpec((tm, LANES), lambda i: (i, 0)),
                  pl.BlockSpec((1, D), lambda i: (0, 0)), pl.BlockSpec(memory_space=pl.ANY)],
        out_specs=tile,
        out_shape=jax.ShapeDtypeStruct(x2.shape, x2.dtype),
        scratch_shapes=[pltpu.VMEM((tm, D // 2), _U32), pltpu.VMEM((tm, D // 2), _U32),
                        pltpu.SemaphoreType.DMA],
        compiler_params=pltpu.CompilerParams(
            dimension_semantics=("arbitrary",), vmem_limit_bytes=VMEM_LIMIT),
        name="moe_combine",
    )(pos1.reshape(T // tm, 1, tm), pos2.reshape(T // tm, 1, tm), x2, meta, g_final, ys)


def _moe(layer, moe_layer, x2, g, w_r, b_r, w_gu, w_down, g_final, final_norm):
    T, D = x2.shape
    E = w_gu.shape[1]
    rt = EXPERT_ROW_TILE
    n_tiles = (TOP_K * T) // rt + E
    hp, meta, counts = _route(layer, moe_layer, x2, g, w_r, b_r)

    counts = counts[0, :E].astype(jnp.int32)
    group_tiles = (counts + rt - 1) // rt
    tile_end = jnp.cumsum(group_tiles)
    base = (tile_end - group_tiles) * rt
    experts = jnp.arange(E, dtype=jnp.int32)

    def position(e_lane, r_lane):
        e = meta[:, e_lane].astype(jnp.int32)
        return jnp.sum(jnp.where(e[:, None] == experts[None, :], base[None, :], 0), axis=1) \
            + meta[:, r_lane].astype(jnp.int32)

    pos1 = position(META_E1, META_R1)
    pos2 = position(META_E2, META_R2)
    tile_ids = jnp.arange(n_tiles, dtype=jnp.int32)
    tile_expert = jnp.minimum(
        jnp.sum((tile_ids[:, None] >= tile_end[None, :]).astype(jnp.int32), axis=1), E - 1)
    tile_valid = (tile_ids < tile_end[E - 1]).astype(jnp.int32)
    trailing = n_tiles - 1 - experts
    fill_on = jnp.concatenate([group_tiles > 0, trailing >= tile_end[E - 1]]).astype(jnp.int32)
    fill_row = jnp.concatenate([jnp.maximum(tile_end - 1, 0), trailing]) * rt

    hs = _dispatch(fill_row, fill_on, pos1, pos2, hp, n_tiles * rt)
    ys = _experts(moe_layer, tile_expert, tile_valid, hs, w_gu, w_down)
    return _combine(pos1, pos2, x2, meta, ys, g_final, final_norm)


def _final_norm_kernel(x_ref, g_ref, o_ref):
    o_ref[...] = _rms(x_ref[...], g_ref[...])


def _final_norm(x2, g):
    T, D = x2.shape
    tm = TOK_TILE
    tile = pl.BlockSpec((tm, D), lambda i: (i, 0))
    return pl.pallas_call(
        _final_norm_kernel,
        grid=(T // tm,),
        in_specs=[tile, pl.BlockSpec((1, D), lambda i: (0, 0))],
        out_specs=tile,
        out_shape=jax.ShapeDtypeStruct(x2.shape, x2.dtype),
        compiler_params=pltpu.CompilerParams(dimension_semantics=("parallel",)),
        name="final_norm",
    )(x2, g)


def kernel(x, mem, g_mix, w_in, w_pool, pool_scale, w_dw, b_dw, conv_ln_g, conv_ln_b, w_conv_out, w_mix_out, g_xattn, g_mem, w_xq, w_xkv, w_xo, g_ffn, w_ffn_gu, w_ffn_down, w_router, b_router, w_moe_gu, w_moe_down, g_final):
    B, S, D = x.shape
    depth = g_mix.shape[0]
    bf = lambda a: a.astype(_BF16)
    rows = lambda a: a.reshape(a.shape[0], 1, a.shape[1])

    g_mix, pool_scale, b_dw, conv_ln_g, conv_ln_b = map(rows, (g_mix, pool_scale, b_dw, conv_ln_g, conv_ln_b))
    g_xattn, g_mem, g_ffn = map(rows, (g_xattn, g_mem, g_ffn))
    w_dw = w_dw.reshape(depth, CONV_KERNEL, CONV_WIDTH)
    w_in, w_pool, w_conv_out, w_mix_out = map(bf, (w_in, w_pool, w_conv_out, w_mix_out))
    w_xq, w_xkv, w_xo = map(bf, (w_xq, w_xkv, w_xo))
    w_ffn_gu, w_ffn_down, w_moe_gu, w_moe_down = map(bf, (w_ffn_gu, w_ffn_down, w_moe_gu, w_moe_down))
    w_r_pad = jnp.pad(w_router, ((0, 0), (0, 0), (0, LANES - N_EXPERTS)))
    b_r_pad = rows(jnp.pad(b_router, ((0, 0), (0, LANES - N_EXPERTS)), constant_values=NEG_BIG))
    g_final = g_final.reshape(1, D)

    for i in range(depth):
        x = _mixer(i, x, g_mix, w_in, w_pool, pool_scale, w_dw, b_dw, conv_ln_g, conv_ln_b,
                   w_conv_out, w_mix_out)
        k, v = _kv_proj(i, mem, g_mem, w_xkv)
        x = _xattn(i, x, g_xattn, w_xq, k, v, w_xo)
        x2 = x.reshape(B * S, D)
        last = i == depth - 1
        if i % 2 == 0:
            x2 = _ffn(i, i // 2, x2, g_ffn, w_ffn_gu, w_ffn_down)
            if last:
                x2 = _final_norm(x2, g_final)
        else:
            x2 = _moe(i, i // 2, x2, g_ffn, w_r_pad, b_r_pad, w_moe_gu, w_moe_down, g_final, last)
        x = x2.reshape(B, S, D)
    return x
```

```python
import functools

import jax
import jax.numpy as jnp
import numpy as np
from jax import lax
from jax.experimental import pallas as pl
from jax.experimental.pallas import tpu as pltpu

D_MODEL = 1024
POOL_WIDTH = 512
POOL_GROUPS = 4
POOL_WINDOWS = (2, 4, 8, 16)
POOL_GROUP_IN = POOL_WIDTH // POOL_GROUPS
POOL_GROUP_OUT = D_MODEL // POOL_GROUPS
CONV_WIDTH = 512
CONV_KERNEL = 31
IN_COLS = POOL_WIDTH + 2 * CONV_WIDTH + 2 * D_MODEL
N_XHEADS = 4
XHEAD_DIM = D_MODEL // N_XHEADS
D_FF = 2816
N_EXPERTS = 8
TOP_K = 2
EPS = 1e-6

COL_POOL = 0
COL_GLU_A = POOL_WIDTH
COL_GLU_B = POOL_WIDTH + CONV_WIDTH
COL_GATE_POOL = POOL_WIDTH + 2 * CONV_WIDTH
COL_GATE_CONV = COL_GATE_POOL + D_MODEL

LANES = 128
SUBLANES = 8
POOL_HALO = 16
CONV_HALO = 32
SEQ_TILE = 512
TOK_TILE = 512
MXU_DIM = 256
FF_CHUNKS = ((0, 6 * MXU_DIM), (6 * MXU_DIM, 5 * MXU_DIM))
assert sum(n for _, n in FF_CHUNKS) == D_FF
CONV_ROW_CHUNK = 64
ROUTE_TILE = 512
DISPATCH_TILE = 512
EXPERT_ROW_TILE = 512
DMA_UNROLL = 8
VMEM_LIMIT = 56 * 1024 * 1024
NEG_BIG = -1e30
META_G1, META_G2, META_E1, META_E2, META_R1, META_R2 = range(6)

_F32 = jnp.float32
_BF16 = jnp.bfloat16
_U32 = jnp.uint32
HIGH_HALF_MASK = np.uint32(0xFFFF0000)


def _layer_spec(arr, layer):
    tail = (0,) * (arr.ndim - 1)
    return pl.BlockSpec((None,) + arr.shape[1:], lambda *_: (layer,) + tail,
                        pipeline_mode=pl.Buffered(1))


def _rms(x, g):
    return x * lax.rsqrt(jnp.mean(x * x, axis=-1, keepdims=True) + EPS) * g


def _dot(a, b):
    return jnp.dot(a, b, preferred_element_type=_F32)


def _mixer_kernel(x_ref, g_ref, w_in_ref, w_pool_ref, pscale_ref, w_dw_ref, b_dw_ref,
                  ln_g_ref, ln_b_ref, w_pw_ref, w_out_ref, o_ref,
                  up_ext, cv_ext, cv_shift, cv_out, mix_ref, gate_ref):
    s = pl.program_id(1)
    ts = x_ref.shape[0]

    @pl.when(s == 0)
    def _():
        up_ext[0:POOL_HALO, :] = jnp.zeros((POOL_HALO, POOL_WIDTH), _F32)
        cv_ext[0:CONV_HALO, :] = jnp.zeros((CONV_HALO, CONV_WIDTH), _F32)

    @pl.when(s > 0)
    def _():
        up_ext[0:POOL_HALO, :] = up_ext[ts:ts + POOL_HALO, :]
        cv_ext[0:CONV_HALO, :] = cv_ext[ts:ts + CONV_HALO, :]

    x = x_ref[...]
    h = _rms(x, g_ref[...]).astype(_BF16)

    glu_a = _dot(h, w_in_ref[:, COL_GLU_A:COL_GLU_A + CONV_WIDTH])
    glu_b = _dot(h, w_in_ref[:, COL_GLU_B:COL_GLU_B + CONV_WIDTH])
    cv_ext[CONV_HALO:CONV_HALO + ts, :] = glu_a * jax.nn.sigmoid(glu_b)

    shift_rows = cv_shift.shape[1]
    for r in range(1, SUBLANES):
        cv_shift[r - 1] = cv_ext[r:r + shift_rows, :]

    tap0 = CONV_HALO - (CONV_KERNEL - 1)
    n_blocks = CONV_WIDTH // LANES
    gate_cols = 2 * D_MODEL // n_blocks
    for blk in range(n_blocks):
        c0 = blk * LANES
        d0 = COL_GATE_POOL + blk * gate_cols
        gate_ref[:, blk * gate_cols:(blk + 1) * gate_cols] = jax.nn.sigmoid(
            _dot(h, w_in_ref[:, d0:d0 + gate_cols]))
        for r0 in range(0, ts, CONV_ROW_CHUNK):
            acc = jnp.broadcast_to(b_dw_ref[:, c0:c0 + LANES], (CONV_ROW_CHUNK, LANES))
            for k in range(CONV_KERNEL):
                off = tap0 + k
                mis = off % SUBLANES
                row = r0 + off - mis
                if mis == 0:
                    win = cv_ext[row:row + CONV_ROW_CHUNK, c0:c0 + LANES]
                else:
                    win = cv_shift[mis - 1, row:row + CONV_ROW_CHUNK, c0:c0 + LANES]
                acc = acc + w_dw_ref[k:k + 1, c0:c0 + LANES] * win
            cv_out[r0:r0 + CONV_ROW_CHUNK, c0:c0 + LANES] = acc

    up_ext[POOL_HALO:POOL_HALO + ts, :] = _dot(h, w_in_ref[:, COL_POOL:COL_POOL + POOL_WIDTH])
    pos = (s * ts + 1 + lax.broadcasted_iota(jnp.int32, (ts, 1), 0)).astype(_F32)
    for g, w in enumerate(POOL_WINDOWS):
        c0 = g * POOL_GROUP_IN
        win = up_ext[:, c0:c0 + POOL_GROUP_IN]
        span = 1
        while span < w:
            win = win + pltpu.roll(win, span, 0)
            span *= 2
        cur = up_ext[POOL_HALO:POOL_HALO + ts, c0:c0 + POOL_GROUP_IN]
        pooled = win[POOL_HALO:, :] / jnp.minimum(pos, float(w)) - cur
        d0 = g * POOL_GROUP_OUT
        mix_ref[:, d0:d0 + POOL_GROUP_OUT] = (
            _dot(pooled.astype(_BF16), w_pool_ref[g]) * pscale_ref[:, d0:d0 + POOL_GROUP_OUT])

    u = cv_out[...]
    mu = jnp.mean(u, axis=-1, keepdims=True)
    uc = u - mu
    var = jnp.mean(uc * uc, axis=-1, keepdims=True)
    un = uc * lax.rsqrt(var + EPS) * ln_g_ref[...] + ln_b_ref[...]
    un = un * jax.nn.sigmoid(un)
    y_conv = _dot(un.astype(_BF16), w_pw_ref[...])
    mix = gate_ref[:, 0:D_MODEL] * mix_ref[...] + gate_ref[:, D_MODEL:] * y_conv

    o_ref[...] = x + _dot(mix.astype(_BF16), w_out_ref[...])


def _mixer(layer, x, g, w_in, w_pool, pscale, w_dw, b_dw, ln_g, ln_b, w_pw, w_out):
    B, S, D = x.shape
    ts = SEQ_TILE
    shift_rows = ts + CONV_HALO - SUBLANES
    tile = pl.BlockSpec((None, ts, D), lambda b, s: (b, s, 0))
    params = (g, w_in, w_pool, pscale, w_dw, b_dw, ln_g, ln_b, w_pw, w_out)
    return pl.pallas_call(
        _mixer_kernel,
        grid=(B, S // ts),
        in_specs=[tile] + [_layer_spec(p, layer) for p in params],
        out_specs=tile,
        out_shape=jax.ShapeDtypeStruct(x.shape, x.dtype),
        scratch_shapes=[pltpu.VMEM((POOL_HALO + ts, POOL_WIDTH), _F32),
                        pltpu.VMEM((CONV_HALO + ts, CONV_WIDTH), _F32),
                        pltpu.VMEM((SUBLANES - 1, shift_rows, CONV_WIDTH), _F32),
                        pltpu.VMEM((ts, CONV_WIDTH), _F32),
                        pltpu.VMEM((ts, D), _F32),
                        pltpu.VMEM((ts, 2 * D), _F32)],
        compiler_params=pltpu.CompilerParams(
            dimension_semantics=("parallel", "arbitrary"), vmem_limit_bytes=VMEM_LIMIT),
        name="mixer",
    )(x, *params)


def _kv_kernel(m_ref, g_ref, w_ref, k_ref, v_ref):
    m = _rms(m_ref[...], g_ref[...]).astype(_BF16)
    k_ref[...] = (_dot(m, w_ref[:, 0:D_MODEL]) * (XHEAD_DIM ** -0.5)).astype(_BF16)
    v_ref[...] = _dot(m, w_ref[:, D_MODEL:2 * D_MODEL]).astype(_BF16)


def _kv_proj(layer, mem, g, w_kv):
    B, M, D = mem.shape
    blk = pl.BlockSpec((None, M, D), lambda b: (b, 0, 0))
    return pl.pallas_call(
        _kv_kernel,
        grid=(B,),
        in_specs=[blk, _layer_spec(g, layer), _layer_spec(w_kv, layer)],
        out_specs=[blk, blk],
        out_shape=[jax.ShapeDtypeStruct(mem.shape, _BF16)] * 2,
        compiler_params=pltpu.CompilerParams(
            dimension_semantics=("parallel",), vmem_limit_bytes=VMEM_LIMIT),
        name="kv_proj",
    )(mem, g, w_kv)


def _xattn_kernel(x_ref, g_ref, wq_ref, k_ref, v_ref, wo_ref, o_ref, att_ref):
    x = x_ref[...]
    h = _rms(x, g_ref[...]).astype(_BF16)
    q = _dot(h, wq_ref[...]).astype(_BF16)
    for hd in range(N_XHEADS):
        c0 = hd * XHEAD_DIM
        sc = lax.dot_general(q[:, c0:c0 + XHEAD_DIM], k_ref[:, c0:c0 + XHEAD_DIM],
                             (((1,), (1,)), ((), ())), preferred_element_type=_F32)
        p = jnp.exp(sc - jnp.max(sc, axis=-1, keepdims=True))
        denom = jnp.sum(p, axis=-1, keepdims=True)
        att = _dot(p.astype(_BF16), v_ref[:, c0:c0 + XHEAD_DIM]) / denom
        att_ref[:, c0:c0 + XHEAD_DIM] = att.astype(_BF16)
    o_ref[...] = x + _dot(att_ref[...], wo_ref[...])


def _xattn(layer, x, g, w_q, k, v, w_o):
    B, S, D = x.shape
    M = k.shape[1]
    ts = SEQ_TILE
    tile = pl.BlockSpec((None, ts, D), lambda b, s: (b, s, 0))
    kv_blk = pl.BlockSpec((None, M, D), lambda b, s: (b, 0, 0))
    return pl.pallas_call(
        _xattn_kernel,
        grid=(B, S // ts),
        in_specs=[tile, _layer_spec(g, layer), _layer_spec(w_q, layer), kv_blk, kv_blk,
                  _layer_spec(w_o, layer)],
        out_specs=tile,
        out_shape=jax.ShapeDtypeStruct(x.shape, x.dtype),
        scratch_shapes=[pltpu.VMEM((ts, D), _BF16)],
        compiler_params=pltpu.CompilerParams(
            dimension_semantics=("parallel", "parallel"), vmem_limit_bytes=VMEM_LIMIT),
        name="xattn",
    )(x, g, w_q, k, v, w_o)


def _ffn_kernel(x_ref, g_ref, w_gu_ref, w_down_ref, o_ref):
    x = x_ref[...]
    h = _rms(x, g_ref[...]).astype(_BF16)
    acc = x
    for c0, n in FF_CHUNKS:
        gate = _dot(h, w_gu_ref[:, c0:c0 + n])
        up = _dot(h, w_gu_ref[:, D_FF + c0:D_FF + c0 + n])
        act = (gate * jax.nn.sigmoid(gate) * up).astype(_BF16)
        acc = acc + _dot(act, w_down_ref[c0:c0 + n, :])
    o_ref[...] = acc


def _ffn(layer, ffn_layer, x2, g, w_gu, w_down):
    T, D = x2.shape
    tm = TOK_TILE
    tile = pl.BlockSpec((tm, D), lambda i: (i, 0))
    return pl.pallas_call(
        _ffn_kernel,
        grid=(T // tm,),
        in_specs=[tile, _layer_spec(g, layer), _layer_spec(w_gu, ffn_layer),
                  _layer_spec(w_down, ffn_layer)],
        out_specs=tile,
        out_shape=jax.ShapeDtypeStruct(x2.shape, x2.dtype),
        compiler_params=pltpu.CompilerParams(
            dimension_semantics=("parallel",), vmem_limit_bytes=VMEM_LIMIT),
        name="ffn",
    )(x2, g, w_gu, w_down)


def _pack_halves(y):
    half = y.shape[1] // 2
    lo = lax.bitcast_convert_type(y[:, :half].astype(_BF16).astype(_F32), _U32)
    hi = lax.bitcast_convert_type(y[:, half:].astype(_BF16).astype(_F32), _U32)
    return (lo >> 16) | (hi & HIGH_HALF_MASK)


def _unpack_halves(p):
    lo = lax.bitcast_convert_type(p << 16, _F32)
    hi = lax.bitcast_convert_type(p & HIGH_HALF_MASK, _F32)
    return lo, hi


def _route_kernel(x_ref, g_ref, w_r_ref, b_r_ref, earlier_ref, hp_ref, meta_ref, cnt_ref, carry_ref):
    i = pl.program_id(0)
    tm = x_ref.shape[0]

    @pl.when(i == 0)
    def _():
        carry_ref[...] = jnp.zeros_like(carry_ref)

    hf = _rms(x_ref[...], g_ref[...])
    hp_ref[...] = _pack_halves(hf)

    lane = lax.broadcasted_iota(jnp.int32, (tm, LANES), 1)
    h_hi = hf.astype(_BF16)
    h_lo = (hf - h_hi.astype(_F32)).astype(_BF16)
    w_r = w_r_ref[...]
    w_hi = w_r.astype(_BF16)
    w_lo = (w_r - w_hi.astype(_F32)).astype(_BF16)
    logits = _dot(h_hi, w_hi) + _dot(h_lo, w_hi) + _dot(h_hi, w_lo) + b_r_ref[...]
    v1 = jnp.max(logits, axis=-1, keepdims=True)
    i1 = jnp.min(jnp.where(logits == v1, lane, LANES), axis=-1, keepdims=True)
    rest = jnp.where(lane == i1, NEG_BIG, logits)
    v2 = jnp.max(rest, axis=-1, keepdims=True)
    i2 = jnp.min(jnp.where(rest == v2, lane, LANES), axis=-1, keepdims=True)
    z = jnp.exp(v2 - v1)
    g1 = 1.0 / (1.0 + z)
    g2 = z / (1.0 + z)

    sel1 = lane == i1
    sel2 = lane == i2
    chosen = jnp.where(sel1 | sel2, 1.0, 0.0)
    before = _dot(earlier_ref[...], chosen.astype(_BF16)) + carry_ref[...]
    r1 = jnp.sum(jnp.where(sel1, before, 0.0), axis=-1, keepdims=True)
    r2 = jnp.sum(jnp.where(sel2, before, 0.0), axis=-1, keepdims=True)
    carry_ref[...] += jnp.sum(chosen, axis=0, keepdims=True)
    cnt_ref[...] = carry_ref[...]

    meta = jnp.where(lane == META_G1, g1, 0.0)
    meta = jnp.where(lane == META_G2, g2, meta)
    meta = jnp.where(lane == META_E1, i1.astype(_F32), meta)
    meta = jnp.where(lane == META_E2, i2.astype(_F32), meta)
    meta = jnp.where(lane == META_R1, r1, meta)
    meta = jnp.where(lane == META_R2, r2, meta)
    meta_ref[...] = meta


def _route(layer, moe_layer, x2, g, w_r, b_r):
    T, D = x2.shape
    tm = ROUTE_TILE
    earlier = jnp.asarray(np.tril(np.ones((tm, tm), np.float32), -1), _BF16)
    return pl.pallas_call(
        _route_kernel,
        grid=(T // tm,),
        in_specs=[pl.BlockSpec((tm, D), lambda i: (i, 0)), _layer_spec(g, layer),
                  _layer_spec(w_r, moe_layer), _layer_spec(b_r, moe_layer),
                  pl.BlockSpec((tm, tm), lambda i: (0, 0), pipeline_mode=pl.Buffered(1))],
        out_specs=[pl.BlockSpec((tm, D // 2), lambda i: (i, 0)),
                   pl.BlockSpec((tm, LANES), lambda i: (i, 0)),
                   pl.BlockSpec((1, LANES), lambda i: (0, 0))],
        out_shape=[jax.ShapeDtypeStruct((T, D // 2), _U32),
                   jax.ShapeDtypeStruct((T, LANES), _F32),
                   jax.ShapeDtypeStruct((1, LANES), _F32)],
        scratch_shapes=[pltpu.VMEM((1, LANES), _F32)],
        compiler_params=pltpu.CompilerParams(
            dimension_semantics=("arbitrary",), vmem_limit_bytes=VMEM_LIMIT),
        name="moe_route",
    )(x2, g, w_r, b_r, earlier)


def _row_copy(src_ref, src_row, dst_ref, dst_row, sem):
    return pltpu.make_async_copy(src_ref.at[pl.ds(src_row, 1)], dst_ref.at[pl.ds(dst_row, 1)], sem)


def _dispatch_kernel(fill_row_ref, fill_on_ref, pos1_ref, pos2_ref, hp_ref, hs_ref,
                     zero_buf, row_sem, fill_sem):
    tm = hp_ref.shape[0]
    rt = zero_buf.shape[0]

    @pl.when(pl.program_id(0) == 0)
    def _():
        zero_buf[...] = jnp.zeros_like(zero_buf)

        def fill(j):
            row = pl.multiple_of(fill_row_ref[j], rt)
            return pltpu.make_async_copy(zero_buf, hs_ref.at[pl.ds(row, rt)], fill_sem)

        for j in range(fill_row_ref.shape[0]):
            @pl.when(fill_on_ref[j] != 0)
            def _():
                fill(j).start()
        for j in range(fill_row_ref.shape[0]):
            @pl.when(fill_on_ref[j] != 0)
            def _():
                fill(j).wait()

    def issue(t, carry):
        _row_copy(hp_ref, t, hs_ref, pos1_ref[0, t], row_sem).start()
        _row_copy(hp_ref, t, hs_ref, pos2_ref[0, t], row_sem).start()
        return carry

    lax.fori_loop(0, tm, issue, 0, unroll=DMA_UNROLL)
    for _ in range(TOP_K):
        pltpu.make_async_copy(hp_ref, hs_ref.at[pl.ds(0, tm)], row_sem).wait()


def _dispatch(fill_row, fill_on, pos1, pos2, hp, n_rows):
    T, W = hp.shape
    tm = DISPATCH_TILE
    smem_blk = pl.BlockSpec((None, 1, tm), lambda i, *_: (i, 0, 0), memory_space=pltpu.SMEM)
    grid_spec = pltpu.PrefetchScalarGridSpec(
        num_scalar_prefetch=2,
        grid=(T // tm,),
        in_specs=[smem_blk, smem_blk, pl.BlockSpec((tm, W), lambda i, *_: (i, 0))],
        out_specs=pl.BlockSpec(memory_space=pl.ANY),
        scratch_shapes=[pltpu.VMEM((EXPERT_ROW_TILE, W), _U32),
                        pltpu.SemaphoreType.DMA, pltpu.SemaphoreType.DMA],
    )
    return pl.pallas_call(
        _dispatch_kernel,
        grid_spec=grid_spec,
        out_shape=jax.ShapeDtypeStruct((n_rows, W), hp.dtype),
        compiler_params=pltpu.CompilerParams(
            dimension_semantics=("arbitrary",), vmem_limit_bytes=VMEM_LIMIT),
        name="moe_dispatch",
    )(fill_row, fill_on, pos1.reshape(T // tm, 1, tm), pos2.reshape(T // tm, 1, tm), hp)


def _expert_kernel(te_ref, tv_ref, hs_ref, w_gu_ref, w_down_ref, ys_ref):
    del te_ref
    i = pl.program_id(0)

    @pl.when(tv_ref[i] == 0)
    def _():
        ys_ref[...] = jnp.zeros_like(ys_ref)

    @pl.when(tv_ref[i] != 0)
    def _():
        lo, hi = _unpack_halves(hs_ref[...])
        h = jnp.concatenate([lo.astype(_BF16), hi.astype(_BF16)], axis=1)
        acc = None
        for c0, n in FF_CHUNKS:
            gate = _dot(h, w_gu_ref[:, c0:c0 + n])
            up = _dot(h, w_gu_ref[:, D_FF + c0:D_FF + c0 + n])
            act = (gate * jax.nn.sigmoid(gate) * up).astype(_BF16)
            part = _dot(act, w_down_ref[c0:c0 + n, :])
            acc = part if acc is None else acc + part
        ys_ref[...] = _pack_halves(acc)


def _experts(moe_layer, tile_expert, tile_valid, hs, w_gu, w_down):
    n_rows, W = hs.shape
    rt = EXPERT_ROW_TILE
    D = w_gu.shape[2]
    grid_spec = pltpu.PrefetchScalarGridSpec(
        num_scalar_prefetch=2,
        grid=(n_rows // rt,),
        in_specs=[pl.BlockSpec((rt, W), lambda i, te, tv: (i, 0)),
                  pl.BlockSpec((None, None, D, 2 * D_FF), lambda i, te, tv: (moe_layer, te[i], 0, 0)),
                  pl.BlockSpec((None, None, D_FF, D), lambda i, te, tv: (moe_layer, te[i], 0, 0))],
        out_specs=pl.BlockSpec((rt, W), lambda i, te, tv: (i, 0)),
    )
    return pl.pallas_call(
        _expert_kernel,
        grid_spec=grid_spec,
        out_shape=jax.ShapeDtypeStruct((n_rows, W), hs.dtype),
        compiler_params=pltpu.CompilerParams(
            dimension_semantics=("arbitrary",), vmem_limit_bytes=VMEM_LIMIT),
        name="moe_experts",
    )(tile_expert, tile_valid, hs, w_gu, w_down)


def _combine_kernel(pos1_ref, pos2_ref, x_ref, meta_ref, g_final_ref, ys_ref, o_ref,
                    buf1, buf2, sem, *, final_norm):
    tm = x_ref.shape[0]
    half = x_ref.shape[1] // 2

    def issue(t, carry):
        _row_copy(ys_ref, pos1_ref[0, t], buf1, t, sem).start()
        _row_copy(ys_ref, pos2_ref[0, t], buf2, t, sem).start()
        return carry

    lax.fori_loop(0, tm, issue, 0, unroll=DMA_UNROLL)
    for buf in (buf1, buf2):
        pltpu.make_async_copy(ys_ref.at[pl.ds(0, tm)], buf, sem).wait()

    g1 = meta_ref[:, META_G1:META_G1 + 1]
    g2 = meta_ref[:, META_G2:META_G2 + 1]
    lo1, hi1 = _unpack_halves(buf1[...])
    lo2, hi2 = _unpack_halves(buf2[...])
    out_lo = x_ref[:, :half] + g1 * lo1 + g2 * lo2
    out_hi = x_ref[:, half:] + g1 * hi1 + g2 * hi2
    if final_norm:
        ms = (jnp.sum(out_lo * out_lo, axis=-1, keepdims=True)
              + jnp.sum(out_hi * out_hi, axis=-1, keepdims=True)) * (1.0 / x_ref.shape[1])
        scale = lax.rsqrt(ms + EPS)
        out_lo = out_lo * scale * g_final_ref[:, :half]
        out_hi = out_hi * scale * g_final_ref[:, half:]
    o_ref[:, :half] = out_lo
    o_ref[:, half:] = out_hi


def _combine(pos1, pos2, x2, meta, ys, g_final, final_norm):
    T, D = x2.shape
    tm = DISPATCH_TILE
    smem_blk = pl.BlockSpec((None, 1, tm), lambda i: (i, 0, 0), memory_space=pltpu.SMEM)
    tile = pl.BlockSpec((tm, D), lambda i: (i, 0))
    return pl.pallas_call(
        functools.partial(_combine_kernel, final_norm=final_norm),
        grid=(T // tm,),
        in_specs=[smem_blk, smem_blk, tile, pl.BlockSpec((tm, LANES), lambda i: (i, 0)),
                  pl.BlockSpec((1, D), lambda i: (0, 0)), pl.BlockSpec(memory_space=pl.ANY)],
        out_specs=tile,
        out_shape=jax.ShapeDtypeStruct(x2.shape, x2.dtype),
        scratch_shapes=[pltpu.VMEM((tm, D // 2), _U32), pltpu.VMEM((tm, D // 2), _U32),
                        pltpu.SemaphoreType.DMA],
        compiler_params=pltpu.CompilerParams(
            dimension_semantics=("arbitrary",), vmem_limit_bytes=VMEM_LIMIT),
        name="moe_combine",
    )(pos1.reshape(T // tm, 1, tm), pos2.reshape(T // tm, 1, tm), x2, meta, g_final, ys)


def _moe(layer, moe_layer, x2, g, w_r, b_r, w_gu, w_down, g_final, final_norm):
    T, D = x2.shape
    E = w_gu.shape[1]
    rt = EXPERT_ROW_TILE
    n_tiles = (TOP_K * T) // rt + E
    hp, meta, counts = _route(layer, moe_layer, x2, g, w_r, b_r)

    counts = counts[0, :E].astype(jnp.int32)
    group_tiles = (counts + rt - 1) // rt
    tile_end = jnp.cumsum(group_tiles)
    base = (tile_end - group_tiles) * rt
    experts = jnp.arange(E, dtype=jnp.int32)

    def position(e_lane, r_lane):
        e = meta[:, e_lane].astype(jnp.int32)
        return jnp.sum(jnp.where(e[:, None] == experts[None, :], base[None, :], 0), axis=1) \
            + meta[:, r_lane].astype(jnp.int32)

    pos1 = position(META_E1, META_R1)
    pos2 = position(META_E2, META_R2)
    tile_ids = jnp.arange(n_tiles, dtype=jnp.int32)
    tile_expert = jnp.minimum(
        jnp.sum((tile_ids[:, None] >= tile_end[None, :]).astype(jnp.int32), axis=1), E - 1)
    tile_valid = (tile_ids < tile_end[E - 1]).astype(jnp.int32)
    trailing = n_tiles - 1 - experts
    fill_on = jnp.concatenate([group_tiles > 0, trailing >= tile_end[E - 1]]).astype(jnp.int32)
    fill_row = jnp.concatenate([jnp.maximum(tile_end - 1, 0), trailing]) * rt

    hs = _dispatch(fill_row, fill_on, pos1, pos2, hp, n_tiles * rt)
    ys = _experts(moe_layer, tile_expert, tile_valid, hs, w_gu, w_down)
    return _combine(pos1, pos2, x2, meta, ys, g_final, final_norm)


def _final_norm_kernel(x_ref, g_ref, o_ref):
    o_ref[...] = _rms(x_ref[...], g_ref[...])


def _final_norm(x2, g):
    T, D = x2.shape
    tm = TOK_TILE
    tile = pl.BlockSpec((tm, D), lambda i: (i, 0))
    return pl.pallas_call(
        _final_norm_kernel,
        grid=(T // tm,),
        in_specs=[tile, pl.BlockSpec((1, D), lambda i: (0, 0))],
        out_specs=tile,
        out_shape=jax.ShapeDtypeStruct(x2.shape, x2.dtype),
        compiler_params=pltpu.CompilerParams(dimension_semantics=("parallel",)),
        name="final_norm",
    )(x2, g)


def kernel(x, mem, g_mix, w_in, w_pool, pool_scale, w_dw, b_dw, conv_ln_g, conv_ln_b, w_conv_out, w_mix_out, g_xattn, g_mem, w_xq, w_xkv, w_xo, g_ffn, w_ffn_gu, w_ffn_down, w_router, b_router, w_moe_gu, w_moe_down, g_final):
    B, S, D = x.shape
    depth = g_mix.shape[0]
    bf = lambda a: a.astype(_BF16)
    rows = lambda a: a.reshape(a.shape[0], 1, a.shape[1])

    g_mix, pool_scale, b_dw, conv_ln_g, conv_ln_b = map(rows, (g_mix, pool_scale, b_dw, conv_ln_g, conv_ln_b))
    g_xattn, g_mem, g_ffn = map(rows, (g_xattn, g_mem, g_ffn))
    w_dw = w_dw.reshape(depth, CONV_KERNEL, CONV_WIDTH)
    w_in, w_pool, w_conv_out, w_mix_out = map(bf, (w_in, w_pool, w_conv_out, w_mix_out))
    w_xq, w_xkv, w_xo = map(bf, (w_xq, w_xkv, w_xo))
    w_ffn_gu, w_ffn_down, w_moe_gu, w_moe_down = map(bf, (w_ffn_gu, w_ffn_down, w_moe_gu, w_moe_down))
    w_r_pad = jnp.pad(w_router, ((0, 0), (0, 0), (0, LANES - N_EXPERTS)))
    b_r_pad = rows(jnp.pad(b_router, ((0, 0), (0, LANES - N_EXPERTS)), constant_values=NEG_BIG))
    g_final = g_final.reshape(1, D)

    for i in range(depth):
        x = _mixer(i, x, g_mix, w_in, w_pool, pool_scale, w_dw, b_dw, conv_ln_g, conv_ln_b,
                   w_conv_out, w_mix_out)
        k, v = _kv_proj(i, mem, g_mem, w_xkv)
        x = _xattn(i, x, g_xattn, w_xq, k, v, w_xo)
        x2 = x.reshape(B * S, D)
        last = i == depth - 1
        if i % 2 == 0:
            x2 = _ffn(i, i // 2, x2, g_ffn, w_ffn_gu, w_ffn_down)
            if last:
                x2 = _final_norm(x2, g_final)
        else:
            x2 = _moe(i, i // 2, x2, g_ffn, w_r_pad, b_r_pad, w_moe_gu, w_moe_down, g_final, last)
        x = x2.reshape(B, S, D)
    return x
```

```python
import functools

import jax
import jax.numpy as jnp
import numpy as np
from jax import lax
from jax.experimental import pallas as pl
from jax.experimental.pallas import tpu as pltpu

D_MODEL = 1024
POOL_WIDTH = 512
POOL_GROUPS = 4
POOL_WINDOWS = (2, 4, 8, 16)
POOL_GROUP_IN = POOL_WIDTH // POOL_GROUPS
POOL_GROUP_OUT = D_MODEL // POOL_GROUPS
CONV_WIDTH = 512
CONV_KERNEL = 31
IN_COLS = POOL_WIDTH + 2 * CONV_WIDTH + 2 * D_MODEL
N_XHEADS = 4
XHEAD_DIM = D_MODEL // N_XHEADS
D_FF = 2816
N_EXPERTS = 8
TOP_K = 2
EPS = 1e-6

COL_POOL = 0
COL_GLU_A = POOL_WIDTH
COL_GLU_B = POOL_WIDTH + CONV_WIDTH
COL_GATE_POOL = POOL_WIDTH + 2 * CONV_WIDTH
COL_GATE_CONV = COL_GATE_POOL + D_MODEL

LANES = 128
SUBLANES = 8
POOL_HALO = 16
CONV_HALO = 32
SEQ_TILE = 512
TOK_TILE = 512
MXU_DIM = 256
FF_CHUNKS = ((0, 6 * MXU_DIM), (6 * MXU_DIM, 5 * MXU_DIM))
assert sum(n for _, n in FF_CHUNKS) == D_FF
CONV_ROW_CHUNK = 64
ROUTE_TILE = 512
DISPATCH_TILE = 512
EXPERT_ROW_TILE = 512
DMA_UNROLL = 8
GROUP_WINDOW = 256
VMEM_LIMIT = 56 * 1024 * 1024
NEG_BIG = -1e30
META_G1, META_G2, META_P1, META_P2 = range(4)

_F32 = jnp.float32
_BF16 = jnp.bfloat16
_U32 = jnp.uint32
HIGH_HALF_MASK = np.uint32(0xFFFF0000)


def _layer_spec(arr, layer):
    tail = (0,) * (arr.ndim - 1)
    return pl.BlockSpec((None,) + arr.shape[1:], lambda *_: (layer,) + tail,
                        pipeline_mode=pl.Buffered(1))


def _rms(x, g):
    return x * lax.rsqrt(jnp.mean(x * x, axis=-1, keepdims=True) + EPS) * g


def _dot(a, b):
    return jnp.dot(a, b, preferred_element_type=_F32)


def _mixer_kernel(x_ref, g_ref, w_in_ref, w_pool_ref, pscale_ref, w_dw_ref, b_dw_ref,
                  ln_g_ref, ln_b_ref, w_pw_ref, w_out_ref, o_ref,
                  up_ext, cv_ext, cv_shift, cv_out, mix_ref, gate_ref):
    s = pl.program_id(1)
    ts = x_ref.shape[0]

    @pl.when(s == 0)
    def _():
        up_ext[0:POOL_HALO, :] = jnp.zeros((POOL_HALO, POOL_WIDTH), _F32)
        cv_ext[0:CONV_HALO, :] = jnp.zeros((CONV_HALO, CONV_WIDTH), _F32)

    @pl.when(s > 0)
    def _():
        up_ext[0:POOL_HALO, :] = up_ext[ts:ts + POOL_HALO, :]
        cv_ext[0:CONV_HALO, :] = cv_ext[ts:ts + CONV_HALO, :]

    x = x_ref[...]
    h = _rms(x, g_ref[...]).astype(_BF16)

    glu_a = _dot(h, w_in_ref[:, COL_GLU_A:COL_GLU_A + CONV_WIDTH])
    glu_b = _dot(h, w_in_ref[:, COL_GLU_B:COL_GLU_B + CONV_WIDTH])
    cv_ext[CONV_HALO:CONV_HALO + ts, :] = glu_a * jax.nn.sigmoid(glu_b)

    shift_rows = cv_shift.shape[1]
    for r in range(1, SUBLANES):
        cv_shift[r - 1] = cv_ext[r:r + shift_rows, :]

    tap0 = CONV_HALO - (CONV_KERNEL - 1)
    n_blocks = CONV_WIDTH // LANES
    gate_cols = 2 * D_MODEL // n_blocks
    for blk in range(n_blocks):
        c0 = blk * LANES
        d0 = COL_GATE_POOL + blk * gate_cols
        gate_ref[:, blk * gate_cols:(blk + 1) * gate_cols] = jax.nn.sigmoid(
            _dot(h, w_in_ref[:, d0:d0 + gate_cols]))
        for r0 in range(0, ts, CONV_ROW_CHUNK):
            acc = jnp.broadcast_to(b_dw_ref[:, c0:c0 + LANES], (CONV_ROW_CHUNK, LANES))
            for k in range(CONV_KERNEL):
                off = tap0 + k
                mis = off % SUBLANES
                row = r0 + off - mis
                if mis == 0:
                    win = cv_ext[row:row + CONV_ROW_CHUNK, c0:c0 + LANES]
                else:
                    win = cv_shift[mis - 1, row:row + CONV_ROW_CHUNK, c0:c0 + LANES]
                acc = acc + w_dw_ref[k:k + 1, c0:c0 + LANES] * win
            cv_out[r0:r0 + CONV_ROW_CHUNK, c0:c0 + LANES] = acc

    up_ext[POOL_HALO:POOL_HALO + ts, :] = _dot(h, w_in_ref[:, COL_POOL:COL_POOL + POOL_WIDTH])
    pos = (s * ts + 1 + lax.broadcasted_iota(jnp.int32, (ts, 1), 0)).astype(_F32)
    for g, w in enumerate(POOL_WINDOWS):
        c0 = g * POOL_GROUP_IN
        win = up_ext[:, c0:c0 + POOL_GROUP_IN]
        span = 1
        while span < w:
            win = win + pltpu.roll(win, span, 0)
            span *= 2
        cur = up_ext[POOL_HALO:POOL_HALO + ts, c0:c0 + POOL_GROUP_IN]
        pooled = win[POOL_HALO:, :] / jnp.minimum(pos, float(w)) - cur
        d0 = g * POOL_GROUP_OUT
        mix_ref[:, d0:d0 + POOL_GROUP_OUT] = (
            _dot(pooled.astype(_BF16), w_pool_ref[g]) * pscale_ref[:, d0:d0 + POOL_GROUP_OUT])

    u = cv_out[...]
    mu = jnp.mean(u, axis=-1, keepdims=True)
    uc = u - mu
    var = jnp.mean(uc * uc, axis=-1, keepdims=True)
    un = uc * lax.rsqrt(var + EPS) * ln_g_ref[...] + ln_b_ref[...]
    un = un * jax.nn.sigmoid(un)
    y_conv = _dot(un.astype(_BF16), w_pw_ref[...])
    mix = gate_ref[:, 0:D_MODEL] * mix_ref[...] + gate_ref[:, D_MODEL:] * y_conv

    o_ref[...] = x + _dot(mix.astype(_BF16), w_out_ref[...])


def _mixer(layer, x, g, w_in, w_pool, pscale, w_dw, b_dw, ln_g, ln_b, w_pw, w_out):
    B, S, D = x.shape
    ts = SEQ_TILE
    shift_rows = ts + CONV_HALO - SUBLANES
    tile = pl.BlockSpec((None, ts, D), lambda b, s: (b, s, 0))
    params = (g, w_in, w_pool, pscale, w_dw, b_dw, ln_g, ln_b, w_pw, w_out)
    return pl.pallas_call(
        _mixer_kernel,
        grid=(B, S // ts),
        in_specs=[tile] + [_layer_spec(p, layer) for p in params],
        out_specs=tile,
        out_shape=jax.ShapeDtypeStruct(x.shape, x.dtype),
        scratch_shapes=[pltpu.VMEM((POOL_HALO + ts, POOL_WIDTH), _F32),
                        pltpu.VMEM((CONV_HALO + ts, CONV_WIDTH), _F32),
                        pltpu.VMEM((SUBLANES - 1, shift_rows, CONV_WIDTH), _F32),
                        pltpu.VMEM((ts, CONV_WIDTH), _F32),
                        pltpu.VMEM((ts, D), _F32),
                        pltpu.VMEM((ts, 2 * D), _F32)],
        compiler_params=pltpu.CompilerParams(
            dimension_semantics=("parallel", "arbitrary"), vmem_limit_bytes=VMEM_LIMIT),
        name="mixer",
    )(x, *params)


def _kv_kernel(m_ref, g_ref, w_ref, k_ref, v_ref):
    m = _rms(m_ref[...], g_ref[...]).astype(_BF16)
    k_ref[...] = (_dot(m, w_ref[:, 0:D_MODEL]) * (XHEAD_DIM ** -0.5)).astype(_BF16)
    v_ref[...] = _dot(m, w_ref[:, D_MODEL:2 * D_MODEL]).astype(_BF16)


def _kv_proj(layer, mem, g, w_kv):
    B, M, D = mem.shape
    blk = pl.BlockSpec((None, M, D), lambda b: (b, 0, 0))
    return pl.pallas_call(
        _kv_kernel,
        grid=(B,),
        in_specs=[blk, _layer_spec(g, layer), _layer_spec(w_kv, layer)],
        out_specs=[blk, blk],
        out_shape=[jax.ShapeDtypeStruct(mem.shape, _BF16)] * 2,
        compiler_params=pltpu.CompilerParams(
            dimension_semantics=("parallel",), vmem_limit_bytes=VMEM_LIMIT),
        name="kv_proj",
    )(mem, g, w_kv)


def _xattn_kernel(x_ref, g_ref, wq_ref, k_ref, v_ref, wo_ref, o_ref, att_ref):
    x = x_ref[...]
    h = _rms(x, g_ref[...]).astype(_BF16)
    q = _dot(h, wq_ref[...]).astype(_BF16)
    for hd in range(N_XHEADS):
        c0 = hd * XHEAD_DIM
        sc = lax.dot_general(q[:, c0:c0 + XHEAD_DIM], k_ref[:, c0:c0 + XHEAD_DIM],
                             (((1,), (1,)), ((), ())), preferred_element_type=_F32)
        p = jnp.exp(sc - jnp.max(sc, axis=-1, keepdims=True))
        denom = jnp.sum(p, axis=-1, keepdims=True)
        att = _dot(p.astype(_BF16), v_ref[:, c0:c0 + XHEAD_DIM]) / denom
        att_ref[:, c0:c0 + XHEAD_DIM] = att.astype(_BF16)
    o_ref[...] = x + _dot(att_ref[...], wo_ref[...])


def _xattn(layer, x, g, w_q, k, v, w_o):
    B, S, D = x.shape
    M = k.shape[1]
    ts = SEQ_TILE
    tile = pl.BlockSpec((None, ts, D), lambda b, s: (b, s, 0))
    kv_blk = pl.BlockSpec((None, M, D), lambda b, s: (b, 0, 0))
    return pl.pallas_call(
        _xattn_kernel,
        grid=(B, S // ts),
        in_specs=[tile, _layer_spec(g, layer), _layer_spec(w_q, layer), kv_blk, kv_blk,
                  _layer_spec(w_o, layer)],
        out_specs=tile,
        out_shape=jax.ShapeDtypeStruct(x.shape, x.dtype),
        scratch_shapes=[pltpu.VMEM((ts, D), _BF16)],
        compiler_params=pltpu.CompilerParams(
            dimension_semantics=("parallel", "parallel"), vmem_limit_bytes=VMEM_LIMIT),
        name="xattn",
    )(x, g, w_q, k, v, w_o)


def _ffn_kernel(x_ref, g_ref, w_gu_ref, w_down_ref, o_ref):
    x = x_ref[...]
    h = _rms(x, g_ref[...]).astype(_BF16)
    acc = x
    for c0, n in FF_CHUNKS:
        gate = _dot(h, w_gu_ref[:, c0:c0 + n])
        up = _dot(h, w_gu_ref[:, D_FF + c0:D_FF + c0 + n])
        act = (gate * jax.nn.sigmoid(gate) * up).astype(_BF16)
        acc = acc + _dot(act, w_down_ref[c0:c0 + n, :])
    o_ref[...] = acc


def _ffn(layer, ffn_layer, x2, g, w_gu, w_down):
    T, D = x2.shape
    tm = TOK_TILE
    tile = pl.BlockSpec((tm, D), lambda i: (i, 0))
    return pl.pallas_call(
        _ffn_kernel,
        grid=(T // tm,),
        in_specs=[tile, _layer_spec(g, layer), _layer_spec(w_gu, ffn_layer),
                  _layer_spec(w_down, ffn_layer)],
        out_specs=tile,
        out_shape=jax.ShapeDtypeStruct(x2.shape, x2.dtype),
        compiler_params=pltpu.CompilerParams(
            dimension_semantics=("parallel",), vmem_limit_bytes=VMEM_LIMIT),
        name="ffn",
    )(x2, g, w_gu, w_down)


def _pack_halves(y):
    half = y.shape[1] // 2
    lo = lax.bitcast_convert_type(y[:, :half].astype(_BF16).astype(_F32), _U32)
    hi = lax.bitcast_convert_type(y[:, half:].astype(_BF16).astype(_F32), _U32)
    return (lo >> 16) | (hi & HIGH_HALF_MASK)


def _unpack_halves(p):
    lo = lax.bitcast_convert_type(p << 16, _F32)
    hi = lax.bitcast_convert_type(p & HIGH_HALF_MASK, _F32)
    return lo, hi


def _route_kernel(x_ref, g_ref, w_r_ref, b_r_ref, earlier_ref, lane_before_ref,
                  meta_ref, cnt_ref, hs_ref, carry_ref, placed_smem, sorted_buf, sem,
                  *, region_rows, sort_rows):
    i = pl.program_id(0)
    tm = x_ref.shape[0]

    @pl.when(i == 0)
    def _():
        carry_ref[...] = jnp.zeros_like(carry_ref)
        for e in range(N_EXPERTS):
            placed_smem[e] = 0
        sorted_buf[sort_rows:, :] = jnp.zeros(
            (sorted_buf.shape[0] - sort_rows, sorted_buf.shape[1]), _U32)

    hf = _rms(x_ref[...], g_ref[...])

    lane = lax.broadcasted_iota(jnp.int32, (tm, LANES), 1)
    h_hi = hf.astype(_BF16)
    h_lo = (hf - h_hi.astype(_F32)).astype(_BF16)
    w_r = w_r_ref[...]
    w_hi = w_r.astype(_BF16)
    w_lo = (w_r - w_hi.astype(_F32)).astype(_BF16)
    logits = _dot(h_hi, w_hi) + _dot(h_lo, w_hi) + _dot(h_hi, w_lo) + b_r_ref[...]
    v1 = jnp.max(logits, axis=-1, keepdims=True)
    i1 = jnp.min(jnp.where(logits == v1, lane, LANES), axis=-1, keepdims=True)
    rest = jnp.where(lane == i1, NEG_BIG, logits)
    v2 = jnp.max(rest, axis=-1, keepdims=True)
    i2 = jnp.min(jnp.where(rest == v2, lane, LANES), axis=-1, keepdims=True)
    z = jnp.exp(v2 - v1)
    g1 = 1.0 / (1.0 + z)
    g2 = z / (1.0 + z)

    sel1 = lane == i1
    sel2 = lane == i2
    chosen = jnp.where(sel1 | sel2, 1.0, 0.0)
    before = _dot(earlier_ref[...], chosen.astype(_BF16))
    cnt = jnp.sum(chosen, axis=0, keepdims=True)
    cnt_pad = jnp.ceil(cnt * (1.0 / SUBLANES)) * SUBLANES
    group_off = _dot(jnp.broadcast_to(cnt_pad, (SUBLANES, LANES)).astype(_BF16),
                     lane_before_ref[...])[0:1, :]
    local = group_off + before
    loc1 = jnp.sum(jnp.where(sel1, local, 0.0), axis=-1, keepdims=True)
    loc2 = jnp.sum(jnp.where(sel2, local, 0.0), axis=-1, keepdims=True)
    placed = lane.astype(_F32) * region_rows + carry_ref[...] + before
    p1 = jnp.sum(jnp.where(sel1, placed, 0.0), axis=-1, keepdims=True)
    p2 = jnp.sum(jnp.where(sel2, placed, 0.0), axis=-1, keepdims=True)

    loc_t = jnp.where(lane == 0, loc1, jnp.where(lane == 1, loc2, -1.0)).T
    row = lax.broadcasted_iota(jnp.int32, (sort_rows, tm), 0).astype(_F32)
    perm = jnp.where((row == loc_t[0:1, :]) | (row == loc_t[1:2, :]), 1.0, 0.0).astype(_BF16)
    sorted_buf[0:sort_rows, :] = _pack_halves(_dot(perm, h_hi))

    def window(e, part):
        src = pl.multiple_of(off_i[0, e], SUBLANES) + part * GROUP_WINDOW
        dst = pl.multiple_of(e * region_rows + placed_smem[e], SUBLANES) + part * GROUP_WINDOW
        return pltpu.make_async_copy(sorted_buf.at[pl.ds(src, GROUP_WINDOW)],
                                     hs_ref.at[pl.ds(dst, GROUP_WINDOW)], sem)

    off_i = group_off.astype(jnp.int32)
    cnt_i = cnt_pad.astype(jnp.int32)
    needs_second = [cnt_i[0, e] > GROUP_WINDOW for e in range(N_EXPERTS)]
    for e in range(N_EXPERTS):
        window(e, 0).start()

        @pl.when(needs_second[e])
        def _():
            window(e, 1).start()
    for e in range(N_EXPERTS):
        window(e, 0).wait()

        @pl.when(needs_second[e])
        def _():
            window(e, 1).wait()
    for e in range(N_EXPERTS):
        placed_smem[e] = placed_smem[e] + cnt_i[0, e]
    carry_ref[...] += cnt_pad
    cnt_ref[...] = carry_ref[...]

    @pl.when(i == pl.num_programs(0) - 1)
    def _():
        def tail(e, part):
            dst = pl.multiple_of(e * region_rows + placed_smem[e], SUBLANES) + part * GROUP_WINDOW
            return pltpu.make_async_copy(sorted_buf.at[pl.ds(sort_rows, GROUP_WINDOW)],
                                         hs_ref.at[pl.ds(dst, GROUP_WINDOW)], sem)

        for e in range(N_EXPERTS):
            for part in range(EXPERT_ROW_TILE // GROUP_WINDOW):
                tail(e, part).start()
        for e in range(N_EXPERTS):
            for part in range(EXPERT_ROW_TILE // GROUP_WINDOW):
                tail(e, part).wait()

    meta = jnp.where(lane == META_G1, g1, 0.0)
    meta = jnp.where(lane == META_G2, g2, meta)
    meta = jnp.where(lane == META_P1, p1, meta)
    meta = jnp.where(lane == META_P2, p2, meta)
    meta_ref[...] = meta


def _region_rows(n_tokens):
    pad = (n_tokens // ROUTE_TILE) * (SUBLANES - 1) + 2 * EXPERT_ROW_TILE
    return -(-(n_tokens + pad) // EXPERT_ROW_TILE) * EXPERT_ROW_TILE


def _route(layer, moe_layer, x2, g, w_r, b_r):
    T, D = x2.shape
    tm = ROUTE_TILE
    region_rows = _region_rows(T)
    sort_rows = TOP_K * tm + N_EXPERTS * SUBLANES
    earlier = jnp.asarray(np.tril(np.ones((tm, tm), np.float32), -1), _BF16)
    lane_before = jnp.asarray(np.triu(np.ones((LANES, LANES), np.float32), 1), _BF16)
    const = lambda n: pl.BlockSpec((n, n), lambda i: (0, 0), pipeline_mode=pl.Buffered(1))
    return pl.pallas_call(
        functools.partial(_route_kernel, region_rows=region_rows, sort_rows=sort_rows),
        grid=(T // tm,),
        in_specs=[pl.BlockSpec((tm, D), lambda i: (i, 0)), _layer_spec(g, layer),
                  _layer_spec(w_r, moe_layer), _layer_spec(b_r, moe_layer),
                  const(tm), const(LANES)],
        out_specs=[pl.BlockSpec((tm, LANES), lambda i: (i, 0)),
                   pl.BlockSpec((1, LANES), lambda i: (0, 0)),
                   pl.BlockSpec(memory_space=pl.ANY)],
        out_shape=[jax.ShapeDtypeStruct((T, LANES), _F32),
                   jax.ShapeDtypeStruct((1, LANES), _F32),
                   jax.ShapeDtypeStruct((N_EXPERTS * region_rows, D // 2), _U32)],
        scratch_shapes=[pltpu.VMEM((1, LANES), _F32),
                        pltpu.SMEM((N_EXPERTS,), jnp.int32),
                        pltpu.VMEM((sort_rows + 2 * GROUP_WINDOW, D // 2), _U32),
                        pltpu.SemaphoreType.DMA],
        compiler_params=pltpu.CompilerParams(
            dimension_semantics=("arbitrary",), vmem_limit_bytes=VMEM_LIMIT),
        name="moe_route",
    )(x2, g, w_r, b_r, earlier, lane_before)


def _row_copy(src_ref, src_row, dst_ref, dst_row, sem):
    return pltpu.make_async_copy(src_ref.at[pl.ds(src_row, 1)], dst_ref.at[pl.ds(dst_row, 1)], sem)


def _expert_kernel(te_ref, tb_ref, tv_ref, hs_ref, w_gu_ref, w_down_ref, ys_ref):
    del te_ref, tb_ref
    i = pl.program_id(0)

    @pl.when(tv_ref[i] == 0)
    def _():
        ys_ref[...] = jnp.zeros_like(ys_ref)

    @pl.when(tv_ref[i] != 0)
    def _():
        lo, hi = _unpack_halves(hs_ref[...])
        h = jnp.concatenate([lo.astype(_BF16), hi.astype(_BF16)], axis=1)
        acc = None
        for c0, n in FF_CHUNKS:
            gate = _dot(h, w_gu_ref[:, c0:c0 + n])
            up = _dot(h, w_gu_ref[:, D_FF + c0:D_FF + c0 + n])
            act = (gate * jax.nn.sigmoid(gate) * up).astype(_BF16)
            part = _dot(act, w_down_ref[c0:c0 + n, :])
            acc = part if acc is None else acc + part
        ys_ref[...] = _pack_halves(acc)


def _experts(moe_layer, tile_expert, tile_block, tile_valid, hs, w_gu, w_down):
    n_rows, W = hs.shape
    rt = EXPERT_ROW_TILE
    D = w_gu.shape[2]
    grid_spec = pltpu.PrefetchScalarGridSpec(
        num_scalar_prefetch=3,
        grid=(tile_expert.shape[0],),
        in_specs=[pl.BlockSpec((rt, W), lambda i, te, tb, tv: (tb[i] * tv[i], 0)),
                  pl.BlockSpec((None, None, D, 2 * D_FF), lambda i, te, tb, tv: (moe_layer, te[i], 0, 0)),
                  pl.BlockSpec((None, None, D_FF, D), lambda i, te, tb, tv: (moe_layer, te[i], 0, 0))],
        out_specs=pl.BlockSpec((rt, W), lambda i, te, tb, tv: (tb[i], 0)),
    )
    return pl.pallas_call(
        _expert_kernel,
        grid_spec=grid_spec,
        out_shape=jax.ShapeDtypeStruct((n_rows + rt, W), hs.dtype),
        compiler_params=pltpu.CompilerParams(
            dimension_semantics=("arbitrary",), vmem_limit_bytes=VMEM_LIMIT),
        name="moe_experts",
    )(tile_expert, tile_block, tile_valid, hs, w_gu, w_down)


def _combine_kernel(pos1_ref, pos2_ref, x_ref, meta_ref, g_final_ref, ys_ref, o_ref,
                    buf1, buf2, sem, *, final_norm):
    tm = x_ref.shape[0]
    half = x_ref.shape[1] // 2

    def issue(t, carry):
        _row_copy(ys_ref, pos1_ref[0, t], buf1, t, sem).start()
        _row_copy(ys_ref, pos2_ref[0, t], buf2, t, sem).start()
        return carry

    lax.fori_loop(0, tm, issue, 0, unroll=DMA_UNROLL)
    for buf in (buf1, buf2):
        pltpu.make_async_copy(ys_ref.at[pl.ds(0, tm)], buf, sem).wait()

    g1 = meta_ref[:, META_G1:META_G1 + 1]
    g2 = meta_ref[:, META_G2:META_G2 + 1]
    lo1, hi1 = _unpack_halves(buf1[...])
    lo2, hi2 = _unpack_halves(buf2[...])
    out_lo = x_ref[:, :half] + g1 * lo1 + g2 * lo2
    out_hi = x_ref[:, half:] + g1 * hi1 + g2 * hi2
    if final_norm:
        ms = (jnp.sum(out_lo * out_lo, axis=-1, keepdims=True)
              + jnp.sum(out_hi * out_hi, axis=-1, keepdims=True)) * (1.0 / x_ref.shape[1])
        scale = lax.rsqrt(ms + EPS)
        out_lo = out_lo * scale * g_final_ref[:, :half]
        out_hi = out_hi * scale * g_final_ref[:, half:]
    o_ref[:, :half] = out_lo
    o_ref[:, half:] = out_hi


def _combine(pos1, pos2, x2, meta, ys, g_final, final_norm):
    T, D = x2.shape
    tm = DISPATCH_TILE
    smem_blk = pl.BlockSpec((None, 1, tm), lambda i: (i, 0, 0), memory_space=pltpu.SMEM)
    tile = pl.BlockSpec((tm, D), lambda i: (i, 0))
    return pl.pallas_call(
        functools.partial(_combine_kernel, final_norm=final_norm),
        grid=(T // tm,),
        in_specs=[smem_blk, smem_blk, tile, pl.BlockSpec((tm, LANES), lambda i: (i, 0)),
                  pl.BlockSpec((1, D), lambda i: (0, 0)), pl.BlockSpec(memory_space=pl.ANY)],
        out_specs=tile,
        out_shape=jax.ShapeDtypeStruct(x2.shape, x2.dtype),
        scratch_shapes=[pltpu.VMEM((tm, D // 2), _U32), pltpu.VMEM((tm, D // 2), _U32),
                        pltpu.SemaphoreType.DMA],
        compiler_params=pltpu.CompilerParams(
            dimension_semantics=("arbitrary",), vmem_limit_bytes=VMEM_LIMIT),
        name="moe_combine",
    )(pos1.reshape(T // tm, 1, tm), pos2.reshape(T // tm, 1, tm), x2, meta, g_final, ys)


def _moe(layer, moe_layer, x2, g, w_r, b_r, w_gu, w_down, g_final, final_norm):
    T, D = x2.shape
    E = w_gu.shape[1]
    rt = EXPERT_ROW_TILE
    meta, counts, hs = _route(layer, moe_layer, x2, g, w_r, b_r)

    region_tiles = hs.shape[0] // (E * rt)
    max_rows = TOP_K * T + (T // ROUTE_TILE) * E * (SUBLANES - 1)
    n_tiles = -(-max_rows // rt) + E
    counts = counts[0, :E].astype(jnp.int32)
    group_tiles = (counts + rt - 1) // rt
    tile_end = jnp.cumsum(group_tiles)
    tile_ids = jnp.arange(n_tiles, dtype=jnp.int32)
    tile_expert = jnp.minimum(
        jnp.sum((tile_ids[:, None] >= tile_end[None, :]).astype(jnp.int32), axis=1), E - 1)
    tile_valid = (tile_ids < tile_end[E - 1]).astype(jnp.int32)
    in_group = tile_ids - (tile_end - group_tiles)[tile_expert]
    tile_block = jnp.where(tile_valid != 0, tile_expert * region_tiles + in_group, E * region_tiles)

    ys = _experts(moe_layer, tile_expert, tile_block.astype(jnp.int32), tile_valid, hs, w_gu, w_down)
    pos1 = meta[:, META_P1].astype(jnp.int32)
    pos2 = meta[:, META_P2].astype(jnp.int32)
    return _combine(pos1, pos2, x2, meta, ys, g_final, final_norm)


def _final_norm_kernel(x_ref, g_ref, o_ref):
    o_ref[...] = _rms(x_ref[...], g_ref[...])


def _final_norm(x2, g):
    T, D = x2.shape
    tm = TOK_TILE
    tile = pl.BlockSpec((tm, D), lambda i: (i, 0))
    return pl.pallas_call(
        _final_norm_kernel,
        grid=(T // tm,),
        in_specs=[tile, pl.BlockSpec((1, D), lambda i: (0, 0))],
        out_specs=tile,
        out_shape=jax.ShapeDtypeStruct(x2.shape, x2.dtype),
        compiler_params=pltpu.CompilerParams(dimension_semantics=("parallel",)),
        name="final_norm",
    )(x2, g)


def kernel(x, mem, g_mix, w_in, w_pool, pool_scale, w_dw, b_dw, conv_ln_g, conv_ln_b, w_conv_out, w_mix_out, g_xattn, g_mem, w_xq, w_xkv, w_xo, g_ffn, w_ffn_gu, w_ffn_down, w_router, b_router, w_moe_gu, w_moe_down, g_final):
    B, S, D = x.shape
    depth = g_mix.shape[0]
    bf = lambda a: a.astype(_BF16)
    rows = lambda a: a.reshape(a.shape[0], 1, a.shape[1])

    g_mix, pool_scale, b_dw, conv_ln_g, conv_ln_b = map(rows, (g_mix, pool_scale, b_dw, conv_ln_g, conv_ln_b))
    g_xattn, g_mem, g_ffn = map(rows, (g_xattn, g_mem, g_ffn))
    w_dw = w_dw.reshape(depth, CONV_KERNEL, CONV_WIDTH)
    w_in, w_pool, w_conv_out, w_mix_out = map(bf, (w_in, w_pool, w_conv_out, w_mix_out))
    w_xq, w_xkv, w_xo = map(bf, (w_xq, w_xkv, w_xo))
    w_ffn_gu, w_ffn_down, w_moe_gu, w_moe_down = map(bf, (w_ffn_gu, w_ffn_down, w_moe_gu, w_moe_down))
    w_r_pad = jnp.pad(w_router, ((0, 0), (0, 0), (0, LANES - N_EXPERTS)))
    b_r_pad = rows(jnp.pad(b_router, ((0, 0), (0, LANES - N_EXPERTS)), constant_values=NEG_BIG))
    g_final = g_final.reshape(1, D)

    for i in range(depth):
        x = _mixer(i, x, g_mix, w_in, w_pool, pool_scale, w_dw, b_dw, conv_ln_g, conv_ln_b,
                   w_conv_out, w_mix_out)
        k, v = _kv_proj(i, mem, g_mem, w_xkv)
        x = _xattn(i, x, g_xattn, w_xq, k, v, w_xo)
        x2 = x.reshape(B * S, D)
        last = i == depth - 1
        if i % 2 == 0:
            x2 = _ffn(i, i // 2, x2, g_ffn, w_ffn_gu, w_ffn_down)
            if last:
                x2 = _final_norm(x2, g_final)
        else:
            x2 = _moe(i, i // 2, x2, g_ffn, w_r_pad, b_r_pad, w_moe_gu, w_moe_down, g_final, last)
        x = x2.reshape(B, S, D)
    return x
```

```python
import functools

import jax
import jax.numpy as jnp
import numpy as np
from jax import lax
from jax.experimental import pallas as pl
from jax.experimental.pallas import tpu as pltpu

D_MODEL = 1024
POOL_WIDTH = 512
POOL_GROUPS = 4
POOL_WINDOWS = (2, 4, 8, 16)
POOL_GROUP_IN = POOL_WIDTH // POOL_GROUPS
POOL_GROUP_OUT = D_MODEL // POOL_GROUPS
CONV_WIDTH = 512
CONV_KERNEL = 31
IN_COLS = POOL_WIDTH + 2 * CONV_WIDTH + 2 * D_MODEL
N_XHEADS = 4
XHEAD_DIM = D_MODEL // N_XHEADS
D_FF = 2816
N_EXPERTS = 8
TOP_K = 2
EPS = 1e-6

COL_POOL = 0
COL_GLU_A = POOL_WIDTH
COL_GLU_B = POOL_WIDTH + CONV_WIDTH
COL_GATE_POOL = POOL_WIDTH + 2 * CONV_WIDTH
COL_GATE_CONV = COL_GATE_POOL + D_MODEL

LANES = 128
SUBLANES = 8
POOL_HALO = 16
CONV_HALO = 32
SEQ_TILE = 512
TOK_TILE = 512
MXU_DIM = 256
FF_CHUNKS = ((0, 6 * MXU_DIM), (6 * MXU_DIM, 5 * MXU_DIM))
assert sum(n for _, n in FF_CHUNKS) == D_FF
CONV_ROW_CHUNK = 64
ROUTE_TILE = 512
EXPERT_ROW_TILE = 512
GROUP_WINDOW = 256
VMEM_LIMIT = 56 * 1024 * 1024
NEG_BIG = -1e30
META_G1, META_G2, META_L1, META_L2 = range(4)

_F32 = jnp.float32
_BF16 = jnp.bfloat16
_U32 = jnp.uint32
HIGH_HALF_MASK = np.uint32(0xFFFF0000)


def _layer_spec(arr, layer):
    tail = (0,) * (arr.ndim - 1)
    return pl.BlockSpec((None,) + arr.shape[1:], lambda *_: (layer,) + tail,
                        pipeline_mode=pl.Buffered(1))


def _rms(x, g):
    return x * lax.rsqrt(jnp.mean(x * x, axis=-1, keepdims=True) + EPS) * g


def _dot(a, b):
    return jnp.dot(a, b, preferred_element_type=_F32)


def _mixer_kernel(x_ref, g_ref, w_in_ref, w_pool_ref, pscale_ref, w_dw_ref, b_dw_ref,
                  ln_g_ref, ln_b_ref, w_pw_ref, w_out_ref, o_ref,
                  up_ext, cv_ext, cv_shift, cv_out, mix_ref, gate_ref):
    s = pl.program_id(1)
    ts = x_ref.shape[0]

    @pl.when(s == 0)
    def _():
        up_ext[0:POOL_HALO, :] = jnp.zeros((POOL_HALO, POOL_WIDTH), _F32)
        cv_ext[0:CONV_HALO, :] = jnp.zeros((CONV_HALO, CONV_WIDTH), _F32)

    @pl.when(s > 0)
    def _():
        up_ext[0:POOL_HALO, :] = up_ext[ts:ts + POOL_HALO, :]
        cv_ext[0:CONV_HALO, :] = cv_ext[ts:ts + CONV_HALO, :]

    x = x_ref[...]
    h = _rms(x, g_ref[...]).astype(_BF16)

    glu_a = _dot(h, w_in_ref[:, COL_GLU_A:COL_GLU_A + CONV_WIDTH])
    glu_b = _dot(h, w_in_ref[:, COL_GLU_B:COL_GLU_B + CONV_WIDTH])
    cv_ext[CONV_HALO:CONV_HALO + ts, :] = glu_a * jax.nn.sigmoid(glu_b)

    shift_rows = cv_shift.shape[1]
    for r in range(1, SUBLANES):
        cv_shift[r - 1] = cv_ext[r:r + shift_rows, :]

    tap0 = CONV_HALO - (CONV_KERNEL - 1)
    n_blocks = CONV_WIDTH // LANES
    gate_cols = 2 * D_MODEL // n_blocks
    for blk in range(n_blocks):
        c0 = blk * LANES
        d0 = COL_GATE_POOL + blk * gate_cols
        gate_ref[:, blk * gate_cols:(blk + 1) * gate_cols] = jax.nn.sigmoid(
            _dot(h, w_in_ref[:, d0:d0 + gate_cols]))
        for r0 in range(0, ts, CONV_ROW_CHUNK):
            acc = jnp.broadcast_to(b_dw_ref[:, c0:c0 + LANES], (CONV_ROW_CHUNK, LANES))
            for k in range(CONV_KERNEL):
                off = tap0 + k
                mis = off % SUBLANES
                row = r0 + off - mis
                if mis == 0:
                    win = cv_ext[row:row + CONV_ROW_CHUNK, c0:c0 + LANES]
                else:
                    win = cv_shift[mis - 1, row:row + CONV_ROW_CHUNK, c0:c0 + LANES]
                acc = acc + w_dw_ref[k:k + 1, c0:c0 + LANES] * win
            cv_out[r0:r0 + CONV_ROW_CHUNK, c0:c0 + LANES] = acc

    up_ext[POOL_HALO:POOL_HALO + ts, :] = _dot(h, w_in_ref[:, COL_POOL:COL_POOL + POOL_WIDTH])
    pos = (s * ts + 1 + lax.broadcasted_iota(jnp.int32, (ts, 1), 0)).astype(_F32)
    for g, w in enumerate(POOL_WINDOWS):
        c0 = g * POOL_GROUP_IN
        win = up_ext[:, c0:c0 + POOL_GROUP_IN]
        span = 1
        while span < w:
            win = win + pltpu.roll(win, span, 0)
            span *= 2
        cur = up_ext[POOL_HALO:POOL_HALO + ts, c0:c0 + POOL_GROUP_IN]
        pooled = win[POOL_HALO:, :] / jnp.minimum(pos, float(w)) - cur
        d0 = g * POOL_GROUP_OUT
        mix_ref[:, d0:d0 + POOL_GROUP_OUT] = (
            _dot(pooled.astype(_BF16), w_pool_ref[g]) * pscale_ref[:, d0:d0 + POOL_GROUP_OUT])

    u = cv_out[...]
    mu = jnp.mean(u, axis=-1, keepdims=True)
    uc = u - mu
    var = jnp.mean(uc * uc, axis=-1, keepdims=True)
    un = uc * lax.rsqrt(var + EPS) * ln_g_ref[...] + ln_b_ref[...]
    un = un * jax.nn.sigmoid(un)
    y_conv = _dot(un.astype(_BF16), w_pw_ref[...])
    mix = gate_ref[:, 0:D_MODEL] * mix_ref[...] + gate_ref[:, D_MODEL:] * y_conv

    o_ref[...] = x + _dot(mix.astype(_BF16), w_out_ref[...])


def _mixer(layer, x, g, w_in, w_pool, pscale, w_dw, b_dw, ln_g, ln_b, w_pw, w_out):
    B, S, D = x.shape
    ts = SEQ_TILE
    shift_rows = ts + CONV_HALO - SUBLANES
    tile = pl.BlockSpec((None, ts, D), lambda b, s: (b, s, 0))
    params = (g, w_in, w_pool, pscale, w_dw, b_dw, ln_g, ln_b, w_pw, w_out)
    return pl.pallas_call(
        _mixer_kernel,
        grid=(B, S // ts),
        in_specs=[tile] + [_layer_spec(p, layer) for p in params],
        out_specs=tile,
        out_shape=jax.ShapeDtypeStruct(x.shape, x.dtype),
        scratch_shapes=[pltpu.VMEM((POOL_HALO + ts, POOL_WIDTH), _F32),
                        pltpu.VMEM((CONV_HALO + ts, CONV_WIDTH), _F32),
                        pltpu.VMEM((SUBLANES - 1, shift_rows, CONV_WIDTH), _F32),
                        pltpu.VMEM((ts, CONV_WIDTH), _F32),
                        pltpu.VMEM((ts, D), _F32),
                        pltpu.VMEM((ts, 2 * D), _F32)],
        compiler_params=pltpu.CompilerParams(
            dimension_semantics=("parallel", "arbitrary"), vmem_limit_bytes=VMEM_LIMIT),
        name="mixer",
    )(x, *params)


def _kv_kernel(m_ref, g_ref, w_ref, k_ref, v_ref):
    m = _rms(m_ref[...], g_ref[...]).astype(_BF16)
    k_ref[...] = (_dot(m, w_ref[:, 0:D_MODEL]) * (XHEAD_DIM ** -0.5)).astype(_BF16)
    v_ref[...] = _dot(m, w_ref[:, D_MODEL:2 * D_MODEL]).astype(_BF16)


def _kv_proj(layer, mem, g, w_kv):
    B, M, D = mem.shape
    blk = pl.BlockSpec((None, M, D), lambda b: (b, 0, 0))
    return pl.pallas_call(
        _kv_kernel,
        grid=(B,),
        in_specs=[blk, _layer_spec(g, layer), _layer_spec(w_kv, layer)],
        out_specs=[blk, blk],
        out_shape=[jax.ShapeDtypeStruct(mem.shape, _BF16)] * 2,
        compiler_params=pltpu.CompilerParams(
            dimension_semantics=("parallel",), vmem_limit_bytes=VMEM_LIMIT),
        name="kv_proj",
    )(mem, g, w_kv)


def _xattn_kernel(x_ref, g_ref, wq_ref, k_ref, v_ref, wo_ref, o_ref, att_ref):
    x = x_ref[...]
    h = _rms(x, g_ref[...]).astype(_BF16)
    q = _dot(h, wq_ref[...]).astype(_BF16)
    for hd in range(N_XHEADS):
        c0 = hd * XHEAD_DIM
        sc = lax.dot_general(q[:, c0:c0 + XHEAD_DIM], k_ref[:, c0:c0 + XHEAD_DIM],
                             (((1,), (1,)), ((), ())), preferred_element_type=_F32)
        p = jnp.exp(sc - jnp.max(sc, axis=-1, keepdims=True))
        denom = jnp.sum(p, axis=-1, keepdims=True)
        att = _dot(p.astype(_BF16), v_ref[:, c0:c0 + XHEAD_DIM]) / denom
        att_ref[:, c0:c0 + XHEAD_DIM] = att.astype(_BF16)
    o_ref[...] = x + _dot(att_ref[...], wo_ref[...])


def _xattn(layer, x, g, w_q, k, v, w_o):
    B, S, D = x.shape
    M = k.shape[1]
    ts = SEQ_TILE
    tile = pl.BlockSpec((None, ts, D), lambda b, s: (b, s, 0))
    kv_blk = pl.BlockSpec((None, M, D), lambda b, s: (b, 0, 0))
    return pl.pallas_call(
        _xattn_kernel,
        grid=(B, S // ts),
        in_specs=[tile, _layer_spec(g, layer), _layer_spec(w_q, layer), kv_blk, kv_blk,
                  _layer_spec(w_o, layer)],
        out_specs=tile,
        out_shape=jax.ShapeDtypeStruct(x.shape, x.dtype),
        scratch_shapes=[pltpu.VMEM((ts, D), _BF16)],
        compiler_params=pltpu.CompilerParams(
            dimension_semantics=("parallel", "parallel"), vmem_limit_bytes=VMEM_LIMIT),
        name="xattn",
    )(x, g, w_q, k, v, w_o)


def _ffn_kernel(x_ref, g_ref, w_gu_ref, w_down_ref, o_ref):
    x = x_ref[...]
    h = _rms(x, g_ref[...]).astype(_BF16)
    acc = x
    for c0, n in FF_CHUNKS:
        gate = _dot(h, w_gu_ref[:, c0:c0 + n])
        up = _dot(h, w_gu_ref[:, D_FF + c0:D_FF + c0 + n])
        act = (gate * jax.nn.sigmoid(gate) * up).astype(_BF16)
        acc = acc + _dot(act, w_down_ref[c0:c0 + n, :])
    o_ref[...] = acc


def _ffn(layer, ffn_layer, x2, g, w_gu, w_down):
    T, D = x2.shape
    tm = TOK_TILE
    tile = pl.BlockSpec((tm, D), lambda i: (i, 0))
    return pl.pallas_call(
        _ffn_kernel,
        grid=(T // tm,),
        in_specs=[tile, _layer_spec(g, layer), _layer_spec(w_gu, ffn_layer),
                  _layer_spec(w_down, ffn_layer)],
        out_specs=tile,
        out_shape=jax.ShapeDtypeStruct(x2.shape, x2.dtype),
        compiler_params=pltpu.CompilerParams(
            dimension_semantics=("parallel",), vmem_limit_bytes=VMEM_LIMIT),
        name="ffn",
    )(x2, g, w_gu, w_down)


def _pack_halves(y):
    half = y.shape[1] // 2
    lo = lax.bitcast_convert_type(y[:, :half].astype(_BF16).astype(_F32), _U32)
    hi = lax.bitcast_convert_type(y[:, half:].astype(_BF16).astype(_F32), _U32)
    return (lo >> 16) | (hi & HIGH_HALF_MASK)


def _unpack_halves(p):
    lo = lax.bitcast_convert_type(p << 16, _F32)
    hi = lax.bitcast_convert_type(p & HIGH_HALF_MASK, _F32)
    return lo, hi


def _route_kernel(x_ref, g_ref, w_r_ref, b_r_ref, earlier_ref, lane_before_ref,
                  meta_ref, cnt_ref, tab_ref, hs_ref,
                  carry_ref, placed_smem, second_smem, sorted_buf, sem,
                  *, region_rows, sort_rows):
    i = pl.program_id(0)
    tm = x_ref.shape[0]

    @pl.when(i == 0)
    def _():
        carry_ref[...] = jnp.zeros_like(carry_ref)
        for e in range(N_EXPERTS):
            placed_smem[e] = 0
        sorted_buf[:, sort_rows:, :] = jnp.zeros(
            (sorted_buf.shape[0], sorted_buf.shape[1] - sort_rows, sorted_buf.shape[2]), _U32)

    hf = _rms(x_ref[...], g_ref[...])

    lane = lax.broadcasted_iota(jnp.int32, (tm, LANES), 1)
    h_hi = hf.astype(_BF16)
    h_lo = (hf - h_hi.astype(_F32)).astype(_BF16)
    w_r = w_r_ref[...]
    w_hi = w_r.astype(_BF16)
    w_lo = (w_r - w_hi.astype(_F32)).astype(_BF16)
    logits = _dot(h_hi, w_hi) + _dot(h_lo, w_hi) + _dot(h_hi, w_lo) + b_r_ref[...]
    v1 = jnp.max(logits, axis=-1, keepdims=True)
    i1 = jnp.min(jnp.where(logits == v1, lane, LANES), axis=-1, keepdims=True)
    rest = jnp.where(lane == i1, NEG_BIG, logits)
    v2 = jnp.max(rest, axis=-1, keepdims=True)
    i2 = jnp.min(jnp.where(rest == v2, lane, LANES), axis=-1, keepdims=True)
    z = jnp.exp(v2 - v1)
    g1 = 1.0 / (1.0 + z)
    g2 = z / (1.0 + z)

    sel1 = lane == i1
    sel2 = lane == i2
    chosen = jnp.where(sel1 | sel2, 1.0, 0.0)
    before = _dot(earlier_ref[...], chosen.astype(_BF16))
    cnt = jnp.sum(chosen, axis=0, keepdims=True)
    cnt_pad = jnp.ceil(cnt * (1.0 / SUBLANES)) * SUBLANES
    group_off = _dot(jnp.broadcast_to(cnt_pad, (SUBLANES, LANES)).astype(_BF16),
                     lane_before_ref[...])[0:1, :]
    local = group_off + before
    loc1 = jnp.sum(jnp.where(sel1, local, 0.0), axis=-1, keepdims=True)
    loc2 = jnp.sum(jnp.where(sel2, local, 0.0), axis=-1, keepdims=True)

    slot = lax.rem(i, 2)
    loc_t = jnp.where(lane == 0, loc1, jnp.where(lane == 1, loc2, -1.0)).T
    row = lax.broadcasted_iota(jnp.int32, (sort_rows, tm), 0).astype(_F32)
    perm = jnp.where((row == loc_t[0:1, :]) | (row == loc_t[1:2, :]), 1.0, 0.0).astype(_BF16)
    sorted_buf[slot, 0:sort_rows, :] = _pack_halves(_dot(perm, h_hi))

    def window(src, dst):
        return pltpu.make_async_copy(sorted_buf.at[slot, pl.ds(src, GROUP_WINDOW)],
                                     hs_ref.at[pl.ds(dst, GROUP_WINDOW)], sem)

    def wait_windows(second_flags):
        for e in range(N_EXPERTS):
            window(0, 0).wait()

            @pl.when(second_flags(e) != 0)
            def _():
                window(0, 0).wait()

    @pl.when(i > 0)
    def _():
        wait_windows(lambda e: second_smem[e])

    off_i = group_off.astype(jnp.int32)
    cnt_i = cnt_pad.astype(jnp.int32)
    for e in range(N_EXPERTS):
        src = pl.multiple_of(off_i[0, e], SUBLANES)
        start = pl.multiple_of(e * region_rows + placed_smem[e], SUBLANES)
        rows = cnt_i[0, e]
        second = (rows > GROUP_WINDOW).astype(jnp.int32)
        window(src, start).start()

        @pl.when(second != 0)
        def _():
            window(src + GROUP_WINDOW, start + GROUP_WINDOW).start()

        second_smem[e] = second
        placed_smem[e] = placed_smem[e] + rows
        tab_ref[0, e] = start
        tab_ref[0, N_EXPERTS + e] = rows
    carry_ref[...] += cnt_pad
    cnt_ref[...] = carry_ref[...]

    @pl.when(i == pl.num_programs(0) - 1)
    def _():
        wait_windows(lambda e: second_smem[e])

        def tail(e, part):
            dst = pl.multiple_of(e * region_rows + placed_smem[e], SUBLANES) + part * GROUP_WINDOW
            return window(sort_rows, dst)

        for e in range(N_EXPERTS):
            for part in range(EXPERT_ROW_TILE // GROUP_WINDOW):
                tail(e, part).start()
        for e in range(N_EXPERTS):
            for part in range(EXPERT_ROW_TILE // GROUP_WINDOW):
                tail(e, part).wait()

    meta = jnp.where(lane == META_G1, g1, 0.0)
    meta = jnp.where(lane == META_G2, g2, meta)
    meta = jnp.where(lane == META_L1, loc1, meta)
    meta = jnp.where(lane == META_L2, loc2, meta)
    meta_ref[...] = meta


def _region_rows(n_tokens):
    pad = (n_tokens // ROUTE_TILE) * (SUBLANES - 1) + 2 * EXPERT_ROW_TILE
    return -(-(n_tokens + pad) // EXPERT_ROW_TILE) * EXPERT_ROW_TILE


def _route(layer, moe_layer, x2, g, w_r, b_r):
    T, D = x2.shape
    tm = ROUTE_TILE
    region_rows = _region_rows(T)
    sort_rows = TOP_K * tm + N_EXPERTS * SUBLANES
    earlier = jnp.asarray(np.tril(np.ones((tm, tm), np.float32), -1), _BF16)
    lane_before = jnp.asarray(np.triu(np.ones((LANES, LANES), np.float32), 1), _BF16)
    const = lambda n: pl.BlockSpec((n, n), lambda i: (0, 0), pipeline_mode=pl.Buffered(1))
    return pl.pallas_call(
        functools.partial(_route_kernel, region_rows=region_rows, sort_rows=sort_rows),
        grid=(T // tm,),
        in_specs=[pl.BlockSpec((tm, D), lambda i: (i, 0)), _layer_spec(g, layer),
                  _layer_spec(w_r, moe_layer), _layer_spec(b_r, moe_layer),
                  const(tm), const(LANES)],
        out_specs=[pl.BlockSpec((tm, LANES), lambda i: (i, 0)),
                   pl.BlockSpec((1, LANES), lambda i: (0, 0)),
                   pl.BlockSpec((None, 1, 2 * N_EXPERTS), lambda i: (i, 0, 0),
                                memory_space=pltpu.SMEM),
                   pl.BlockSpec(memory_space=pl.ANY)],
        out_shape=[jax.ShapeDtypeStruct((T, LANES), _F32),
                   jax.ShapeDtypeStruct((1, LANES), _F32),
                   jax.ShapeDtypeStruct((T // tm, 1, 2 * N_EXPERTS), jnp.int32),
                   jax.ShapeDtypeStruct((N_EXPERTS * region_rows, D // 2), _U32)],
        scratch_shapes=[pltpu.VMEM((1, LANES), _F32),
                        pltpu.SMEM((N_EXPERTS,), jnp.int32),
                        pltpu.SMEM((N_EXPERTS,), jnp.int32),
                        pltpu.VMEM((2, sort_rows + GROUP_WINDOW, D // 2), _U32),
                        pltpu.SemaphoreType.DMA],
        compiler_params=pltpu.CompilerParams(
            dimension_semantics=("arbitrary",), vmem_limit_bytes=VMEM_LIMIT),
        name="moe_route",
    )(x2, g, w_r, b_r, earlier, lane_before)


def _expert_kernel(te_ref, tb_ref, tv_ref, hs_ref, w_gu_ref, w_down_ref, ys_ref):
    del te_ref, tb_ref
    i = pl.program_id(0)

    @pl.when(tv_ref[i] == 0)
    def _():
        ys_ref[...] = jnp.zeros_like(ys_ref)

    @pl.when(tv_ref[i] != 0)
    def _():
        lo, hi = _unpack_halves(hs_ref[...])
        h = jnp.concatenate([lo.astype(_BF16), hi.astype(_BF16)], axis=1)
        acc = None
        for c0, n in FF_CHUNKS:
            gate = _dot(h, w_gu_ref[:, c0:c0 + n])
            up = _dot(h, w_gu_ref[:, D_FF + c0:D_FF + c0 + n])
            act = (gate * jax.nn.sigmoid(gate) * up).astype(_BF16)
            part = _dot(act, w_down_ref[c0:c0 + n, :])
            acc = part if acc is None else acc + part
        ys_ref[...] = _pack_halves(acc)


def _experts(moe_layer, tile_expert, tile_block, tile_valid, hs, w_gu, w_down):
    n_rows, W = hs.shape
    rt = EXPERT_ROW_TILE
    D = w_gu.shape[2]
    grid_spec = pltpu.PrefetchScalarGridSpec(
        num_scalar_prefetch=3,
        grid=(tile_expert.shape[0],),
        in_specs=[pl.BlockSpec((rt, W), lambda i, te, tb, tv: (tb[i] * tv[i], 0)),
                  pl.BlockSpec((None, None, D, 2 * D_FF), lambda i, te, tb, tv: (moe_layer, te[i], 0, 0)),
                  pl.BlockSpec((None, None, D_FF, D), lambda i, te, tb, tv: (moe_layer, te[i], 0, 0))],
        out_specs=pl.BlockSpec((rt, W), lambda i, te, tb, tv: (tb[i], 0)),
    )
    return pl.pallas_call(
        _expert_kernel,
        grid_spec=grid_spec,
        out_shape=jax.ShapeDtypeStruct((n_rows + rt, W), hs.dtype),
        compiler_params=pltpu.CompilerParams(
            dimension_semantics=("arbitrary",), vmem_limit_bytes=VMEM_LIMIT),
        name="moe_experts",
    )(tile_expert, tile_block, tile_valid, hs, w_gu, w_down)


def _combine_kernel(tab_ref, x_ref, meta_ref, g_final_ref, ys_ref, o_ref, ybuf, sem, *, final_norm):
    i = pl.program_id(0)
    n = pl.num_programs(0)
    tm, D = x_ref.shape
    slot = lax.rem(i, 2)

    @pl.when(i == 0)
    def _():
        ybuf[...] = jnp.zeros_like(ybuf)

    def fetch(tile, to_slot):
        off = 0
        for e in range(N_EXPERTS):
            start = tab_ref[tile, e]
            rows = tab_ref[tile, N_EXPERTS + e]
            size = ROUTE_TILE
            while size >= SUBLANES:
                above = rows & ~(2 * size - 1)
                src = pl.multiple_of(start + above, SUBLANES)
                dst = pl.multiple_of(off + above, SUBLANES)

                @pl.when((rows & size) != 0)
                def _():
                    pltpu.make_async_copy(ys_ref.at[pl.ds(src, size)],
                                          ybuf.at[to_slot, pl.ds(dst, size)], sem.at[to_slot]).start()
                size //= 2
            off = off + rows
        return off

    def total_rows(tile):
        total = 0
        for e in range(N_EXPERTS):
            total = total + tab_ref[tile, N_EXPERTS + e]
        return total

    @pl.when(i == 0)
    def _():
        fetch(0, 0)

    @pl.when(i + 1 < n)
    def _():
        fetch(i + 1, 1 - slot)

    landed = pl.multiple_of(total_rows(i), SUBLANES)
    pltpu.make_async_copy(ys_ref.at[pl.ds(0, landed)], ybuf.at[slot, pl.ds(0, landed)],
                          sem.at[slot]).wait()

    lo, hi = _unpack_halves(ybuf[slot])
    y = jnp.concatenate([lo.astype(_BF16), hi.astype(_BF16)], axis=1)
    col = lax.broadcasted_iota(jnp.int32, (tm, ybuf.shape[1]), 1).astype(_F32)
    pick1 = jnp.where(col == meta_ref[:, META_L1:META_L1 + 1], 1.0, 0.0).astype(_BF16)
    pick2 = jnp.where(col == meta_ref[:, META_L2:META_L2 + 1], 1.0, 0.0).astype(_BF16)
    out = (x_ref[...] + meta_ref[:, META_G1:META_G1 + 1] * _dot(pick1, y)
           + meta_ref[:, META_G2:META_G2 + 1] * _dot(pick2, y))
    if final_norm:
        out = _rms(out, g_final_ref[...])
    o_ref[...] = out


def _combine(tab, x2, meta, ys, g_final, final_norm):
    T, D = x2.shape
    tm = ROUTE_TILE
    sort_rows = -(-(TOP_K * tm + N_EXPERTS * SUBLANES) // MXU_DIM) * MXU_DIM
    tile = pl.BlockSpec((tm, D), lambda i, tab: (i, 0))
    grid_spec = pltpu.PrefetchScalarGridSpec(
        num_scalar_prefetch=1,
        grid=(T // tm,),
        in_specs=[tile, pl.BlockSpec((tm, LANES), lambda i, tab: (i, 0)),
                  pl.BlockSpec((1, D), lambda i, tab: (0, 0)), pl.BlockSpec(memory_space=pl.ANY)],
        out_specs=tile,
        scratch_shapes=[pltpu.VMEM((2, sort_rows, D // 2), _U32), pltpu.SemaphoreType.DMA((2,))],
    )
    return pl.pallas_call(
        functools.partial(_combine_kernel, final_norm=final_norm),
        grid_spec=grid_spec,
        out_shape=jax.ShapeDtypeStruct(x2.shape, x2.dtype),
        compiler_params=pltpu.CompilerParams(
            dimension_semantics=("arbitrary",), vmem_limit_bytes=VMEM_LIMIT),
        name="moe_combine",
    )(tab, x2, meta, g_final, ys)


def _moe(layer, moe_layer, x2, g, w_r, b_r, w_gu, w_down, g_final, final_norm):
    T, D = x2.shape
    E = w_gu.shape[1]
    rt = EXPERT_ROW_TILE
    meta, counts, tab, hs = _route(layer, moe_layer, x2, g, w_r, b_r)

    region_tiles = hs.shape[0] // (E * rt)
    max_rows = TOP_K * T + (T // ROUTE_TILE) * E * (SUBLANES - 1)
    n_tiles = -(-max_rows // rt) + E
    counts = counts[0, :E].astype(jnp.int32)
    group_tiles = (counts + rt - 1) // rt
    tile_end = jnp.cumsum(group_tiles)
    tile_ids = jnp.arange(n_tiles, dtype=jnp.int32)
    tile_expert = jnp.minimum(
        jnp.sum((tile_ids[:, None] >= tile_end[None, :]).astype(jnp.int32), axis=1), E - 1)
    tile_valid = (tile_ids < tile_end[E - 1]).astype(jnp.int32)
    in_group = tile_ids - (tile_end - group_tiles)[tile_expert]
    tile_block = jnp.where(tile_valid != 0, tile_expert * region_tiles + in_group, E * region_tiles)

    ys = _experts(moe_layer, tile_expert, tile_block.astype(jnp.int32), tile_valid, hs, w_gu, w_down)
    return _combine(tab.reshape(tab.shape[0], tab.shape[2]), x2, meta, ys, g_final, final_norm)


def _final_norm_kernel(x_ref, g_ref, o_ref):
    o_ref[...] = _rms(x_ref[...], g_ref[...])


def _final_norm(x2, g):
    T, D = x2.shape
    tm = TOK_TILE
    tile = pl.BlockSpec((tm, D), lambda i: (i, 0))
    return pl.pallas_call(
        _final_norm_kernel,
        grid=(T // tm,),
        in_specs=[tile, pl.BlockSpec((1, D), lambda i: (0, 0))],
        out_specs=tile,
        out_shape=jax.ShapeDtypeStruct(x2.shape, x2.dtype),
        compiler_params=pltpu.CompilerParams(dimension_semantics=("parallel",)),
        name="final_norm",
    )(x2, g)


def kernel(x, mem, g_mix, w_in, w_pool, pool_scale, w_dw, b_dw, conv_ln_g, conv_ln_b, w_conv_out, w_mix_out, g_xattn, g_mem, w_xq, w_xkv, w_xo, g_ffn, w_ffn_gu, w_ffn_down, w_router, b_router, w_moe_gu, w_moe_down, g_final):
    B, S, D = x.shape
    depth = g_mix.shape[0]
    bf = lambda a: a.astype(_BF16)
    rows = lambda a: a.reshape(a.shape[0], 1, a.shape[1])

    g_mix, pool_scale, b_dw, conv_ln_g, conv_ln_b = map(rows, (g_mix, pool_scale, b_dw, conv_ln_g, conv_ln_b))
    g_xattn, g_mem, g_ffn = map(rows, (g_xattn, g_mem, g_ffn))
    w_dw = w_dw.reshape(depth, CONV_KERNEL, CONV_WIDTH)
    w_in, w_pool, w_conv_out, w_mix_out = map(bf, (w_in, w_pool, w_conv_out, w_mix_out))
    w_xq, w_xkv, w_xo = map(bf, (w_xq, w_xkv, w_xo))
    w_ffn_gu, w_ffn_down, w_moe_gu, w_moe_down = map(bf, (w_ffn_gu, w_ffn_down, w_moe_gu, w_moe_down))
    w_r_pad = jnp.pad(w_router, ((0, 0), (0, 0), (0, LANES - N_EXPERTS)))
    b_r_pad = rows(jnp.pad(b_router, ((0, 0), (0, LANES - N_EXPERTS)), constant_values=NEG_BIG))
    g_final = g_final.reshape(1, D)

    for i in range(depth):
        x = _mixer(i, x, g_mix, w_in, w_pool, pool_scale, w_dw, b_dw, conv_ln_g, conv_ln_b,
                   w_conv_out, w_mix_out)
        k, v = _kv_proj(i, mem, g_mem, w_xkv)
        x = _xattn(i, x, g_xattn, w_xq, k, v, w_xo)
        x2 = x.reshape(B * S, D)
        last = i == depth - 1
        if i % 2 == 0:
            x2 = _ffn(i, i // 2, x2, g_ffn, w_ffn_gu, w_ffn_down)
            if last:
                x2 = _final_norm(x2, g_final)
        else:
            x2 = _moe(i, i // 2, x2, g_ffn, w_r_pad, b_r_pad, w_moe_gu, w_moe_down, g_final, last)
        x = x2.reshape(B, S, D)
    return x
```

```python
import functools
from typing import NamedTuple, Optional

import jax
import jax.numpy as jnp
import numpy as np
from jax import lax
from jax.experimental import pallas as pl
from jax.experimental.pallas import tpu as pltpu

D_MODEL = 1024
POOL_WIDTH = 512
POOL_GROUPS = 4
POOL_WINDOWS = (2, 4, 8, 16)
POOL_GROUP_IN = POOL_WIDTH // POOL_GROUPS
POOL_GROUP_OUT = D_MODEL // POOL_GROUPS
CONV_WIDTH = 512
CONV_KERNEL = 31
IN_COLS = POOL_WIDTH + 2 * CONV_WIDTH + 2 * D_MODEL
N_XHEADS = 4
XHEAD_DIM = D_MODEL // N_XHEADS
D_FF = 2816
N_EXPERTS = 8
TOP_K = 2
EPS = 1e-6

COL_POOL = 0
COL_GLU_A = POOL_WIDTH
COL_GLU_B = POOL_WIDTH + CONV_WIDTH
COL_GATE_POOL = POOL_WIDTH + 2 * CONV_WIDTH
COL_GATE_CONV = COL_GATE_POOL + D_MODEL

LANES = 128
SUBLANES = 8
POOL_HALO = 16
CONV_HALO = 32
SEQ_TILE = 512
TOK_TILE = 512
MXU_DIM = 256
FF_CHUNKS = ((0, 6 * MXU_DIM), (6 * MXU_DIM, 5 * MXU_DIM))
assert sum(n for _, n in FF_CHUNKS) == D_FF
CONV_ROW_CHUNK = 64
ROUTE_TILE = 512
EXPERT_ROW_TILE = 512
GROUP_WINDOW = 256
CAST_PARTS = 8
VMEM_LIMIT = 56 * 1024 * 1024
NEG_BIG = -1e30
META_G1, META_G2, META_L1, META_L2 = range(4)

_F32 = jnp.float32
_BF16 = jnp.bfloat16
_U32 = jnp.uint32
HIGH_HALF_MASK = np.uint32(0xFFFF0000)


def _layer_spec(arr, layer):
    tail = (0,) * (arr.ndim - 1)
    return pl.BlockSpec((None,) + arr.shape[1:], lambda *_: (layer,) + tail,
                        pipeline_mode=pl.Buffered(1))


def _rms(x, g):
    return x * lax.rsqrt(jnp.mean(x * x, axis=-1, keepdims=True) + EPS) * g


def _dot(a, b):
    return jnp.dot(a, b, preferred_element_type=_F32)


def _mixer_kernel(x_ref, g_ref, w_in_ref, w_pool_ref, pscale_ref, w_dw_ref, b_dw_ref,
                  ln_g_ref, ln_b_ref, w_pw_ref, w_out_ref, o_ref,
                  up_ext, cv_ext, cv_shift, cv_out, mix_ref, gate_ref):
    s = pl.program_id(1)
    ts = x_ref.shape[0]

    @pl.when(s == 0)
    def _():
        up_ext[0:POOL_HALO, :] = jnp.zeros((POOL_HALO, POOL_WIDTH), _F32)
        cv_ext[0:CONV_HALO, :] = jnp.zeros((CONV_HALO, CONV_WIDTH), _F32)

    @pl.when(s > 0)
    def _():
        up_ext[0:POOL_HALO, :] = up_ext[ts:ts + POOL_HALO, :]
        cv_ext[0:CONV_HALO, :] = cv_ext[ts:ts + CONV_HALO, :]

    x = x_ref[...]
    h = _rms(x, g_ref[...]).astype(_BF16)

    glu_a = _dot(h, w_in_ref[:, COL_GLU_A:COL_GLU_A + CONV_WIDTH])
    glu_b = _dot(h, w_in_ref[:, COL_GLU_B:COL_GLU_B + CONV_WIDTH])
    cv_ext[CONV_HALO:CONV_HALO + ts, :] = glu_a * jax.nn.sigmoid(glu_b)

    shift_rows = cv_shift.shape[1]
    for r in range(1, SUBLANES):
        cv_shift[r - 1] = cv_ext[r:r + shift_rows, :]

    tap0 = CONV_HALO - (CONV_KERNEL - 1)
    n_blocks = CONV_WIDTH // LANES
    gate_cols = 2 * D_MODEL // n_blocks
    for blk in range(n_blocks):
        c0 = blk * LANES
        d0 = COL_GATE_POOL + blk * gate_cols
        gate_ref[:, blk * gate_cols:(blk + 1) * gate_cols] = jax.nn.sigmoid(
            _dot(h, w_in_ref[:, d0:d0 + gate_cols]))
        for r0 in range(0, ts, CONV_ROW_CHUNK):
            acc = jnp.broadcast_to(b_dw_ref[:, c0:c0 + LANES], (CONV_ROW_CHUNK, LANES))
            for k in range(CONV_KERNEL):
                off = tap0 + k
                mis = off % SUBLANES
                row = r0 + off - mis
                if mis == 0:
                    win = cv_ext[row:row + CONV_ROW_CHUNK, c0:c0 + LANES]
                else:
                    win = cv_shift[mis - 1, row:row + CONV_ROW_CHUNK, c0:c0 + LANES]
                acc = acc + w_dw_ref[k:k + 1, c0:c0 + LANES] * win
            cv_out[r0:r0 + CONV_ROW_CHUNK, c0:c0 + LANES] = acc

    up_ext[POOL_HALO:POOL_HALO + ts, :] = _dot(h, w_in_ref[:, COL_POOL:COL_POOL + POOL_WIDTH])
    pos = (s * ts + 1 + lax.broadcasted_iota(jnp.int32, (ts, 1), 0)).astype(_F32)
    for g, w in enumerate(POOL_WINDOWS):
        c0 = g * POOL_GROUP_IN
        win = up_ext[:, c0:c0 + POOL_GROUP_IN]
        span = 1
        while span < w:
            win = win + pltpu.roll(win, span, 0)
            span *= 2
        cur = up_ext[POOL_HALO:POOL_HALO + ts, c0:c0 + POOL_GROUP_IN]
        pooled = win[POOL_HALO:, :] / jnp.minimum(pos, float(w)) - cur
        d0 = g * POOL_GROUP_OUT
        mix_ref[:, d0:d0 + POOL_GROUP_OUT] = (
            _dot(pooled.astype(_BF16), w_pool_ref[g]) * pscale_ref[:, d0:d0 + POOL_GROUP_OUT])

    u = cv_out[...]
    mu = jnp.mean(u, axis=-1, keepdims=True)
    uc = u - mu
    var = jnp.mean(uc * uc, axis=-1, keepdims=True)
    un = uc * lax.rsqrt(var + EPS) * ln_g_ref[...] + ln_b_ref[...]
    un = un * jax.nn.sigmoid(un)
    y_conv = _dot(un.astype(_BF16), w_pw_ref[...])
    mix = gate_ref[:, 0:D_MODEL] * mix_ref[...] + gate_ref[:, D_MODEL:] * y_conv

    o_ref[...] = x + _dot(mix.astype(_BF16), w_out_ref[...])


def _mixer(layer, x, g, w_in, w_pool, pscale, w_dw, b_dw, ln_g, ln_b, w_pw, w_out):
    B, S, D = x.shape
    ts = SEQ_TILE
    shift_rows = ts + CONV_HALO - SUBLANES
    tile = pl.BlockSpec((None, ts, D), lambda b, s: (b, s, 0))
    params = (g, w_in, w_pool, pscale, w_dw, b_dw, ln_g, ln_b, w_pw, w_out)
    return pl.pallas_call(
        _mixer_kernel,
        grid=(B, S // ts),
        in_specs=[tile] + [_layer_spec(p, layer) for p in params],
        out_specs=tile,
        out_shape=jax.ShapeDtypeStruct(x.shape, x.dtype),
        scratch_shapes=[pltpu.VMEM((POOL_HALO + ts, POOL_WIDTH), _F32),
                        pltpu.VMEM((CONV_HALO + ts, CONV_WIDTH), _F32),
                        pltpu.VMEM((SUBLANES - 1, shift_rows, CONV_WIDTH), _F32),
                        pltpu.VMEM((ts, CONV_WIDTH), _F32),
                        pltpu.VMEM((ts, D), _F32),
                        pltpu.VMEM((ts, 2 * D), _F32)],
        compiler_params=pltpu.CompilerParams(
            dimension_semantics=("parallel", "arbitrary"), vmem_limit_bytes=VMEM_LIMIT),
        name="mixer",
    )(x, *params)


def _kv_kernel(m_ref, g_ref, w_ref, k_ref, v_ref):
    m = _rms(m_ref[...], g_ref[...]).astype(_BF16)
    k_ref[...] = (_dot(m, w_ref[:, 0:D_MODEL]) * (XHEAD_DIM ** -0.5)).astype(_BF16)
    v_ref[...] = _dot(m, w_ref[:, D_MODEL:2 * D_MODEL]).astype(_BF16)


def _kv_proj(layer, mem, g, w_kv):
    B, M, D = mem.shape
    blk = pl.BlockSpec((None, M, D), lambda b: (b, 0, 0))
    return pl.pallas_call(
        _kv_kernel,
        grid=(B,),
        in_specs=[blk, _layer_spec(g, layer), _layer_spec(w_kv, layer)],
        out_specs=[blk, blk],
        out_shape=[jax.ShapeDtypeStruct(mem.shape, _BF16)] * 2,
        compiler_params=pltpu.CompilerParams(
            dimension_semantics=("parallel",), vmem_limit_bytes=VMEM_LIMIT),
        name="kv_proj",
    )(mem, g, w_kv)


class _CastJob(NamedTuple):
    moe_layer: int
    first_slice: int
    w_gu: jax.Array
    w_down: jax.Array
    gu_bf: Optional[jax.Array]
    down_bf: Optional[jax.Array]


def _cast_plumbing(job, step_of, n_in, n_out):
    _, _, d, f2 = job.w_gu.shape
    f = job.w_down.shape[2]

    def idx(*grid_idx):
        q = job.first_slice + step_of(*grid_idx)
        return (job.moe_layer, q // CAST_PARTS, q % CAST_PARTS, 0)

    gu_blk = pl.BlockSpec((None, None, d // CAST_PARTS, f2), idx)
    down_blk = pl.BlockSpec((None, None, f // CAST_PARTS, d), idx)
    in_specs, args, aliases = [gu_blk, down_blk], [job.w_gu, job.w_down], {}
    if job.gu_bf is not None:
        in_specs += [pl.BlockSpec(memory_space=pl.ANY)] * 2
        args += [job.gu_bf, job.down_bf]
        aliases = {n_in + 2: n_out, n_in + 3: n_out + 1}
    out_shapes = [jax.ShapeDtypeStruct(job.w_gu.shape, _BF16),
                  jax.ShapeDtypeStruct(job.w_down.shape, _BF16)]
    return in_specs, args, [gu_blk, down_blk], out_shapes, aliases


def _split_cast_refs(rest, n_out, cast_fill):
    n_cast_in = 4 if cast_fill else 2
    gu_in, down_in = rest[0], rest[1]
    host_out = rest[n_cast_in:n_cast_in + n_out]
    gu_out, down_out = rest[n_cast_in + n_out], rest[n_cast_in + n_out + 1]
    gu_out[...] = gu_in[...].astype(_BF16)
    down_out[...] = down_in[...].astype(_BF16)
    return tuple(host_out) + tuple(rest[n_cast_in + n_out + 2:])


def _xattn_kernel(x_ref, g_ref, wq_ref, k_ref, v_ref, wo_ref, *rest, cast):
    if cast is not None:
        rest = _split_cast_refs(rest, 1, cast)
    o_ref, att_ref = rest
    x = x_ref[...]
    h = _rms(x, g_ref[...]).astype(_BF16)
    q = _dot(h, wq_ref[...]).astype(_BF16)
    for hd in range(N_XHEADS):
        c0 = hd * XHEAD_DIM
        sc = lax.dot_general(q[:, c0:c0 + XHEAD_DIM], k_ref[:, c0:c0 + XHEAD_DIM],
                             (((1,), (1,)), ((), ())), preferred_element_type=_F32)
        p = jnp.exp(sc - jnp.max(sc, axis=-1, keepdims=True))
        denom = jnp.sum(p, axis=-1, keepdims=True)
        att = _dot(p.astype(_BF16), v_ref[:, c0:c0 + XHEAD_DIM]) / denom
        att_ref[:, c0:c0 + XHEAD_DIM] = att.astype(_BF16)
    o_ref[...] = x + _dot(att_ref[...], wo_ref[...])


def _xattn(layer, x, g, w_q, k, v, w_o, job=None):
    B, S, D = x.shape
    M = k.shape[1]
    ts = SEQ_TILE
    n_seq = S // ts
    tile = pl.BlockSpec((None, ts, D), lambda b, s: (b, s, 0))
    kv_blk = pl.BlockSpec((None, M, D), lambda b, s: (b, 0, 0))
    in_specs = [tile, _layer_spec(g, layer), _layer_spec(w_q, layer), kv_blk, kv_blk,
                _layer_spec(w_o, layer)]
    args = [x, g, w_q, k, v, w_o]
    out_specs = [tile]
    out_shape = [jax.ShapeDtypeStruct(x.shape, x.dtype)]
    aliases, cast = {}, None
    if job is not None:
        assert B * n_seq == (job.w_gu.shape[1] * CAST_PARTS) // 2
        extra = _cast_plumbing(job, lambda b, s: b * n_seq + s, len(in_specs), len(out_specs))
        in_specs += extra[0]
        args += extra[1]
        out_specs += extra[2]
        out_shape += extra[3]
        aliases, cast = extra[4], job.gu_bf is not None
    out = pl.pallas_call(
        functools.partial(_xattn_kernel, cast=cast),
        grid=(B, n_seq),
        in_specs=in_specs,
        out_specs=out_specs,
        out_shape=out_shape,
        scratch_shapes=[pltpu.VMEM((ts, D), _BF16)],
        input_output_aliases=aliases,
        compiler_params=pltpu.CompilerParams(
            dimension_semantics=("parallel", "parallel"), vmem_limit_bytes=VMEM_LIMIT),
        name="xattn",
    )(*args)
    return out[0] if job is None else tuple(out)


def _ffn_kernel(x_ref, g_ref, w_gu_ref, w_down_ref, *rest, cast):
    if cast is not None:
        rest = _split_cast_refs(rest, 1, cast)
    (o_ref,) = rest
    x = x_ref[...]
    h = _rms(x, g_ref[...]).astype(_BF16)
    acc = x
    for c0, n in FF_CHUNKS:
        gate = _dot(h, w_gu_ref[:, c0:c0 + n])
        up = _dot(h, w_gu_ref[:, D_FF + c0:D_FF + c0 + n])
        act = (gate * jax.nn.sigmoid(gate) * up).astype(_BF16)
        acc = acc + _dot(act, w_down_ref[c0:c0 + n, :])
    o_ref[...] = acc


def _ffn(layer, ffn_layer, x2, g, w_gu, w_down, job=None):
    T, D = x2.shape
    tm = TOK_TILE
    tile = pl.BlockSpec((tm, D), lambda i: (i, 0))
    in_specs = [tile, _layer_spec(g, layer), _layer_spec(w_gu, ffn_layer),
                _layer_spec(w_down, ffn_layer)]
    args = [x2, g, w_gu, w_down]
    out_specs = [tile]
    out_shape = [jax.ShapeDtypeStruct(x2.shape, x2.dtype)]
    aliases, cast = {}, None
    if job is not None:
        assert T // tm == (job.w_gu.shape[1] * CAST_PARTS) // 2
        extra = _cast_plumbing(job, lambda i: i, len(in_specs), len(out_specs))
        in_specs += extra[0]
        args += extra[1]
        out_specs += extra[2]
        out_shape += extra[3]
        aliases, cast = extra[4], job.gu_bf is not None
    out = pl.pallas_call(
        functools.partial(_ffn_kernel, cast=cast),
        grid=(T // tm,),
        in_specs=in_specs,
        out_specs=out_specs,
        out_shape=out_shape,
        input_output_aliases=aliases,
        compiler_params=pltpu.CompilerParams(
            dimension_semantics=("parallel",), vmem_limit_bytes=VMEM_LIMIT),
        name="ffn",
    )(*args)
    return out[0] if job is None else tuple(out)


def _pack_halves(y):
    half = y.shape[1] // 2
    lo = lax.bitcast_convert_type(y[:, :half].astype(_BF16).astype(_F32), _U32)
    hi = lax.bitcast_convert_type(y[:, half:].astype(_BF16).astype(_F32), _U32)
    return (lo >> 16) | (hi & HIGH_HALF_MASK)


def _unpack_halves(p):
    lo = lax.bitcast_convert_type(p << 16, _F32)
    hi = lax.bitcast_convert_type(p & HIGH_HALF_MASK, _F32)
    return lo, hi


def _route_kernel(x_ref, g_ref, w_r_ref, b_r_ref, earlier_ref, lane_before_ref,
                  meta_ref, cnt_ref, tab_ref, hs_ref,
                  carry_ref, placed_smem, second_smem, sorted_buf, sem,
                  *, region_rows, sort_rows):
    i = pl.program_id(0)
    tm = x_ref.shape[0]

    @pl.when(i == 0)
    def _():
        carry_ref[...] = jnp.zeros_like(carry_ref)
        for e in range(N_EXPERTS):
            placed_smem[e] = 0
        sorted_buf[:, sort_rows:, :] = jnp.zeros(
            (sorted_buf.shape[0], sorted_buf.shape[1] - sort_rows, sorted_buf.shape[2]), _U32)

    hf = _rms(x_ref[...], g_ref[...])

    lane = lax.broadcasted_iota(jnp.int32, (tm, LANES), 1)
    h_hi = hf.astype(_BF16)
    h_lo = (hf - h_hi.astype(_F32)).astype(_BF16)
    w_r = w_r_ref[...]
    w_hi = w_r.astype(_BF16)
    w_lo = (w_r - w_hi.astype(_F32)).astype(_BF16)
    logits = _dot(h_hi, w_hi) + _dot(h_lo, w_hi) + _dot(h_hi, w_lo) + b_r_ref[...]
    v1 = jnp.max(logits, axis=-1, keepdims=True)
    i1 = jnp.min(jnp.where(logits == v1, lane, LANES), axis=-1, keepdims=True)
    rest = jnp.where(lane == i1, NEG_BIG, logits)
    v2 = jnp.max(rest, axis=-1, keepdims=True)
    i2 = jnp.min(jnp.where(rest == v2, lane, LANES), axis=-1, keepdims=True)
    z = jnp.exp(v2 - v1)
    g1 = 1.0 / (1.0 + z)
    g2 = z / (1.0 + z)

    sel1 = lane == i1
    sel2 = lane == i2
    chosen = jnp.where(sel1 | sel2, 1.0, 0.0)
    before = _dot(earlier_ref[...], chosen.astype(_BF16))
    cnt = jnp.sum(chosen, axis=0, keepdims=True)
    cnt_pad = jnp.ceil(cnt * (1.0 / SUBLANES)) * SUBLANES
    group_off = _dot(jnp.broadcast_to(cnt_pad, (SUBLANES, LANES)).astype(_BF16),
                     lane_before_ref[...])[0:1, :]
    local = group_off + before
    loc1 = jnp.sum(jnp.where(sel1, local, 0.0), axis=-1, keepdims=True)
    loc2 = jnp.sum(jnp.where(sel2, local, 0.0), axis=-1, keepdims=True)

    slot = lax.rem(i, 2)
    loc_t = jnp.where(lane == 0, loc1, jnp.where(lane == 1, loc2, -1.0)).T
    row = lax.broadcasted_iota(jnp.int32, (sort_rows, tm), 0).astype(_F32)
    perm = jnp.where((row == loc_t[0:1, :]) | (row == loc_t[1:2, :]), 1.0, 0.0).astype(_BF16)
    sorted_buf[slot, 0:sort_rows, :] = _pack_halves(_dot(perm, h_hi))

    def window(src, dst):
        return pltpu.make_async_copy(sorted_buf.at[slot, pl.ds(src, GROUP_WINDOW)],
                                     hs_ref.at[pl.ds(dst, GROUP_WINDOW)], sem)

    def wait_windows(second_flags):
        for e in range(N_EXPERTS):
            window(0, 0).wait()

            @pl.when(second_flags(e) != 0)
            def _():
                window(0, 0).wait()

    @pl.when(i > 0)
    def _():
        wait_windows(lambda e: second_smem[e])

    off_i = group_off.astype(jnp.int32)
    cnt_i = cnt_pad.astype(jnp.int32)
    for e in range(N_EXPERTS):
        src = pl.multiple_of(off_i[0, e], SUBLANES)
        start = pl.multiple_of(e * region_rows + placed_smem[e], SUBLANES)
        rows = cnt_i[0, e]
        second = (rows > GROUP_WINDOW).astype(jnp.int32)
        window(src, start).start()

        @pl.when(second != 0)
        def _():
            window(src + GROUP_WINDOW, start + GROUP_WINDOW).start()

        second_smem[e] = second
        placed_smem[e] = placed_smem[e] + rows
        tab_ref[0, e] = start
        tab_ref[0, N_EXPERTS + e] = rows
    carry_ref[...] += cnt_pad
    cnt_ref[...] = carry_ref[...]

    @pl.when(i == pl.num_programs(0) - 1)
    def _():
        wait_windows(lambda e: second_smem[e])

        def tail(e, part):
            dst = pl.multiple_of(e * region_rows + placed_smem[e], SUBLANES) + part * GROUP_WINDOW
            return window(sort_rows, dst)

        for e in range(N_EXPERTS):
            for part in range(EXPERT_ROW_TILE // GROUP_WINDOW):
                tail(e, part).start()
        for e in range(N_EXPERTS):
            for part in range(EXPERT_ROW_TILE // GROUP_WINDOW):
                tail(e, part).wait()

    meta = jnp.where(lane == META_G1, g1, 0.0)
    meta = jnp.where(lane == META_G2, g2, meta)
    meta = jnp.where(lane == META_L1, loc1, meta)
    meta = jnp.where(lane == META_L2, loc2, meta)
    meta_ref[...] = meta


def _region_rows(n_tokens):
    pad = (n_tokens // ROUTE_TILE) * (SUBLANES - 1) + 2 * EXPERT_ROW_TILE
    return -(-(n_tokens + pad) // EXPERT_ROW_TILE) * EXPERT_ROW_TILE


def _route(layer, moe_layer, x2, g, w_r, b_r):
    T, D = x2.shape
    tm = ROUTE_TILE
    region_rows = _region_rows(T)
    sort_rows = TOP_K * tm + N_EXPERTS * SUBLANES
    earlier = jnp.asarray(np.tril(np.ones((tm, tm), np.float32), -1), _BF16)
    lane_before = jnp.asarray(np.triu(np.ones((LANES, LANES), np.float32), 1), _BF16)
    const = lambda n: pl.BlockSpec((n, n), lambda i: (0, 0), pipeline_mode=pl.Buffered(1))
    return pl.pallas_call(
        functools.partial(_route_kernel, region_rows=region_rows, sort_rows=sort_rows),
        grid=(T // tm,),
        in_specs=[pl.BlockSpec((tm, D), lambda i: (i, 0)), _layer_spec(g, layer),
                  _layer_spec(w_r, moe_layer), _layer_spec(b_r, moe_layer),
                  const(tm), const(LANES)],
        out_specs=[pl.BlockSpec((tm, LANES), lambda i: (i, 0)),
                   pl.BlockSpec((1, LANES), lambda i: (0, 0)),
                   pl.BlockSpec((None, 1, 2 * N_EXPERTS), lambda i: (i, 0, 0),
                                memory_space=pltpu.SMEM),
                   pl.BlockSpec(memory_space=pl.ANY)],
        out_shape=[jax.ShapeDtypeStruct((T, LANES), _F32),
                   jax.ShapeDtypeStruct((1, LANES), _F32),
                   jax.ShapeDtypeStruct((T // tm, 1, 2 * N_EXPERTS), jnp.int32),
                   jax.ShapeDtypeStruct((N_EXPERTS * region_rows, D // 2), _U32)],
        scratch_shapes=[pltpu.VMEM((1, LANES), _F32),
                        pltpu.SMEM((N_EXPERTS,), jnp.int32),
                        pltpu.SMEM((N_EXPERTS,), jnp.int32),
                        pltpu.VMEM((2, sort_rows + GROUP_WINDOW, D // 2), _U32),
                        pltpu.SemaphoreType.DMA],
        compiler_params=pltpu.CompilerParams(
            dimension_semantics=("arbitrary",), vmem_limit_bytes=VMEM_LIMIT),
        name="moe_route",
    )(x2, g, w_r, b_r, earlier, lane_before)


def _expert_kernel(te_ref, tb_ref, tv_ref, hs_ref, w_gu_ref, w_down_ref, ys_ref):
    del te_ref, tb_ref
    i = pl.program_id(0)

    @pl.when(tv_ref[i] == 0)
    def _():
        ys_ref[...] = jnp.zeros_like(ys_ref)

    @pl.when(tv_ref[i] != 0)
    def _():
        lo, hi = _unpack_halves(hs_ref[...])
        h = jnp.concatenate([lo.astype(_BF16), hi.astype(_BF16)], axis=1)
        acc = None
        for c0, n in FF_CHUNKS:
            gate = _dot(h, w_gu_ref[:, c0:c0 + n])
            up = _dot(h, w_gu_ref[:, D_FF + c0:D_FF + c0 + n])
            act = (gate * jax.nn.sigmoid(gate) * up).astype(_BF16)
            part = _dot(act, w_down_ref[c0:c0 + n, :])
            acc = part if acc is None else acc + part
        ys_ref[...] = _pack_halves(acc)


def _experts(moe_layer, tile_expert, tile_block, tile_valid, hs, w_gu, w_down):
    n_rows, W = hs.shape
    rt = EXPERT_ROW_TILE
    D = w_gu.shape[2]
    grid_spec = pltpu.PrefetchScalarGridSpec(
        num_scalar_prefetch=3,
        grid=(tile_expert.shape[0],),
        in_specs=[pl.BlockSpec((rt, W), lambda i, te, tb, tv: (tb[i] * tv[i], 0)),
                  pl.BlockSpec((None, None, D, 2 * D_FF), lambda i, te, tb, tv: (moe_layer, te[i], 0, 0)),
                  pl.BlockSpec((None, None, D_FF, D), lambda i, te, tb, tv: (moe_layer, te[i], 0, 0))],
        out_specs=pl.BlockSpec((rt, W), lambda i, te, tb, tv: (tb[i], 0)),
    )
    return pl.pallas_call(
        _expert_kernel,
        grid_spec=grid_spec,
        out_shape=jax.ShapeDtypeStruct((n_rows + rt, W), hs.dtype),
        compiler_params=pltpu.CompilerParams(
            dimension_semantics=("arbitrary",), vmem_limit_bytes=VMEM_LIMIT),
        name="moe_experts",
    )(tile_expert, tile_block, tile_valid, hs, w_gu, w_down)


def _combine_kernel(tab_ref, x_ref, meta_ref, g_final_ref, ys_ref, o_ref, ybuf, sem, *, final_norm):
    i = pl.program_id(0)
    n = pl.num_programs(0)
    tm, D = x_ref.shape
    slot = lax.rem(i, 2)

    @pl.when(i == 0)
    def _():
        ybuf[...] = jnp.zeros_like(ybuf)

    def fetch(tile, to_slot):
        off = 0
        for e in range(N_EXPERTS):
            start = tab_ref[tile, e]
            rows = tab_ref[tile, N_EXPERTS + e]
            size = ROUTE_TILE
            while size >= SUBLANES:
                above = rows & ~(2 * size - 1)
                src = pl.multiple_of(start + above, SUBLANES)
                dst = pl.multiple_of(off + above, SUBLANES)

                @pl.when((rows & size) != 0)
                def _():
                    pltpu.make_async_copy(ys_ref.at[pl.ds(src, size)],
                                          ybuf.at[to_slot, pl.ds(dst, size)], sem.at[to_slot]).start()
                size //= 2
            off = off + rows
        return off

    def total_rows(tile):
        total = 0
        for e in range(N_EXPERTS):
            total = total + tab_ref[tile, N_EXPERTS + e]
        return total

    @pl.when(i == 0)
    def _():
        fetch(0, 0)

    @pl.when(i + 1 < n)
    def _():
        fetch(i + 1, 1 - slot)

    landed = pl.multiple_of(total_rows(i), SUBLANES)
    pltpu.make_async_copy(ys_ref.at[pl.ds(0, landed)], ybuf.at[slot, pl.ds(0, landed)],
                          sem.at[slot]).wait()

    lo, hi = _unpack_halves(ybuf[slot])
    y = jnp.concatenate([lo.astype(_BF16), hi.astype(_BF16)], axis=1)
    col = lax.broadcasted_iota(jnp.int32, (tm, ybuf.shape[1]), 1).astype(_F32)
    pick1 = jnp.where(col == meta_ref[:, META_L1:META_L1 + 1], 1.0, 0.0).astype(_BF16)
    pick2 = jnp.where(col == meta_ref[:, META_L2:META_L2 + 1], 1.0, 0.0).astype(_BF16)
    out = (x_ref[...] + meta_ref[:, META_G1:META_G1 + 1] * _dot(pick1, y)
           + meta_ref[:, META_G2:META_G2 + 1] * _dot(pick2, y))
    if final_norm:
        out = _rms(out, g_final_ref[...])
    o_ref[...] = out


def _combine(tab, x2, meta, ys, g_final, final_norm):
    T, D = x2.shape
    tm = ROUTE_TILE
    sort_rows = -(-(TOP_K * tm + N_EXPERTS * SUBLANES) // MXU_DIM) * MXU_DIM
    tile = pl.BlockSpec((tm, D), lambda i, tab: (i, 0))
    grid_spec = pltpu.PrefetchScalarGridSpec(
        num_scalar_prefetch=1,
        grid=(T // tm,),
        in_specs=[tile, pl.BlockSpec((tm, LANES), lambda i, tab: (i, 0)),
                  pl.BlockSpec((1, D), lambda i, tab: (0, 0)), pl.BlockSpec(memory_space=pl.ANY)],
        out_specs=tile,
        scratch_shapes=[pltpu.VMEM((2, sort_rows, D // 2), _U32), pltpu.SemaphoreType.DMA((2,))],
    )
    return pl.pallas_call(
        functools.partial(_combine_kernel, final_norm=final_norm),
        grid_spec=grid_spec,
        out_shape=jax.ShapeDtypeStruct(x2.shape, x2.dtype),
        compiler_params=pltpu.CompilerParams(
            dimension_semantics=("arbitrary",), vmem_limit_bytes=VMEM_LIMIT),
        name="moe_combine",
    )(tab, x2, meta, g_final, ys)


def _moe(layer, moe_layer, x2, g, w_r, b_r, w_gu, w_down, g_final, final_norm):
    T, D = x2.shape
    E = w_gu.shape[1]
    rt = EXPERT_ROW_TILE
    meta, counts, tab, hs = _route(layer, moe_layer, x2, g, w_r, b_r)

    region_tiles = hs.shape[0] // (E * rt)
    max_rows = TOP_K * T + (T // ROUTE_TILE) * E * (SUBLANES - 1)
    n_tiles = -(-max_rows // rt) + E
    counts = counts[0, :E].astype(jnp.int32)
    group_tiles = (counts + rt - 1) // rt
    tile_end = jnp.cumsum(group_tiles)
    tile_ids = jnp.arange(n_tiles, dtype=jnp.int32)
    tile_expert = jnp.minimum(
        jnp.sum((tile_ids[:, None] >= tile_end[None, :]).astype(jnp.int32), axis=1), E - 1)
    tile_valid = (tile_ids < tile_end[E - 1]).astype(jnp.int32)
    in_group = tile_ids - (tile_end - group_tiles)[tile_expert]
    tile_block = jnp.where(tile_valid != 0, tile_expert * region_tiles + in_group, E * region_tiles)

    ys = _experts(moe_layer, tile_expert, tile_block.astype(jnp.int32), tile_valid, hs, w_gu, w_down)
    return _combine(tab.reshape(tab.shape[0], tab.shape[2]), x2, meta, ys, g_final, final_norm)


def _final_norm_kernel(x_ref, g_ref, o_ref):
    o_ref[...] = _rms(x_ref[...], g_ref[...])


def _final_norm(x2, g):
    T, D = x2.shape
    tm = TOK_TILE
    tile = pl.BlockSpec((tm, D), lambda i: (i, 0))
    return pl.pallas_call(
        _final_norm_kernel,
        grid=(T // tm,),
        in_specs=[tile, pl.BlockSpec((1, D), lambda i: (0, 0))],
        out_specs=tile,
        out_shape=jax.ShapeDtypeStruct(x2.shape, x2.dtype),
        compiler_params=pltpu.CompilerParams(dimension_semantics=("parallel",)),
        name="final_norm",
    )(x2, g)


def kernel(x, mem, g_mix, w_in, w_pool, pool_scale, w_dw, b_dw, conv_ln_g, conv_ln_b, w_conv_out, w_mix_out, g_xattn, g_mem, w_xq, w_xkv, w_xo, g_ffn, w_ffn_gu, w_ffn_down, w_router, b_router, w_moe_gu, w_moe_down, g_final):
    B, S, D = x.shape
    depth = g_mix.shape[0]
    bf = lambda a: a.astype(_BF16)
    rows = lambda a: a.reshape(a.shape[0], 1, a.shape[1])

    g_mix, pool_scale, b_dw, conv_ln_g, conv_ln_b = map(rows, (g_mix, pool_scale, b_dw, conv_ln_g, conv_ln_b))
    g_xattn, g_mem, g_ffn = map(rows, (g_xattn, g_mem, g_ffn))
    w_dw = w_dw.reshape(depth, CONV_KERNEL, CONV_WIDTH)
    w_in, w_pool, w_conv_out, w_mix_out = map(bf, (w_in, w_pool, w_conv_out, w_mix_out))
    w_xq, w_xkv, w_xo = map(bf, (w_xq, w_xkv, w_xo))
    w_ffn_gu, w_ffn_down = map(bf, (w_ffn_gu, w_ffn_down))
    w_r_pad = jnp.pad(w_router, ((0, 0), (0, 0), (0, LANES - N_EXPERTS)))
    b_r_pad = rows(jnp.pad(b_router, ((0, 0), (0, LANES - N_EXPERTS)), constant_values=NEG_BIG))
    g_final = g_final.reshape(1, D)

    moe_gu_bf = moe_down_bf = None
    half_slices = (w_moe_gu.shape[1] * CAST_PARTS) // 2
    for i in range(depth):
        x = _mixer(i, x, g_mix, w_in, w_pool, pool_scale, w_dw, b_dw, conv_ln_g, conv_ln_b,
                   w_conv_out, w_mix_out)
        k, v = _kv_proj(i, mem, g_mem, w_xkv)
        last = i == depth - 1
        if i % 2 == 0:
            feeds_moe = not last
            job = lambda first: _CastJob((i + 1) // 2, first, w_moe_gu, w_moe_down, moe_gu_bf, moe_down_bf)
            if feeds_moe:
                x, moe_gu_bf, moe_down_bf = _xattn(i, x, g_xattn, w_xq, k, v, w_xo, job(0))
                x2, moe_gu_bf, moe_down_bf = _ffn(i, i // 2, x.reshape(B * S, D), g_ffn, w_ffn_gu,
                                                  w_ffn_down, job(half_slices))
            else:
                x = _xattn(i, x, g_xattn, w_xq, k, v, w_xo)
                x2 = _ffn(i, i // 2, x.reshape(B * S, D), g_ffn, w_ffn_gu, w_ffn_down)
            if last:
                x2 = _final_norm(x2, g_final)
        else:
            x = _xattn(i, x, g_xattn, w_xq, k, v, w_xo)
            x2 = _moe(i, i // 2, x.reshape(B * S, D), g_ffn, w_r_pad, b_r_pad, moe_gu_bf, moe_down_bf,
                      g_final, last)
        x = x2.reshape(B, S, D)
    return x
```

```python
import functools
from typing import NamedTuple, Optional

import jax
import jax.numpy as jnp
import numpy as np
from jax import lax
from jax.experimental import pallas as pl
from jax.experimental.pallas import tpu as pltpu

D_MODEL = 1024
POOL_WIDTH = 512
POOL_GROUPS = 4
POOL_WINDOWS = (2, 4, 8, 16)
POOL_GROUP_IN = POOL_WIDTH // POOL_GROUPS
POOL_GROUP_OUT = D_MODEL // POOL_GROUPS
CONV_WIDTH = 512
CONV_KERNEL = 31
IN_COLS = POOL_WIDTH + 2 * CONV_WIDTH + 2 * D_MODEL
N_XHEADS = 4
XHEAD_DIM = D_MODEL // N_XHEADS
D_FF = 2816
N_EXPERTS = 8
TOP_K = 2
EPS = 1e-6

COL_POOL = 0
COL_GLU_A = POOL_WIDTH
COL_GLU_B = POOL_WIDTH + CONV_WIDTH
COL_GATE_POOL = POOL_WIDTH + 2 * CONV_WIDTH
COL_GATE_CONV = COL_GATE_POOL + D_MODEL

LANES = 128
SUBLANES = 8
POOL_HALO = 16
CONV_HALO = 32
SEQ_TILE = 512
TOK_TILE = 512
MXU_DIM = 256
FF_CHUNKS = ((0, 6 * MXU_DIM), (6 * MXU_DIM, 5 * MXU_DIM))
assert sum(n for _, n in FF_CHUNKS) == D_FF
CONV_ROW_CHUNK = 64
ROUTE_TILE = 512
EXPERT_ROW_TILE = 512
GROUP_WINDOW = 256
CAST_PARTS = 8
VMEM_LIMIT = 56 * 1024 * 1024
NEG_BIG = -1e30
META_G1, META_G2, META_L1, META_L2 = range(4)

_F32 = jnp.float32
_BF16 = jnp.bfloat16
_U32 = jnp.uint32
HIGH_HALF_MASK = np.uint32(0xFFFF0000)


def _layer_spec(arr, layer):
    tail = (0,) * (arr.ndim - 1)
    return pl.BlockSpec((None,) + arr.shape[1:], lambda *_: (layer,) + tail,
                        pipeline_mode=pl.Buffered(1))


def _rms(x, g):
    return x * lax.rsqrt(jnp.mean(x * x, axis=-1, keepdims=True) + EPS) * g


def _dot(a, b):
    return jnp.dot(a, b, preferred_element_type=_F32)


def _zero_like_bits(v):
    u = lax.bitcast_convert_type(v, _U32)
    return lax.bitcast_convert_type((u >> 16) >> 16, _F32)


def _mixer_kernel(x_ref, g_ref, w_in_ref, w_pool_ref, pscale_ref, w_dw_ref, b_dw_ref,
                  ln_g_ref, ln_b_ref, w_pw_ref, w_out_ref, o_ref,
                  up_ext, cv_ext, cv_shift, cv_out, mix_ref, gate_ref):
    s = pl.program_id(1)
    ts = x_ref.shape[0]

    @pl.when(s == 0)
    def _():
        up_ext[0:POOL_HALO, :] = jnp.zeros((POOL_HALO, POOL_WIDTH), _F32)
        cv_ext[0:CONV_HALO, :] = jnp.zeros((CONV_HALO, CONV_WIDTH), _F32)

    @pl.when(s > 0)
    def _():
        up_ext[0:POOL_HALO, :] = up_ext[ts:ts + POOL_HALO, :]
        cv_ext[0:CONV_HALO, :] = cv_ext[ts:ts + CONV_HALO, :]

    x = x_ref[...]
    h = _rms(x, g_ref[...]).astype(_BF16)

    glu_a = _dot(h, w_in_ref[:, COL_GLU_A:COL_GLU_A + CONV_WIDTH])
    glu_b = _dot(h, w_in_ref[:, COL_GLU_B:COL_GLU_B + CONV_WIDTH])
    cv_ext[CONV_HALO:CONV_HALO + ts, :] = glu_a * jax.nn.sigmoid(glu_b)

    shift_rows = cv_shift.shape[1]
    for r in range(1, SUBLANES):
        cv_shift[r - 1] = cv_ext[r:r + shift_rows, :]

    tap0 = CONV_HALO - (CONV_KERNEL - 1)
    def side_task(col0, store):
        res = _dot(h, w_in_ref[:, col0:col0 + MXU_DIM])
        store(res)
        return _zero_like_bits(res[ts - 1:ts, MXU_DIM - LANES:])

    def pool_store(c):
        def store(res):
            up_ext[POOL_HALO:POOL_HALO + ts, c:c + MXU_DIM] = res
        return store

    def gate_store(c):
        def store(res):
            gate_ref[:, c:c + MXU_DIM] = jax.nn.sigmoid(res)
        return store

    side_tasks = ([(COL_POOL + c, pool_store(c)) for c in range(0, POOL_WIDTH, MXU_DIM)]
                  + [(COL_GATE_POOL + c, gate_store(c)) for c in range(0, 2 * D_MODEL, MXU_DIM)])
    unit_rows = 2 * CONV_ROW_CHUNK
    units = [(c0, q0) for c0 in range(0, CONV_WIDTH, LANES) for q0 in range(0, ts, unit_rows)]
    assert len(side_tasks) <= len(units)
    pace = None
    for u, (c0, q0) in enumerate(units):
        bias = b_dw_ref[:, c0:c0 + LANES]
        if pace is not None:
            bias = bias + pace
        pace = side_task(*side_tasks[u]) if u < len(side_tasks) else None
        for r0 in range(q0, q0 + unit_rows, CONV_ROW_CHUNK):
            acc = jnp.broadcast_to(bias, (CONV_ROW_CHUNK, LANES))
            for k in range(CONV_KERNEL):
                off = tap0 + k
                mis = off % SUBLANES
                row = r0 + off - mis
                if mis == 0:
                    win = cv_ext[row:row + CONV_ROW_CHUNK, c0:c0 + LANES]
                else:
                    win = cv_shift[mis - 1, row:row + CONV_ROW_CHUNK, c0:c0 + LANES]
                acc = acc + w_dw_ref[k:k + 1, c0:c0 + LANES] * win
            cv_out[r0:r0 + CONV_ROW_CHUNK, c0:c0 + LANES] = acc

    pos = (s * ts + 1 + lax.broadcasted_iota(jnp.int32, (ts, 1), 0)).astype(_F32)
    for g, w in enumerate(POOL_WINDOWS):
        c0 = g * POOL_GROUP_IN
        win = up_ext[:, c0:c0 + POOL_GROUP_IN]
        span = 1
        while span < w:
            win = win + pltpu.roll(win, span, 0)
            span *= 2
        cur = up_ext[POOL_HALO:POOL_HALO + ts, c0:c0 + POOL_GROUP_IN]
        pooled = win[POOL_HALO:, :] / jnp.minimum(pos, float(w)) - cur
        d0 = g * POOL_GROUP_OUT
        mix_ref[:, d0:d0 + POOL_GROUP_OUT] = (
            _dot(pooled.astype(_BF16), w_pool_ref[g]) * pscale_ref[:, d0:d0 + POOL_GROUP_OUT])

    u = cv_out[...]
    mu = jnp.mean(u, axis=-1, keepdims=True)
    uc = u - mu
    var = jnp.mean(uc * uc, axis=-1, keepdims=True)
    un = uc * lax.rsqrt(var + EPS) * ln_g_ref[...] + ln_b_ref[...]
    un = un * jax.nn.sigmoid(un)
    y_conv = _dot(un.astype(_BF16), w_pw_ref[...])
    mix = gate_ref[:, 0:D_MODEL] * mix_ref[...] + gate_ref[:, D_MODEL:] * y_conv

    o_ref[...] = x + _dot(mix.astype(_BF16), w_out_ref[...])


def _mixer(layer, x, g, w_in, w_pool, pscale, w_dw, b_dw, ln_g, ln_b, w_pw, w_out):
    B, S, D = x.shape
    ts = SEQ_TILE
    shift_rows = ts + CONV_HALO - SUBLANES
    tile = pl.BlockSpec((None, ts, D), lambda b, s: (b, s, 0))
    params = (g, w_in, w_pool, pscale, w_dw, b_dw, ln_g, ln_b, w_pw, w_out)
    return pl.pallas_call(
        _mixer_kernel,
        grid=(B, S // ts),
        in_specs=[tile] + [_layer_spec(p, layer) for p in params],
        out_specs=tile,
        out_shape=jax.ShapeDtypeStruct(x.shape, x.dtype),
        scratch_shapes=[pltpu.VMEM((POOL_HALO + ts, POOL_WIDTH), _F32),
                        pltpu.VMEM((CONV_HALO + ts, CONV_WIDTH), _F32),
                        pltpu.VMEM((SUBLANES - 1, shift_rows, CONV_WIDTH), _F32),
                        pltpu.VMEM((ts, CONV_WIDTH), _F32),
                        pltpu.VMEM((ts, D), _F32),
                        pltpu.VMEM((ts, 2 * D), _F32)],
        compiler_params=pltpu.CompilerParams(
            dimension_semantics=("parallel", "arbitrary"), vmem_limit_bytes=VMEM_LIMIT),
        name="mixer",
    )(x, *params)


def _kv_kernel(m_ref, g_ref, w_ref, k_ref, v_ref):
    m = _rms(m_ref[...], g_ref[...]).astype(_BF16)
    k_ref[...] = (_dot(m, w_ref[:, 0:D_MODEL]) * (XHEAD_DIM ** -0.5)).astype(_BF16)
    v_ref[...] = _dot(m, w_ref[:, D_MODEL:2 * D_MODEL]).astype(_BF16)


def _kv_proj(layer, mem, g, w_kv):
    B, M, D = mem.shape
    blk = pl.BlockSpec((None, M, D), lambda b: (b, 0, 0))
    return pl.pallas_call(
        _kv_kernel,
        grid=(B,),
        in_specs=[blk, _layer_spec(g, layer), _layer_spec(w_kv, layer)],
        out_specs=[blk, blk],
        out_shape=[jax.ShapeDtypeStruct(mem.shape, _BF16)] * 2,
        compiler_params=pltpu.CompilerParams(
            dimension_semantics=("parallel",), vmem_limit_bytes=VMEM_LIMIT),
        name="kv_proj",
    )(mem, g, w_kv)


class _CastJob(NamedTuple):
    moe_layer: int
    first_slice: int
    w_gu: jax.Array
    w_down: jax.Array
    gu_bf: Optional[jax.Array]
    down_bf: Optional[jax.Array]


def _cast_plumbing(job, step_of, n_in, n_out):
    _, _, d, f2 = job.w_gu.shape
    f = job.w_down.shape[2]

    def idx(*grid_idx):
        q = job.first_slice + step_of(*grid_idx)
        return (job.moe_layer, q // CAST_PARTS, q % CAST_PARTS, 0)

    gu_blk = pl.BlockSpec((None, None, d // CAST_PARTS, f2), idx)
    down_blk = pl.BlockSpec((None, None, f // CAST_PARTS, d), idx)
    in_specs, args, aliases = [gu_blk, down_blk], [job.w_gu, job.w_down], {}
    if job.gu_bf is not None:
        in_specs += [pl.BlockSpec(memory_space=pl.ANY)] * 2
        args += [job.gu_bf, job.down_bf]
        aliases = {n_in + 2: n_out, n_in + 3: n_out + 1}
    out_shapes = [jax.ShapeDtypeStruct(job.w_gu.shape, _BF16),
                  jax.ShapeDtypeStruct(job.w_down.shape, _BF16)]
    return in_specs, args, [gu_blk, down_blk], out_shapes, aliases


def _split_cast_refs(rest, n_out, cast_fill):
    n_cast_in = 4 if cast_fill else 2
    gu_in, down_in = rest[0], rest[1]
    host_out = rest[n_cast_in:n_cast_in + n_out]
    gu_out, down_out = rest[n_cast_in + n_out], rest[n_cast_in + n_out + 1]
    gu_out[...] = gu_in[...].astype(_BF16)
    down_out[...] = down_in[...].astype(_BF16)
    return tuple(host_out) + tuple(rest[n_cast_in + n_out + 2:])


def _xattn_kernel(x_ref, g_ref, wq_ref, k_ref, v_ref, wo_ref, *rest, cast):
    if cast is not None:
        rest = _split_cast_refs(rest, 1, cast)
    o_ref, att_ref = rest
    x = x_ref[...]
    h = _rms(x, g_ref[...]).astype(_BF16)
    q = _dot(h, wq_ref[...]).astype(_BF16)
    for hd in range(N_XHEADS):
        c0 = hd * XHEAD_DIM
        sc = lax.dot_general(q[:, c0:c0 + XHEAD_DIM], k_ref[:, c0:c0 + XHEAD_DIM],
                             (((1,), (1,)), ((), ())), preferred_element_type=_F32)
        p = jnp.exp(sc - jnp.max(sc, axis=-1, keepdims=True))
        denom = jnp.sum(p, axis=-1, keepdims=True)
        att = _dot(p.astype(_BF16), v_ref[:, c0:c0 + XHEAD_DIM]) / denom
        att_ref[:, c0:c0 + XHEAD_DIM] = att.astype(_BF16)
    o_ref[...] = x + _dot(att_ref[...], wo_ref[...])


def _xattn(layer, x, g, w_q, k, v, w_o, job=None):
    B, S, D = x.shape
    M = k.shape[1]
    ts = SEQ_TILE
    n_seq = S // ts
    tile = pl.BlockSpec((None, ts, D), lambda b, s: (b, s, 0))
    kv_blk = pl.BlockSpec((None, M, D), lambda b, s: (b, 0, 0))
    in_specs = [tile, _layer_spec(g, layer), _layer_spec(w_q, layer), kv_blk, kv_blk,
                _layer_spec(w_o, layer)]
    args = [x, g, w_q, k, v, w_o]
    out_specs = [tile]
    out_shape = [jax.ShapeDtypeStruct(x.shape, x.dtype)]
    aliases, cast = {}, None
    if job is not None:
        assert B * n_seq == (job.w_gu.shape[1] * CAST_PARTS) // 2
        extra = _cast_plumbing(job, lambda b, s: b * n_seq + s, len(in_specs), len(out_specs))
        in_specs += extra[0]
        args += extra[1]
        out_specs += extra[2]
        out_shape += extra[3]
        aliases, cast = extra[4], job.gu_bf is not None
    out = pl.pallas_call(
        functools.partial(_xattn_kernel, cast=cast),
        grid=(B, n_seq),
        in_specs=in_specs,
        out_specs=out_specs,
        out_shape=out_shape,
        scratch_shapes=[pltpu.VMEM((ts, D), _BF16)],
        input_output_aliases=aliases,
        compiler_params=pltpu.CompilerParams(
            dimension_semantics=("parallel", "parallel"), vmem_limit_bytes=VMEM_LIMIT),
        name="xattn",
    )(*args)
    return out[0] if job is None else tuple(out)


def _ffn_kernel(x_ref, g_ref, w_gu_ref, w_down_ref, *rest, cast):
    if cast is not None:
        rest = _split_cast_refs(rest, 1, cast)
    (o_ref,) = rest
    x = x_ref[...]
    h = _rms(x, g_ref[...]).astype(_BF16)
    acc = x
    for c0, n in FF_CHUNKS:
        gate = _dot(h, w_gu_ref[:, c0:c0 + n])
        up = _dot(h, w_gu_ref[:, D_FF + c0:D_FF + c0 + n])
        act = (gate * jax.nn.sigmoid(gate) * up).astype(_BF16)
        acc = acc + _dot(act, w_down_ref[c0:c0 + n, :])
    o_ref[...] = acc


def _ffn(layer, ffn_layer, x2, g, w_gu, w_down, job=None):
    T, D = x2.shape
    tm = TOK_TILE
    tile = pl.BlockSpec((tm, D), lambda i: (i, 0))
    in_specs = [tile, _layer_spec(g, layer), _layer_spec(w_gu, ffn_layer),
                _layer_spec(w_down, ffn_layer)]
    args = [x2, g, w_gu, w_down]
    out_specs = [tile]
    out_shape = [jax.ShapeDtypeStruct(x2.shape, x2.dtype)]
    aliases, cast = {}, None
    if job is not None:
        assert T // tm == (job.w_gu.shape[1] * CAST_PARTS) // 2
        extra = _cast_plumbing(job, lambda i: i, len(in_specs), len(out_specs))
        in_specs += extra[0]
        args += extra[1]
        out_specs += extra[2]
        out_shape += extra[3]
        aliases, cast = extra[4], job.gu_bf is not None
    out = pl.pallas_call(
        functools.partial(_ffn_kernel, cast=cast),
        grid=(T // tm,),
        in_specs=in_specs,
        out_specs=out_specs,
        out_shape=out_shape,
        input_output_aliases=aliases,
        compiler_params=pltpu.CompilerParams(
            dimension_semantics=("parallel",), vmem_limit_bytes=VMEM_LIMIT),
        name="ffn",
    )(*args)
    return out[0] if job is None else tuple(out)


def _pack_halves(y):
    half = y.shape[1] // 2
    lo = lax.bitcast_convert_type(y[:, :half].astype(_BF16).astype(_F32), _U32)
    hi = lax.bitcast_convert_type(y[:, half:].astype(_BF16).astype(_F32), _U32)
    return (lo >> 16) | (hi & HIGH_HALF_MASK)


def _unpack_halves(p):
    lo = lax.bitcast_convert_type(p << 16, _F32)
    hi = lax.bitcast_convert_type(p & HIGH_HALF_MASK, _F32)
    return lo, hi


def _route_kernel(x_ref, g_ref, w_r_ref, b_r_ref, earlier_ref, lane_before_ref,
                  meta_ref, cnt_ref, tab_ref, hs_ref,
                  carry_ref, placed_smem, second_smem, sorted_buf, sem,
                  *, region_rows, sort_rows):
    i = pl.program_id(0)
    tm = x_ref.shape[0]

    @pl.when(i == 0)
    def _():
        carry_ref[...] = jnp.zeros_like(carry_ref)
        for e in range(N_EXPERTS):
            placed_smem[e] = 0
        sorted_buf[:, sort_rows:, :] = jnp.zeros(
            (sorted_buf.shape[0], sorted_buf.shape[1] - sort_rows, sorted_buf.shape[2]), _U32)

    hf = _rms(x_ref[...], g_ref[...])

    lane = lax.broadcasted_iota(jnp.int32, (tm, LANES), 1)
    h_hi = hf.astype(_BF16)
    h_lo = (hf - h_hi.astype(_F32)).astype(_BF16)
    w_r = w_r_ref[...]
    w_hi = w_r.astype(_BF16)
    w_lo = (w_r - w_hi.astype(_F32)).astype(_BF16)
    logits = _dot(h_hi, w_hi) + _dot(h_lo, w_hi) + _dot(h_hi, w_lo) + b_r_ref[...]
    v1 = jnp.max(logits, axis=-1, keepdims=True)
    i1 = jnp.min(jnp.where(logits == v1, lane, LANES), axis=-1, keepdims=True)
    rest = jnp.where(lane == i1, NEG_BIG, logits)
    v2 = jnp.max(rest, axis=-1, keepdims=True)
    i2 = jnp.min(jnp.where(rest == v2, lane, LANES), axis=-1, keepdims=True)
    z = jnp.exp(v2 - v1)
    g1 = 1.0 / (1.0 + z)
    g2 = z / (1.0 + z)

    sel1 = lane == i1
    sel2 = lane == i2
    chosen = jnp.where(sel1 | sel2, 1.0, 0.0)
    before = _dot(earlier_ref[...], chosen.astype(_BF16))
    cnt = jnp.sum(chosen, axis=0, keepdims=True)
    cnt_pad = jnp.ceil(cnt * (1.0 / SUBLANES)) * SUBLANES
    group_off = _dot(jnp.broadcast_to(cnt_pad, (SUBLANES, LANES)).astype(_BF16),
                     lane_before_ref[...])[0:1, :]
    local = group_off + before
    loc1 = jnp.sum(jnp.where(sel1, local, 0.0), axis=-1, keepdims=True)
    loc2 = jnp.sum(jnp.where(sel2, local, 0.0), axis=-1, keepdims=True)

    slot = lax.rem(i, 2)
    loc_t = jnp.where(lane == 0, loc1, jnp.where(lane == 1, loc2, -1.0)).T
    row = lax.broadcasted_iota(jnp.int32, (sort_rows, tm), 0).astype(_F32)
    perm = jnp.where((row == loc_t[0:1, :]) | (row == loc_t[1:2, :]), 1.0, 0.0).astype(_BF16)
    sorted_buf[slot, 0:sort_rows, :] = _pack_halves(_dot(perm, h_hi))

    def window(src, dst):
        return pltpu.make_async_copy(sorted_buf.at[slot, pl.ds(src, GROUP_WINDOW)],
                                     hs_ref.at[pl.ds(dst, GROUP_WINDOW)], sem)

    def wait_windows(second_flags):
        for e in range(N_EXPERTS):
            window(0, 0).wait()

            @pl.when(second_flags(e) != 0)
            def _():
                window(0, 0).wait()

    @pl.when(i > 0)
    def _():
        wait_windows(lambda e: second_smem[e])

    off_i = group_off.astype(jnp.int32)
    cnt_i = cnt_pad.astype(jnp.int32)
    for e in range(N_EXPERTS):
        src = pl.multiple_of(off_i[0, e], SUBLANES)
        start = pl.multiple_of(e * region_rows + placed_smem[e], SUBLANES)
        rows = cnt_i[0, e]
        second = (rows > GROUP_WINDOW).astype(jnp.int32)
        window(src, start).start()

        @pl.when(second != 0)
        def _():
            window(src + GROUP_WINDOW, start + GROUP_WINDOW).start()

        second_smem[e] = second
        placed_smem[e] = placed_smem[e] + rows
        tab_ref[0, e] = start
        tab_ref[0, N_EXPERTS + e] = rows
    carry_ref[...] += cnt_pad
    cnt_ref[...] = carry_ref[...]

    @pl.when(i == pl.num_programs(0) - 1)
    def _():
        wait_windows(lambda e: second_smem[e])

        def tail(e, part):
            dst = pl.multiple_of(e * region_rows + placed_smem[e], SUBLANES) + part * GROUP_WINDOW
            return window(sort_rows, dst)

        for e in range(N_EXPERTS):
            for part in range(EXPERT_ROW_TILE // GROUP_WINDOW):
                tail(e, part).start()
        for e in range(N_EXPERTS):
            for part in range(EXPERT_ROW_TILE // GROUP_WINDOW):
                tail(e, part).wait()

    meta = jnp.where(lane == META_G1, g1, 0.0)
    meta = jnp.where(lane == META_G2, g2, meta)
    meta = jnp.where(lane == META_L1, loc1, meta)
    meta = jnp.where(lane == META_L2, loc2, meta)
    meta_ref[...] = meta


def _region_rows(n_tokens):
    pad = (n_tokens // ROUTE_TILE) * (SUBLANES - 1) + 2 * EXPERT_ROW_TILE
    return -(-(n_tokens + pad) // EXPERT_ROW_TILE) * EXPERT_ROW_TILE


def _route(layer, moe_layer, x2, g, w_r, b_r):
    T, D = x2.shape
    tm = ROUTE_TILE
    region_rows = _region_rows(T)
    sort_rows = TOP_K * tm + N_EXPERTS * SUBLANES
    earlier = jnp.asarray(np.tril(np.ones((tm, tm), np.float32), -1), _BF16)
    lane_before = jnp.asarray(np.triu(np.ones((LANES, LANES), np.float32), 1), _BF16)
    const = lambda n: pl.BlockSpec((n, n), lambda i: (0, 0), pipeline_mode=pl.Buffered(1))
    return pl.pallas_call(
        functools.partial(_route_kernel, region_rows=region_rows, sort_rows=sort_rows),
        grid=(T // tm,),
        in_specs=[pl.BlockSpec((tm, D), lambda i: (i, 0)), _layer_spec(g, layer),
                  _layer_spec(w_r, moe_layer), _layer_spec(b_r, moe_layer),
                  const(tm), const(LANES)],
        out_specs=[pl.BlockSpec((tm, LANES), lambda i: (i, 0)),
                   pl.BlockSpec((1, LANES), lambda i: (0, 0)),
                   pl.BlockSpec((None, 1, 2 * N_EXPERTS), lambda i: (i, 0, 0),
                                memory_space=pltpu.SMEM),
                   pl.BlockSpec(memory_space=pl.ANY)],
        out_shape=[jax.ShapeDtypeStruct((T, LANES), _F32),
                   jax.ShapeDtypeStruct((1, LANES), _F32),
                   jax.ShapeDtypeStruct((T // tm, 1, 2 * N_EXPERTS), jnp.int32),
                   jax.ShapeDtypeStruct((N_EXPERTS * region_rows, D // 2), _U32)],
        scratch_shapes=[pltpu.VMEM((1, LANES), _F32),
                        pltpu.SMEM((N_EXPERTS,), jnp.int32),
                        pltpu.SMEM((N_EXPERTS,), jnp.int32),
                        pltpu.VMEM((2, sort_rows + GROUP_WINDOW, D // 2), _U32),
                        pltpu.SemaphoreType.DMA],
        compiler_params=pltpu.CompilerParams(
            dimension_semantics=("arbitrary",), vmem_limit_bytes=VMEM_LIMIT),
        name="moe_route",
    )(x2, g, w_r, b_r, earlier, lane_before)


def _expert_kernel(te_ref, tb_ref, tv_ref, hs_ref, w_gu_ref, w_down_ref, ys_ref):
    del te_ref, tb_ref
    i = pl.program_id(0)

    @pl.when(tv_ref[i] == 0)
    def _():
        ys_ref[...] = jnp.zeros_like(ys_ref)

    @pl.when(tv_ref[i] != 0)
    def _():
        lo, hi = _unpack_halves(hs_ref[...])
        h = jnp.concatenate([lo.astype(_BF16), hi.astype(_BF16)], axis=1)
        acc = None
        for c0, n in FF_CHUNKS:
            gate = _dot(h, w_gu_ref[:, c0:c0 + n])
            up = _dot(h, w_gu_ref[:, D_FF + c0:D_FF + c0 + n])
            act = (gate * jax.nn.sigmoid(gate) * up).astype(_BF16)
            part = _dot(act, w_down_ref[c0:c0 + n, :])
            acc = part if acc is None else acc + part
        ys_ref[...] = _pack_halves(acc)


def _experts(moe_layer, tile_expert, tile_block, tile_valid, hs, w_gu, w_down):
    n_rows, W = hs.shape
    rt = EXPERT_ROW_TILE
    D = w_gu.shape[2]
    grid_spec = pltpu.PrefetchScalarGridSpec(
        num_scalar_prefetch=3,
        grid=(tile_expert.shape[0],),
        in_specs=[pl.BlockSpec((rt, W), lambda i, te, tb, tv: (tb[i] * tv[i], 0)),
                  pl.BlockSpec((None, None, D, 2 * D_FF), lambda i, te, tb, tv: (moe_layer, te[i], 0, 0)),
                  pl.BlockSpec((None, None, D_FF, D), lambda i, te, tb, tv: (moe_layer, te[i], 0, 0))],
        out_specs=pl.BlockSpec((rt, W), lambda i, te, tb, tv: (tb[i], 0)),
    )
    return pl.pallas_call(
        _expert_kernel,
        grid_spec=grid_spec,
        out_shape=jax.ShapeDtypeStruct((n_rows + rt, W), hs.dtype),
        compiler_params=pltpu.CompilerParams(
            dimension_semantics=("arbitrary",), vmem_limit_bytes=VMEM_LIMIT),
        name="moe_experts",
    )(tile_expert, tile_block, tile_valid, hs, w_gu, w_down)


def _combine_kernel(tab_ref, x_ref, meta_ref, g_final_ref, ys_ref, o_ref, ybuf, sem, *, final_norm):
    i = pl.program_id(0)
    n = pl.num_programs(0)
    tm, D = x_ref.shape
    slot = lax.rem(i, 2)

    @pl.when(i == 0)
    def _():
        ybuf[...] = jnp.zeros_like(ybuf)

    def fetch(tile, to_slot):
        off = 0
        for e in range(N_EXPERTS):
            start = tab_ref[tile, e]
            rows = tab_ref[tile, N_EXPERTS + e]
            size = ROUTE_TILE
            while size >= SUBLANES:
                above = rows & ~(2 * size - 1)
                src = pl.multiple_of(start + above, SUBLANES)
                dst = pl.multiple_of(off + above, SUBLANES)

                @pl.when((rows & size) != 0)
                def _():
                    pltpu.make_async_copy(ys_ref.at[pl.ds(src, size)],
                                          ybuf.at[to_slot, pl.ds(dst, size)], sem.at[to_slot]).start()
                size //= 2
            off = off + rows
        return off

    def total_rows(tile):
        total = 0
        for e in range(N_EXPERTS):
            total = total + tab_ref[tile, N_EXPERTS + e]
        return total

    @pl.when(i == 0)
    def _():
        fetch(0, 0)

    @pl.when(i + 1 < n)
    def _():
        fetch(i + 1, 1 - slot)

    landed = pl.multiple_of(total_rows(i), SUBLANES)
    pltpu.make_async_copy(ys_ref.at[pl.ds(0, landed)], ybuf.at[slot, pl.ds(0, landed)],
                          sem.at[slot]).wait()

    lo, hi = _unpack_halves(ybuf[slot])
    y = jnp.concatenate([lo.astype(_BF16), hi.astype(_BF16)], axis=1)
    col = lax.broadcasted_iota(jnp.int32, (tm, ybuf.shape[1]), 1).astype(_F32)
    pick1 = jnp.where(col == meta_ref[:, META_L1:META_L1 + 1], 1.0, 0.0).astype(_BF16)
    pick2 = jnp.where(col == meta_ref[:, META_L2:META_L2 + 1], 1.0, 0.0).astype(_BF16)
    out = (x_ref[...] + meta_ref[:, META_G1:META_G1 + 1] * _dot(pick1, y)
           + meta_ref[:, META_G2:META_G2 + 1] * _dot(pick2, y))
    if final_norm:
        out = _rms(out, g_final_ref[...])
    o_ref[...] = out


def _combine(tab, x2, meta, ys, g_final, final_norm):
    T, D = x2.shape
    tm = ROUTE_TILE
    sort_rows = -(-(TOP_K * tm + N_EXPERTS * SUBLANES) // MXU_DIM) * MXU_DIM
    tile = pl.BlockSpec((tm, D), lambda i, tab: (i, 0))
    grid_spec = pltpu.PrefetchScalarGridSpec(
        num_scalar_prefetch=1,
        grid=(T // tm,),
        in_specs=[tile, pl.BlockSpec((tm, LANES), lambda i, tab: (i, 0)),
                  pl.BlockSpec((1, D), lambda i, tab: (0, 0)), pl.BlockSpec(memory_space=pl.ANY)],
        out_specs=tile,
        scratch_shapes=[pltpu.VMEM((2, sort_rows, D // 2), _U32), pltpu.SemaphoreType.DMA((2,))],
    )
    return pl.pallas_call(
        functools.partial(_combine_kernel, final_norm=final_norm),
        grid_spec=grid_spec,
        out_shape=jax.ShapeDtypeStruct(x2.shape, x2.dtype),
        compiler_params=pltpu.CompilerParams(
            dimension_semantics=("arbitrary",), vmem_limit_bytes=VMEM_LIMIT),
        name="moe_combine",
    )(tab, x2, meta, g_final, ys)


def _moe(layer, moe_layer, x2, g, w_r, b_r, w_gu, w_down, g_final, final_norm):
    T, D = x2.shape
    E = w_gu.shape[1]
    rt = EXPERT_ROW_TILE
    meta, counts, tab, hs = _route(layer, moe_layer, x2, g, w_r, b_r)

    region_tiles = hs.shape[0] // (E * rt)
    max_rows = TOP_K * T + (T // ROUTE_TILE) * E * (SUBLANES - 1)
    n_tiles = -(-max_rows // rt) + E
    counts = counts[0, :E].astype(jnp.int32)
    group_tiles = (counts + rt - 1) // rt
    tile_end = jnp.cumsum(group_tiles)
    tile_ids = jnp.arange(n_tiles, dtype=jnp.int32)
    tile_expert = jnp.minimum(
        jnp.sum((tile_ids[:, None] >= tile_end[None, :]).astype(jnp.int32), axis=1), E - 1)
    tile_valid = (tile_ids < tile_end[E - 1]).astype(jnp.int32)
    in_group = tile_ids - (tile_end - group_tiles)[tile_expert]
    tile_block = jnp.where(tile_valid != 0, tile_expert * region_tiles + in_group, E * region_tiles)

    ys = _experts(moe_layer, tile_expert, tile_block.astype(jnp.int32), tile_valid, hs, w_gu, w_down)
    return _combine(tab.reshape(tab.shape[0], tab.shape[2]), x2, meta, ys, g_final, final_norm)


def _final_norm_kernel(x_ref, g_ref, o_ref):
    o_ref[...] = _rms(x_ref[...], g_ref[...])


def _final_norm(x2, g):
    T, D = x2.shape
    tm = TOK_TILE
    tile = pl.BlockSpec((tm, D), lambda i: (i, 0))
    return pl.pallas_call(
        _final_norm_kernel,
        grid=(T // tm,),
        in_specs=[tile, pl.BlockSpec((1, D), lambda i: (0, 0))],
        out_specs=tile,
        out_shape=jax.ShapeDtypeStruct(x2.shape, x2.dtype),
        compiler_params=pltpu.CompilerParams(dimension_semantics=("parallel",)),
        name="final_norm",
    )(x2, g)


def kernel(x, mem, g_mix, w_in, w_pool, pool_scale, w_dw, b_dw, conv_ln_g, conv_ln_b, w_conv_out, w_mix_out, g_xattn, g_mem, w_xq, w_xkv, w_xo, g_ffn, w_ffn_gu, w_ffn_down, w_router, b_router, w_moe_gu, w_moe_down, g_final):
    B, S, D = x.shape
    depth = g_mix.shape[0]
    bf = lambda a: a.astype(_BF16)
    rows = lambda a: a.reshape(a.shape[0], 1, a.shape[1])

    g_mix, pool_scale, b_dw, conv_ln_g, conv_ln_b = map(rows, (g_mix, pool_scale, b_dw, conv_ln_g, conv_ln_b))
    g_xattn, g_mem, g_ffn = map(rows, (g_xattn, g_mem, g_ffn))
    w_dw = w_dw.reshape(depth, CONV_KERNEL, CONV_WIDTH)
    w_in, w_pool, w_conv_out, w_mix_out = map(bf, (w_in, w_pool, w_conv_out, w_mix_out))
    w_xq, w_xkv, w_xo = map(bf, (w_xq, w_xkv, w_xo))
    w_ffn_gu, w_ffn_down = map(bf, (w_ffn_gu, w_ffn_down))
    w_r_pad = jnp.pad(w_router, ((0, 0), (0, 0), (0, LANES - N_EXPERTS)))
    b_r_pad = rows(jnp.pad(b_router, ((0, 0), (0, LANES - N_EXPERTS)), constant_values=NEG_BIG))
    g_final = g_final.reshape(1, D)

    moe_gu_bf = moe_down_bf = None
    half_slices = (w_moe_gu.shape[1] * CAST_PARTS) // 2
    for i in range(depth):
        x = _mixer(i, x, g_mix, w_in, w_pool, pool_scale, w_dw, b_dw, conv_ln_g, conv_ln_b,
                   w_conv_out, w_mix_out)
        k, v = _kv_proj(i, mem, g_mem, w_xkv)
        last = i == depth - 1
        if i % 2 == 0:
            feeds_moe = not last
            job = lambda first: _CastJob((i + 1) // 2, first, w_moe_gu, w_moe_down, moe_gu_bf, moe_down_bf)
            if feeds_moe:
                x, moe_gu_bf, moe_down_bf = _xattn(i, x, g_xattn, w_xq, k, v, w_xo, job(0))
                x2, moe_gu_bf, moe_down_bf = _ffn(i, i // 2, x.reshape(B * S, D), g_ffn, w_ffn_gu,
                                                  w_ffn_down, job(half_slices))
            else:
                x = _xattn(i, x, g_xattn, w_xq, k, v, w_xo)
                x2 = _ffn(i, i // 2, x.reshape(B * S, D), g_ffn, w_ffn_gu, w_ffn_down)
            if last:
                x2 = _final_norm(x2, g_final)
        else:
            x = _xattn(i, x, g_xattn, w_xq, k, v, w_xo)
            x2 = _moe(i, i // 2, x.reshape(B * S, D), g_ffn, w_r_pad, b_r_pad, moe_gu_bf, moe_down_bf,
                      g_final, last)
        x = x2.reshape(B, S, D)
    return x
```

```python
import functools
from typing import NamedTuple, Optional

import jax
import jax.numpy as jnp
import numpy as np
from jax import lax
from jax.experimental import pallas as pl
from jax.experimental.pallas import tpu as pltpu

D_MODEL = 1024
POOL_WIDTH = 512
POOL_GROUPS = 4
POOL_WINDOWS = (2, 4, 8, 16)
POOL_GROUP_IN = POOL_WIDTH // POOL_GROUPS
POOL_GROUP_OUT = D_MODEL // POOL_GROUPS
CONV_WIDTH = 512
CONV_KERNEL = 31
IN_COLS = POOL_WIDTH + 2 * CONV_WIDTH + 2 * D_MODEL
N_XHEADS = 4
XHEAD_DIM = D_MODEL // N_XHEADS
D_FF = 2816
N_EXPERTS = 8
TOP_K = 2
EPS = 1e-6

COL_POOL = 0
COL_GLU_A = POOL_WIDTH
COL_GLU_B = POOL_WIDTH + CONV_WIDTH
COL_GATE_POOL = POOL_WIDTH + 2 * CONV_WIDTH
COL_GATE_CONV = COL_GATE_POOL + D_MODEL

LANES = 128
SUBLANES = 8
POOL_HALO = 16
CONV_HALO = 32
SEQ_TILE = 512
TOK_TILE = 512
MXU_DIM = 256
FF_CHUNKS = ((0, 6 * MXU_DIM), (6 * MXU_DIM, 5 * MXU_DIM))
assert sum(n for _, n in FF_CHUNKS) == D_FF
CONV_ROW_CHUNK = 64
ROUTE_TILE = 512
EXPERT_ROW_TILE = 512
GROUP_WINDOW = 256
CAST_PARTS = 8
VMEM_LIMIT = 56 * 1024 * 1024
NEG_BIG = -1e30
META_G1, META_G2, META_L1, META_L2 = range(4)

_F32 = jnp.float32
_BF16 = jnp.bfloat16
_U32 = jnp.uint32
HIGH_HALF_MASK = np.uint32(0xFFFF0000)


def _layer_spec(arr, layer):
    tail = (0,) * (arr.ndim - 1)
    return pl.BlockSpec((None,) + arr.shape[1:], lambda *_: (layer,) + tail,
                        pipeline_mode=pl.Buffered(1))


def _rms(x, g):
    return x * lax.rsqrt(jnp.mean(x * x, axis=-1, keepdims=True) + EPS) * g


def _dot(a, b):
    return jnp.dot(a, b, preferred_element_type=_F32)


def _zero_like_bits(v):
    u = lax.bitcast_convert_type(v, _U32)
    return lax.bitcast_convert_type((u >> 16) >> 16, _F32)


def _mixer_kernel(x_ref, g_ref, w_in_ref, w_pool_ref, pscale_ref, w_dw_ref, b_dw_ref,
                  ln_g_ref, ln_b_ref, w_pw_ref, w_out_ref, o_ref,
                  up_ext, cv_ext, cv_shift, cv_out, mix_ref, gate_ref):
    s = pl.program_id(1)
    ts = x_ref.shape[0]

    @pl.when(s == 0)
    def _():
        up_ext[0:POOL_HALO, :] = jnp.zeros((POOL_HALO, POOL_WIDTH), _F32)
        cv_ext[0:CONV_HALO, :] = jnp.zeros((CONV_HALO, CONV_WIDTH), _F32)

    @pl.when(s > 0)
    def _():
        up_ext[0:POOL_HALO, :] = up_ext[ts:ts + POOL_HALO, :]
        cv_ext[0:CONV_HALO, :] = cv_ext[ts:ts + CONV_HALO, :]

    x = x_ref[...]
    h = _rms(x, g_ref[...]).astype(_BF16)

    glu_a = _dot(h, w_in_ref[:, COL_GLU_A:COL_GLU_A + CONV_WIDTH])
    glu_b = _dot(h, w_in_ref[:, COL_GLU_B:COL_GLU_B + CONV_WIDTH])
    cv_ext[CONV_HALO:CONV_HALO + ts, :] = glu_a * jax.nn.sigmoid(glu_b)

    shift_rows = cv_shift.shape[1]
    for r in range(1, SUBLANES):
        cv_shift[r - 1] = cv_ext[r:r + shift_rows, :]

    tap0 = CONV_HALO - (CONV_KERNEL - 1)
    def side_task(col0, store):
        res = _dot(h, w_in_ref[:, col0:col0 + MXU_DIM])
        store(res)
        return _zero_like_bits(res[ts - 1:ts, MXU_DIM - LANES:])

    def pool_store(c):
        def store(res):
            up_ext[POOL_HALO:POOL_HALO + ts, c:c + MXU_DIM] = res
        return store

    def gate_store(c):
        def store(res):
            gate_ref[:, c:c + MXU_DIM] = jax.nn.sigmoid(res)
        return store

    side_tasks = ([(COL_POOL + c, pool_store(c)) for c in range(0, POOL_WIDTH, MXU_DIM)]
                  + [(COL_GATE_POOL + c, gate_store(c)) for c in range(0, 2 * D_MODEL, MXU_DIM)])
    unit_rows = 2 * CONV_ROW_CHUNK
    units = [(c0, q0) for c0 in range(0, CONV_WIDTH, LANES) for q0 in range(0, ts, unit_rows)]
    assert len(side_tasks) <= len(units)
    pace = None
    for u, (c0, q0) in enumerate(units):
        bias = b_dw_ref[:, c0:c0 + LANES]
        if pace is not None:
            bias = bias + pace
        pace = side_task(*side_tasks[u]) if u < len(side_tasks) else None
        for r0 in range(q0, q0 + unit_rows, CONV_ROW_CHUNK):
            acc = jnp.broadcast_to(bias, (CONV_ROW_CHUNK, LANES))
            for k in range(CONV_KERNEL):
                off = tap0 + k
                mis = off % SUBLANES
                row = r0 + off - mis
                if mis == 0:
                    win = cv_ext[row:row + CONV_ROW_CHUNK, c0:c0 + LANES]
                else:
                    win = cv_shift[mis - 1, row:row + CONV_ROW_CHUNK, c0:c0 + LANES]
                acc = acc + w_dw_ref[k:k + 1, c0:c0 + LANES] * win
            cv_out[r0:r0 + CONV_ROW_CHUNK, c0:c0 + LANES] = acc

    pos = (s * ts + 1 + lax.broadcasted_iota(jnp.int32, (ts, 1), 0)).astype(_F32)
    for g, w in enumerate(POOL_WINDOWS):
        c0 = g * POOL_GROUP_IN
        win = up_ext[:, c0:c0 + POOL_GROUP_IN]
        span = 1
        while span < w:
            win = win + pltpu.roll(win, span, 0)
            span *= 2
        cur = up_ext[POOL_HALO:POOL_HALO + ts, c0:c0 + POOL_GROUP_IN]
        pooled = win[POOL_HALO:, :] / jnp.minimum(pos, float(w)) - cur
        d0 = g * POOL_GROUP_OUT
        mix_ref[:, d0:d0 + POOL_GROUP_OUT] = (
            _dot(pooled.astype(_BF16), w_pool_ref[g]) * pscale_ref[:, d0:d0 + POOL_GROUP_OUT])

    u = cv_out[...]
    mu = jnp.mean(u, axis=-1, keepdims=True)
    uc = u - mu
    var = jnp.mean(uc * uc, axis=-1, keepdims=True)
    un = uc * lax.rsqrt(var + EPS) * ln_g_ref[...] + ln_b_ref[...]
    un = un * jax.nn.sigmoid(un)
    y_conv = _dot(un.astype(_BF16), w_pw_ref[...])
    mix = gate_ref[:, 0:D_MODEL] * mix_ref[...] + gate_ref[:, D_MODEL:] * y_conv

    o_ref[...] = x + _dot(mix.astype(_BF16), w_out_ref[...])


def _mixer(layer, x, g, w_in, w_pool, pscale, w_dw, b_dw, ln_g, ln_b, w_pw, w_out):
    B, S, D = x.shape
    ts = SEQ_TILE
    shift_rows = ts + CONV_HALO - SUBLANES
    tile = pl.BlockSpec((None, ts, D), lambda b, s: (b, s, 0))
    params = (g, w_in, w_pool, pscale, w_dw, b_dw, ln_g, ln_b, w_pw, w_out)
    return pl.pallas_call(
        _mixer_kernel,
        grid=(B, S // ts),
        in_specs=[tile] + [_layer_spec(p, layer) for p in params],
        out_specs=tile,
        out_shape=jax.ShapeDtypeStruct(x.shape, x.dtype),
        scratch_shapes=[pltpu.VMEM((POOL_HALO + ts, POOL_WIDTH), _F32),
                        pltpu.VMEM((CONV_HALO + ts, CONV_WIDTH), _F32),
                        pltpu.VMEM((SUBLANES - 1, shift_rows, CONV_WIDTH), _F32),
                        pltpu.VMEM((ts, CONV_WIDTH), _F32),
                        pltpu.VMEM((ts, D), _F32),
                        pltpu.VMEM((ts, 2 * D), _F32)],
        compiler_params=pltpu.CompilerParams(
            dimension_semantics=("parallel", "arbitrary"), vmem_limit_bytes=VMEM_LIMIT),
        name="mixer",
    )(x, *params)


def _kv_kernel(m_ref, g_ref, w_ref, k_ref, v_ref):
    m = _rms(m_ref[...], g_ref[...]).astype(_BF16)
    k_ref[...] = (_dot(m, w_ref[:, 0:D_MODEL]) * (XHEAD_DIM ** -0.5)).astype(_BF16)
    v_ref[...] = _dot(m, w_ref[:, D_MODEL:2 * D_MODEL]).astype(_BF16)


def _kv_proj(mem, g, w_kv):
    B, M, D = mem.shape
    L = w_kv.shape[0]
    out_blk = pl.BlockSpec((None, None, M, D), lambda l, b: (l, b, 0, 0))
    return pl.pallas_call(
        _kv_kernel,
        grid=(L, B),
        in_specs=[pl.BlockSpec((None, M, D), lambda l, b: (b, 0, 0)),
                  pl.BlockSpec((None,) + g.shape[1:], lambda l, b: (l, 0, 0)),
                  pl.BlockSpec((None,) + w_kv.shape[1:], lambda l, b: (l, 0, 0))],
        out_specs=[out_blk, out_blk],
        out_shape=[jax.ShapeDtypeStruct((L,) + mem.shape, _BF16)] * 2,
        compiler_params=pltpu.CompilerParams(
            dimension_semantics=("parallel", "parallel"), vmem_limit_bytes=VMEM_LIMIT),
        name="kv_proj",
    )(mem, g, w_kv)


class _CastJob(NamedTuple):
    moe_layer: int
    first_slice: int
    w_gu: jax.Array
    w_down: jax.Array
    gu_bf: Optional[jax.Array]
    down_bf: Optional[jax.Array]


def _cast_slices(job):
    return job.w_gu.shape[1] * CAST_PARTS


def _cast_plumbing(job, step_of, n_in, n_out):
    _, _, d, f2 = job.w_gu.shape
    f = job.w_down.shape[2]

    def part(*grid_idx):
        q = jnp.minimum(job.first_slice + step_of(*grid_idx), _cast_slices(job) - 1)
        return q // CAST_PARTS, q % CAST_PARTS

    def src_idx(*grid_idx):
        return (job.moe_layer,) + part(*grid_idx) + (0,)

    def dst_idx(*grid_idx):
        return part(*grid_idx) + (0,)

    in_specs = [pl.BlockSpec((None, None, d // CAST_PARTS, f2), src_idx),
                pl.BlockSpec((None, None, f // CAST_PARTS, d), src_idx)]
    out_specs = [pl.BlockSpec((None, d // CAST_PARTS, f2), dst_idx),
                 pl.BlockSpec((None, f // CAST_PARTS, d), dst_idx)]
    args, aliases = [job.w_gu, job.w_down], {}
    if job.gu_bf is not None:
        in_specs += [pl.BlockSpec(memory_space=pl.ANY)] * 2
        args += [job.gu_bf, job.down_bf]
        aliases = {n_in + 2: n_out, n_in + 3: n_out + 1}
    out_shapes = [jax.ShapeDtypeStruct(job.w_gu.shape[1:], _BF16),
                  jax.ShapeDtypeStruct(job.w_down.shape[1:], _BF16)]
    return in_specs, args, out_specs, out_shapes, aliases


def _split_cast_refs(rest, n_out, cast_fill):
    n_cast_in = 4 if cast_fill else 2
    gu_in, down_in = rest[0], rest[1]
    host_out = rest[n_cast_in:n_cast_in + n_out]
    gu_out, down_out = rest[n_cast_in + n_out], rest[n_cast_in + n_out + 1]

    def cast():
        gu_out[...] = gu_in[...].astype(_BF16)
        down_out[...] = down_in[...].astype(_BF16)

    return tuple(host_out) + tuple(rest[n_cast_in + n_out + 2:]), cast


def _xattn_kernel(x_ref, g_ref, wq_ref, k_ref, v_ref, wo_ref, *rest, cast):
    if cast is not None:
        rest, cast_slice = _split_cast_refs(rest, 1, cast)
        cast_slice()
    o_ref, att_ref = rest
    x = x_ref[...]
    h = _rms(x, g_ref[...]).astype(_BF16)
    q = _dot(h, wq_ref[...]).astype(_BF16)
    for hd in range(N_XHEADS):
        c0 = hd * XHEAD_DIM
        sc = lax.dot_general(q[:, c0:c0 + XHEAD_DIM], k_ref[:, c0:c0 + XHEAD_DIM],
                             (((1,), (1,)), ((), ())), preferred_element_type=_F32)
        p = jnp.exp(sc - jnp.max(sc, axis=-1, keepdims=True))
        denom = jnp.sum(p, axis=-1, keepdims=True)
        att = _dot(p.astype(_BF16), v_ref[:, c0:c0 + XHEAD_DIM]) / denom
        att_ref[:, c0:c0 + XHEAD_DIM] = att.astype(_BF16)
    o_ref[...] = x + _dot(att_ref[...], wo_ref[...])


def _xattn(layer, x, g, w_q, k, v, w_o, job=None):
    B, S, D = x.shape
    M = k.shape[2]
    ts = SEQ_TILE
    n_seq = S // ts
    tile = pl.BlockSpec((None, ts, D), lambda b, s: (b, s, 0))
    kv_blk = pl.BlockSpec((None, None, M, D), lambda b, s: (layer, b, 0, 0))
    in_specs = [tile, _layer_spec(g, layer), _layer_spec(w_q, layer), kv_blk, kv_blk,
                _layer_spec(w_o, layer)]
    args = [x, g, w_q, k, v, w_o]
    out_specs = [tile]
    out_shape = [jax.ShapeDtypeStruct(x.shape, x.dtype)]
    aliases, cast = {}, None
    if job is not None:
        assert B * n_seq == (job.w_gu.shape[1] * CAST_PARTS) // 2
        extra = _cast_plumbing(job, lambda b, s: b * n_seq + s, len(in_specs), len(out_specs))
        in_specs += extra[0]
        args += extra[1]
        out_specs += extra[2]
        out_shape += extra[3]
        aliases, cast = extra[4], job.gu_bf is not None
    out = pl.pallas_call(
        functools.partial(_xattn_kernel, cast=cast),
        grid=(B, n_seq),
        in_specs=in_specs,
        out_specs=out_specs,
        out_shape=out_shape,
        scratch_shapes=[pltpu.VMEM((ts, D), _BF16)],
        input_output_aliases=aliases,
        compiler_params=pltpu.CompilerParams(
            dimension_semantics=("parallel", "parallel"), vmem_limit_bytes=VMEM_LIMIT),
        name="xattn",
    )(*args)
    return out[0] if job is None else tuple(out)


def _ffn_kernel(x_ref, g_ref, w_gu_ref, w_down_ref, *rest, cast):
    if cast is not None:
        rest, cast_slice = _split_cast_refs(rest, 1, cast)
        cast_slice()
    (o_ref,) = rest
    x = x_ref[...]
    h = _rms(x, g_ref[...]).astype(_BF16)
    acc = x
    for c0, n in FF_CHUNKS:
        gate = _dot(h, w_gu_ref[:, c0:c0 + n])
        up = _dot(h, w_gu_ref[:, D_FF + c0:D_FF + c0 + n])
        act = (gate * jax.nn.sigmoid(gate) * up).astype(_BF16)
        acc = acc + _dot(act, w_down_ref[c0:c0 + n, :])
    o_ref[...] = acc


def _ffn(layer, ffn_layer, x2, g, w_gu, w_down, job=None):
    T, D = x2.shape
    tm = TOK_TILE
    tile = pl.BlockSpec((tm, D), lambda i: (i, 0))
    in_specs = [tile, _layer_spec(g, layer), _layer_spec(w_gu, ffn_layer),
                _layer_spec(w_down, ffn_layer)]
    args = [x2, g, w_gu, w_down]
    out_specs = [tile]
    out_shape = [jax.ShapeDtypeStruct(x2.shape, x2.dtype)]
    aliases, cast = {}, None
    if job is not None:
        assert T // tm == (job.w_gu.shape[1] * CAST_PARTS) // 2
        extra = _cast_plumbing(job, lambda i: i, len(in_specs), len(out_specs))
        in_specs += extra[0]
        args += extra[1]
        out_specs += extra[2]
        out_shape += extra[3]
        aliases, cast = extra[4], job.gu_bf is not None
    out = pl.pallas_call(
        functools.partial(_ffn_kernel, cast=cast),
        grid=(T // tm,),
        in_specs=in_specs,
        out_specs=out_specs,
        out_shape=out_shape,
        input_output_aliases=aliases,
        compiler_params=pltpu.CompilerParams(
            dimension_semantics=("parallel",), vmem_limit_bytes=VMEM_LIMIT),
        name="ffn",
    )(*args)
    return out[0] if job is None else tuple(out)


def _pack_halves(y):
    half = y.shape[1] // 2
    lo = lax.bitcast_convert_type(y[:, :half].astype(_BF16).astype(_F32), _U32)
    hi = lax.bitcast_convert_type(y[:, half:].astype(_BF16).astype(_F32), _U32)
    return (lo >> 16) | (hi & HIGH_HALF_MASK)


def _unpack_halves(p):
    lo = lax.bitcast_convert_type(p << 16, _F32)
    hi = lax.bitcast_convert_type(p & HIGH_HALF_MASK, _F32)
    return lo, hi


def _route_kernel(x_ref, g_ref, w_r_ref, b_r_ref, earlier_ref, lane_before_ref,
                  meta_ref, cnt_ref, tab_ref, hs_ref,
                  carry_ref, placed_smem, second_smem, sorted_buf, sem,
                  *, region_rows, sort_rows):
    i = pl.program_id(0)
    tm = x_ref.shape[0]

    @pl.when(i == 0)
    def _():
        carry_ref[...] = jnp.zeros_like(carry_ref)
        for e in range(N_EXPERTS):
            placed_smem[e] = 0
        sorted_buf[:, sort_rows:, :] = jnp.zeros(
            (sorted_buf.shape[0], sorted_buf.shape[1] - sort_rows, sorted_buf.shape[2]), _U32)

    hf = _rms(x_ref[...], g_ref[...])

    lane = lax.broadcasted_iota(jnp.int32, (tm, LANES), 1)
    h_hi = hf.astype(_BF16)
    h_lo = (hf - h_hi.astype(_F32)).astype(_BF16)
    w_r = w_r_ref[...]
    w_hi = w_r.astype(_BF16)
    w_lo = (w_r - w_hi.astype(_F32)).astype(_BF16)
    hi_terms = _dot(h_hi, jnp.concatenate([w_hi, w_lo], axis=1))
    logits = hi_terms[:, :LANES] + hi_terms[:, LANES:] + _dot(h_lo, w_hi) + b_r_ref[...]
    v1 = jnp.max(logits, axis=-1, keepdims=True)
    i1 = jnp.min(jnp.where(logits == v1, lane, LANES), axis=-1, keepdims=True)
    rest = jnp.where(lane == i1, NEG_BIG, logits)
    v2 = jnp.max(rest, axis=-1, keepdims=True)
    i2 = jnp.min(jnp.where(rest == v2, lane, LANES), axis=-1, keepdims=True)
    z = jnp.exp(v2 - v1)
    g1 = 1.0 / (1.0 + z)
    g2 = z / (1.0 + z)

    sel1 = lane == i1
    sel2 = lane == i2
    chosen = jnp.where(sel1 | sel2, 1.0, 0.0)
    before = _dot(earlier_ref[...], chosen.astype(_BF16))
    cnt = jnp.sum(chosen, axis=0, keepdims=True)
    cnt_pad = jnp.ceil(cnt * (1.0 / SUBLANES)) * SUBLANES
    group_off = _dot(jnp.broadcast_to(cnt_pad, (SUBLANES, LANES)).astype(_BF16),
                     lane_before_ref[...])[0:1, :]
    local = group_off + before
    loc1 = jnp.sum(jnp.where(sel1, local, 0.0), axis=-1, keepdims=True)
    loc2 = jnp.sum(jnp.where(sel2, local, 0.0), axis=-1, keepdims=True)

    slot = lax.rem(i, 2)
    loc_t = jnp.where(lane == 0, loc1, jnp.where(lane == 1, loc2, -1.0)).T
    row = lax.broadcasted_iota(jnp.int32, (sort_rows, tm), 0).astype(_F32)
    perm = jnp.where((row == loc_t[0:1, :]) | (row == loc_t[1:2, :]), 1.0, 0.0).astype(_BF16)
    sorted_buf[slot, 0:sort_rows, :] = _pack_halves(_dot(perm, h_hi))

    def window(src, dst):
        return pltpu.make_async_copy(sorted_buf.at[slot, pl.ds(src, GROUP_WINDOW)],
                                     hs_ref.at[pl.ds(dst, GROUP_WINDOW)], sem)

    def wait_windows(second_flags):
        for e in range(N_EXPERTS):
            window(0, 0).wait()

            @pl.when(second_flags(e) != 0)
            def _():
                window(0, 0).wait()

    @pl.when(i > 0)
    def _():
        wait_windows(lambda e: second_smem[e])

    off_i = group_off.astype(jnp.int32)
    cnt_i = cnt_pad.astype(jnp.int32)
    for e in range(N_EXPERTS):
        src = pl.multiple_of(off_i[0, e], SUBLANES)
        start = pl.multiple_of(e * region_rows + placed_smem[e], SUBLANES)
        rows = cnt_i[0, e]
        second = (rows > GROUP_WINDOW).astype(jnp.int32)
        window(src, start).start()

        @pl.when(second != 0)
        def _():
            window(src + GROUP_WINDOW, start + GROUP_WINDOW).start()

        second_smem[e] = second
        placed_smem[e] = placed_smem[e] + rows
        tab_ref[0, e] = start
        tab_ref[0, N_EXPERTS + e] = rows
    carry_ref[...] += cnt_pad
    cnt_ref[...] = carry_ref[...]

    @pl.when(i == pl.num_programs(0) - 1)
    def _():
        wait_windows(lambda e: second_smem[e])

        def tail(e, part):
            dst = pl.multiple_of(e * region_rows + placed_smem[e], SUBLANES) + part * GROUP_WINDOW
            return window(sort_rows, dst)

        for e in range(N_EXPERTS):
            for part in range(EXPERT_ROW_TILE // GROUP_WINDOW):
                tail(e, part).start()
        for e in range(N_EXPERTS):
            for part in range(EXPERT_ROW_TILE // GROUP_WINDOW):
                tail(e, part).wait()

    meta = jnp.where(lane == META_G1, g1, 0.0)
    meta = jnp.where(lane == META_G2, g2, meta)
    meta = jnp.where(lane == META_L1, loc1, meta)
    meta = jnp.where(lane == META_L2, loc2, meta)
    meta_ref[...] = meta


def _region_rows(n_tokens):
    pad = (n_tokens // ROUTE_TILE) * (SUBLANES - 1) + 2 * EXPERT_ROW_TILE
    return -(-(n_tokens + pad) // EXPERT_ROW_TILE) * EXPERT_ROW_TILE


def _route(layer, moe_layer, x2, g, w_r, b_r):
    T, D = x2.shape
    tm = ROUTE_TILE
    region_rows = _region_rows(T)
    sort_rows = TOP_K * tm + N_EXPERTS * SUBLANES
    earlier = jnp.asarray(np.tril(np.ones((tm, tm), np.float32), -1), _BF16)
    lane_before = jnp.asarray(np.triu(np.ones((LANES, LANES), np.float32), 1), _BF16)
    const = lambda n: pl.BlockSpec((n, n), lambda i: (0, 0), pipeline_mode=pl.Buffered(1))
    return pl.pallas_call(
        functools.partial(_route_kernel, region_rows=region_rows, sort_rows=sort_rows),
        grid=(T // tm,),
        in_specs=[pl.BlockSpec((tm, D), lambda i: (i, 0)), _layer_spec(g, layer),
                  _layer_spec(w_r, moe_layer), _layer_spec(b_r, moe_layer),
                  const(tm), const(LANES)],
        out_specs=[pl.BlockSpec((tm, LANES), lambda i: (i, 0)),
                   pl.BlockSpec((1, LANES), lambda i: (0, 0)),
                   pl.BlockSpec((None, 1, 2 * N_EXPERTS), lambda i: (i, 0, 0),
                                memory_space=pltpu.SMEM),
                   pl.BlockSpec(memory_space=pl.ANY)],
        out_shape=[jax.ShapeDtypeStruct((T, LANES), _F32),
                   jax.ShapeDtypeStruct((1, LANES), _F32),
                   jax.ShapeDtypeStruct((T // tm, 1, 2 * N_EXPERTS), jnp.int32),
                   jax.ShapeDtypeStruct((N_EXPERTS * region_rows, D // 2), _U32)],
        scratch_shapes=[pltpu.VMEM((1, LANES), _F32),
                        pltpu.SMEM((N_EXPERTS,), jnp.int32),
                        pltpu.SMEM((N_EXPERTS,), jnp.int32),
                        pltpu.VMEM((2, sort_rows + GROUP_WINDOW, D // 2), _U32),
                        pltpu.SemaphoreType.DMA],
        compiler_params=pltpu.CompilerParams(
            dimension_semantics=("arbitrary",), vmem_limit_bytes=VMEM_LIMIT),
        name="moe_route",
    )(x2, g, w_r, b_r, earlier, lane_before)


def _expert_kernel(te_ref, tb_ref, tv_ref, hs_ref, w_gu_ref, w_down_ref, ys_ref):
    del te_ref, tb_ref
    i = pl.program_id(0)

    rt = ys_ref.shape[0]
    half = rt // 2
    rows_used = tv_ref[i]

    def swiglu_rows(n_rows):
        lo, hi = _unpack_halves(hs_ref[0:n_rows, :])
        h = jnp.concatenate([lo.astype(_BF16), hi.astype(_BF16)], axis=1)
        acc = None
        for c0, n in FF_CHUNKS:
            gate = _dot(h, w_gu_ref[:, c0:c0 + n])
            up = _dot(h, w_gu_ref[:, D_FF + c0:D_FF + c0 + n])
            act = (gate * jax.nn.sigmoid(gate) * up).astype(_BF16)
            part = _dot(act, w_down_ref[c0:c0 + n, :])
            acc = part if acc is None else acc + part
        ys_ref[0:n_rows, :] = _pack_halves(acc)

    @pl.when(rows_used == 0)
    def _():
        ys_ref[...] = jnp.zeros_like(ys_ref)

    @pl.when((rows_used > 0) & (rows_used <= half))
    def _():
        swiglu_rows(half)
        ys_ref[half:, :] = jnp.zeros((rt - half, ys_ref.shape[1]), ys_ref.dtype)

    @pl.when(rows_used > half)
    def _():
        swiglu_rows(rt)


def _experts(tile_expert, tile_block, tile_rows, hs, w_gu, w_down):
    n_rows, W = hs.shape
    rt = EXPERT_ROW_TILE
    D = w_gu.shape[1]
    grid_spec = pltpu.PrefetchScalarGridSpec(
        num_scalar_prefetch=3,
        grid=(tile_expert.shape[0],),
        in_specs=[pl.BlockSpec((rt, W), lambda i, te, tb, tv: (jnp.where(tv[i] > 0, tb[i], 0), 0)),
                  pl.BlockSpec((None, D, 2 * D_FF), lambda i, te, tb, tv: (te[i], 0, 0)),
                  pl.BlockSpec((None, D_FF, D), lambda i, te, tb, tv: (te[i], 0, 0))],
        out_specs=pl.BlockSpec((rt, W), lambda i, te, tb, tv: (tb[i], 0)),
    )
    return pl.pallas_call(
        _expert_kernel,
        grid_spec=grid_spec,
        out_shape=jax.ShapeDtypeStruct((n_rows + rt, W), hs.dtype),
        compiler_params=pltpu.CompilerParams(
            dimension_semantics=("arbitrary",), vmem_limit_bytes=VMEM_LIMIT),
        name="moe_experts",
    )(tile_expert, tile_block, tile_rows, hs, w_gu, w_down)


def _combine_kernel(tab_ref, x_ref, meta_ref, g_final_ref, ys_ref, o_ref, ybuf, sem, *, final_norm):
    i = pl.program_id(0)
    n = pl.num_programs(0)
    tm, D = x_ref.shape
    slot = lax.rem(i, 2)

    @pl.when(i == 0)
    def _():
        ybuf[...] = jnp.zeros_like(ybuf)

    def fetch(tile, to_slot):
        off = 0
        for e in range(N_EXPERTS):
            start = tab_ref[tile, e]
            rows = tab_ref[tile, N_EXPERTS + e]
            size = ROUTE_TILE
            while size >= SUBLANES:
                above = rows & ~(2 * size - 1)
                src = pl.multiple_of(start + above, SUBLANES)
                dst = pl.multiple_of(off + above, SUBLANES)

                @pl.when((rows & size) != 0)
                def _():
                    pltpu.make_async_copy(ys_ref.at[pl.ds(src, size)],
                                          ybuf.at[to_slot, pl.ds(dst, size)], sem.at[to_slot]).start()
                size //= 2
            off = off + rows
        return off

    def total_rows(tile):
        total = 0
        for e in range(N_EXPERTS):
            total = total + tab_ref[tile, N_EXPERTS + e]
        return total

    @pl.when(i == 0)
    def _():
        fetch(0, 0)

    @pl.when(i + 1 < n)
    def _():
        fetch(i + 1, 1 - slot)

    landed = pl.multiple_of(total_rows(i), SUBLANES)
    pltpu.make_async_copy(ys_ref.at[pl.ds(0, landed)], ybuf.at[slot, pl.ds(0, landed)],
                          sem.at[slot]).wait()

    lo, hi = _unpack_halves(ybuf[slot])
    y = jnp.concatenate([lo.astype(_BF16), hi.astype(_BF16)], axis=1)
    col = lax.broadcasted_iota(jnp.int32, (tm, ybuf.shape[1]), 1).astype(_F32)
    pick1 = jnp.where(col == meta_ref[:, META_L1:META_L1 + 1], 1.0, 0.0).astype(_BF16)
    pick2 = jnp.where(col == meta_ref[:, META_L2:META_L2 + 1], 1.0, 0.0).astype(_BF16)
    out = (x_ref[...] + meta_ref[:, META_G1:META_G1 + 1] * _dot(pick1, y)
           + meta_ref[:, META_G2:META_G2 + 1] * _dot(pick2, y))
    if final_norm:
        out = _rms(out, g_final_ref[...])
    o_ref[...] = out


def _combine(tab, x2, meta, ys, g_final, final_norm):
    T, D = x2.shape
    tm = ROUTE_TILE
    sort_rows = -(-(TOP_K * tm + N_EXPERTS * SUBLANES) // MXU_DIM) * MXU_DIM
    tile = pl.BlockSpec((tm, D), lambda i, tab: (i, 0))
    grid_spec = pltpu.PrefetchScalarGridSpec(
        num_scalar_prefetch=1,
        grid=(T // tm,),
        in_specs=[tile, pl.BlockSpec((tm, LANES), lambda i, tab: (i, 0)),
                  pl.BlockSpec((1, D), lambda i, tab: (0, 0)), pl.BlockSpec(memory_space=pl.ANY)],
        out_specs=tile,
        scratch_shapes=[pltpu.VMEM((2, sort_rows, D // 2), _U32), pltpu.SemaphoreType.DMA((2,))],
    )
    return pl.pallas_call(
        functools.partial(_combine_kernel, final_norm=final_norm),
        grid_spec=grid_spec,
        out_shape=jax.ShapeDtypeStruct(x2.shape, x2.dtype),
        compiler_params=pltpu.CompilerParams(
            dimension_semantics=("arbitrary",), vmem_limit_bytes=VMEM_LIMIT),
        name="moe_combine",
    )(tab, x2, meta, g_final, ys)


def _moe(layer, moe_layer, x2, g, w_r, b_r, w_gu, w_down, g_final, final_norm):
    T, D = x2.shape
    E = w_gu.shape[0]
    rt = EXPERT_ROW_TILE
    meta, counts, tab, hs = _route(layer, moe_layer, x2, g, w_r, b_r)

    region_tiles = hs.shape[0] // (E * rt)
    max_rows = TOP_K * T + (T // ROUTE_TILE) * E * (SUBLANES - 1)
    n_tiles = -(-max_rows // rt) + E
    counts = counts[0, :E].astype(jnp.int32)
    group_tiles = (counts + rt - 1) // rt
    tile_end = jnp.cumsum(group_tiles)
    tile_ids = jnp.arange(n_tiles, dtype=jnp.int32)
    tile_expert = jnp.minimum(
        jnp.sum((tile_ids[:, None] >= tile_end[None, :]).astype(jnp.int32), axis=1), E - 1)
    tile_valid = (tile_ids < tile_end[E - 1]).astype(jnp.int32)
    in_group = tile_ids - (tile_end - group_tiles)[tile_expert]
    tile_block = jnp.where(tile_valid != 0, tile_expert * region_tiles + in_group, E * region_tiles)

    tile_rows = jnp.clip(counts[tile_expert] - in_group * rt, 0, rt) * tile_valid
    ys = _experts(tile_expert, tile_block.astype(jnp.int32), tile_rows.astype(jnp.int32), hs, w_gu, w_down)
    return _combine(tab.reshape(tab.shape[0], tab.shape[2]), x2, meta, ys, g_final, final_norm)


def _final_norm_kernel(x_ref, g_ref, o_ref):
    o_ref[...] = _rms(x_ref[...], g_ref[...])


def _final_norm(x2, g):
    T, D = x2.shape
    tm = TOK_TILE
    tile = pl.BlockSpec((tm, D), lambda i: (i, 0))
    return pl.pallas_call(
        _final_norm_kernel,
        grid=(T // tm,),
        in_specs=[tile, pl.BlockSpec((1, D), lambda i: (0, 0))],
        out_specs=tile,
        out_shape=jax.ShapeDtypeStruct(x2.shape, x2.dtype),
        compiler_params=pltpu.CompilerParams(dimension_semantics=("parallel",)),
        name="final_norm",
    )(x2, g)


def kernel(x, mem, g_mix, w_in, w_pool, pool_scale, w_dw, b_dw, conv_ln_g, conv_ln_b, w_conv_out, w_mix_out, g_xattn, g_mem, w_xq, w_xkv, w_xo, g_ffn, w_ffn_gu, w_ffn_down, w_router, b_router, w_moe_gu, w_moe_down, g_final):
    B, S, D = x.shape
    depth = g_mix.shape[0]
    bf = lambda a: a.astype(_BF16)
    rows = lambda a: a.reshape(a.shape[0], 1, a.shape[1])

    g_mix, pool_scale, b_dw, conv_ln_g, conv_ln_b = map(rows, (g_mix, pool_scale, b_dw, conv_ln_g, conv_ln_b))
    g_xattn, g_mem, g_ffn = map(rows, (g_xattn, g_mem, g_ffn))
    w_dw = w_dw.reshape(depth, CONV_KERNEL, CONV_WIDTH)
    w_in, w_pool, w_conv_out, w_mix_out = map(bf, (w_in, w_pool, w_conv_out, w_mix_out))
    w_xq, w_xkv, w_xo = map(bf, (w_xq, w_xkv, w_xo))
    w_ffn_gu, w_ffn_down = map(bf, (w_ffn_gu, w_ffn_down))
    w_r_pad = jnp.pad(w_router, ((0, 0), (0, 0), (0, LANES - N_EXPERTS)))
    b_r_pad = rows(jnp.pad(b_router, ((0, 0), (0, LANES - N_EXPERTS)), constant_values=NEG_BIG))
    g_final = g_final.reshape(1, D)

    half_slices = (w_moe_gu.shape[1] * CAST_PARTS) // 2
    k_all, v_all = _kv_proj(mem, g_mem, w_xkv)
    for i in range(depth):
        x = _mixer(i, x, g_mix, w_in, w_pool, pool_scale, w_dw, b_dw, conv_ln_g, conv_ln_b,
                   w_conv_out, w_mix_out)
        k, v = k_all, v_all
        last = i == depth - 1
        if i % 2 == 0:
            feeds_moe = not last
            moe_gu_bf = moe_down_bf = None
            job = lambda first: _CastJob((i + 1) // 2, first, w_moe_gu, w_moe_down, moe_gu_bf, moe_down_bf)
            if feeds_moe:
                x, moe_gu_bf, moe_down_bf = _xattn(i, x, g_xattn, w_xq, k, v, w_xo, job(0))
                x2, moe_gu_bf, moe_down_bf = _ffn(i, i // 2, x.reshape(B * S, D), g_ffn, w_ffn_gu,
                                                  w_ffn_down, job(half_slices))
            else:
                x = _xattn(i, x, g_xattn, w_xq, k, v, w_xo)
                x2 = _ffn(i, i // 2, x.reshape(B * S, D), g_ffn, w_ffn_gu, w_ffn_down)
            if last:
                x2 = _final_norm(x2, g_final)
        else:
            x = _xattn(i, x, g_xattn, w_xq, k, v, w_xo)
            x2 = _moe(i, i // 2, x.reshape(B * S, D), g_ffn, w_r_pad, b_r_pad, moe_gu_bf, moe_down_bf,
                      g_final, last)
        x = x2.reshape(B, S, D)
    return x
```

```python
import functools
from typing import NamedTuple, Optional

import jax
import jax.numpy as jnp
import numpy as np
from jax import lax
from jax.experimental import pallas as pl
from jax.experimental.pallas import tpu as pltpu

D_MODEL = 1024
POOL_WIDTH = 512
POOL_GROUPS = 4
POOL_WINDOWS = (2, 4, 8, 16)
POOL_GROUP_IN = POOL_WIDTH // POOL_GROUPS
POOL_GROUP_OUT = D_MODEL // POOL_GROUPS
CONV_WIDTH = 512
CONV_KERNEL = 31
IN_COLS = POOL_WIDTH + 2 * CONV_WIDTH + 2 * D_MODEL
N_XHEADS = 4
XHEAD_DIM = D_MODEL // N_XHEADS
D_FF = 2816
N_EXPERTS = 8
TOP_K = 2
EPS = 1e-6

COL_POOL = 0
COL_GLU_A = POOL_WIDTH
COL_GLU_B = POOL_WIDTH + CONV_WIDTH
COL_GATE_POOL = POOL_WIDTH + 2 * CONV_WIDTH
COL_GATE_CONV = COL_GATE_POOL + D_MODEL

LANES = 128
SUBLANES = 8
POOL_HALO = 16
CONV_HALO = 32
SEQ_TILE = 512
TOK_TILE = 512
MXU_DIM = 256
FF_CHUNKS = ((0, 6 * MXU_DIM), (6 * MXU_DIM, 5 * MXU_DIM))
assert sum(n for _, n in FF_CHUNKS) == D_FF
CONV_ROW_CHUNK = 64
ROUTE_TILE = 512
EXPERT_ROW_TILE = 512
GROUP_WINDOW = 256
CAST_PARTS = 8
VMEM_LIMIT = 56 * 1024 * 1024
NEG_BIG = -1e30
META_G1, META_G2, META_L1, META_L2 = range(4)

_F32 = jnp.float32
_BF16 = jnp.bfloat16
_U32 = jnp.uint32
HIGH_HALF_MASK = np.uint32(0xFFFF0000)


def _layer_spec(arr, layer):
    tail = (0,) * (arr.ndim - 1)
    return pl.BlockSpec((None,) + arr.shape[1:], lambda *_: (layer,) + tail,
                        pipeline_mode=pl.Buffered(1))


def _rms(x, g):
    return x * lax.rsqrt(jnp.mean(x * x, axis=-1, keepdims=True) + EPS) * g


def _dot(a, b):
    return jnp.dot(a, b, preferred_element_type=_F32)


def _zero_like_bits(v):
    u = lax.bitcast_convert_type(v, _U32)
    return lax.bitcast_convert_type((u >> 16) >> 16, _F32)


def _mixer_kernel(x_ref, g_ref, w_in_ref, w_pool_ref, pscale_ref, w_dw_ref, b_dw_ref,
                  ln_g_ref, ln_b_ref, w_pw_ref, w_out_ref, *rest, cast):
    if cast is not None:
        rest, cast_slice = _split_cast_refs(rest, 1, cast)
        cast_slice()
    o_ref, up_ext, cv_ext, cv_shift, cv_out, mix_ref, gate_ref = rest
    s = pl.program_id(1)
    ts = x_ref.shape[0]

    @pl.when(s == 0)
    def _():
        up_ext[0:POOL_HALO, :] = jnp.zeros((POOL_HALO, POOL_WIDTH), _F32)
        cv_ext[0:CONV_HALO, :] = jnp.zeros((CONV_HALO, CONV_WIDTH), _F32)

    @pl.when(s > 0)
    def _():
        up_ext[0:POOL_HALO, :] = up_ext[ts:ts + POOL_HALO, :]
        cv_ext[0:CONV_HALO, :] = cv_ext[ts:ts + CONV_HALO, :]

    x = x_ref[...]
    h = _rms(x, g_ref[...]).astype(_BF16)

    glu_a = _dot(h, w_in_ref[:, COL_GLU_A:COL_GLU_A + CONV_WIDTH])
    glu_b = _dot(h, w_in_ref[:, COL_GLU_B:COL_GLU_B + CONV_WIDTH])
    cv_ext[CONV_HALO:CONV_HALO + ts, :] = glu_a * jax.nn.sigmoid(glu_b)

    shift_rows = cv_shift.shape[1]
    for r in range(1, SUBLANES):
        cv_shift[r - 1] = cv_ext[r:r + shift_rows, :]

    tap0 = CONV_HALO - (CONV_KERNEL - 1)
    def side_task(col0, store):
        res = _dot(h, w_in_ref[:, col0:col0 + MXU_DIM])
        store(res)
        return _zero_like_bits(res[ts - 1:ts, MXU_DIM - LANES:])

    def pool_store(c):
        def store(res):
            up_ext[POOL_HALO:POOL_HALO + ts, c:c + MXU_DIM] = res
        return store

    def gate_store(c):
        def store(res):
            gate_ref[:, c:c + MXU_DIM] = jax.nn.sigmoid(res)
        return store

    side_tasks = ([(COL_POOL + c, pool_store(c)) for c in range(0, POOL_WIDTH, MXU_DIM)]
                  + [(COL_GATE_POOL + c, gate_store(c)) for c in range(0, 2 * D_MODEL, MXU_DIM)])
    unit_rows = 2 * CONV_ROW_CHUNK
    units = [(c0, q0) for c0 in range(0, CONV_WIDTH, LANES) for q0 in range(0, ts, unit_rows)]
    assert len(side_tasks) <= len(units)
    pace = None
    for u, (c0, q0) in enumerate(units):
        bias = b_dw_ref[:, c0:c0 + LANES]
        if pace is not None:
            bias = bias + pace
        pace = side_task(*side_tasks[u]) if u < len(side_tasks) else None
        for r0 in range(q0, q0 + unit_rows, CONV_ROW_CHUNK):
            acc = jnp.broadcast_to(bias, (CONV_ROW_CHUNK, LANES))
            for k in range(CONV_KERNEL):
                off = tap0 + k
                mis = off % SUBLANES
                row = r0 + off - mis
                if mis == 0:
                    win = cv_ext[row:row + CONV_ROW_CHUNK, c0:c0 + LANES]
                else:
                    win = cv_shift[mis - 1, row:row + CONV_ROW_CHUNK, c0:c0 + LANES]
                acc = acc + w_dw_ref[k:k + 1, c0:c0 + LANES] * win
            cv_out[r0:r0 + CONV_ROW_CHUNK, c0:c0 + LANES] = acc

    pos = (s * ts + 1 + lax.broadcasted_iota(jnp.int32, (ts, 1), 0)).astype(_F32)
    for g, w in enumerate(POOL_WINDOWS):
        c0 = g * POOL_GROUP_IN
        win = up_ext[:, c0:c0 + POOL_GROUP_IN]
        span = 1
        while span < w:
            win = win + pltpu.roll(win, span, 0)
            span *= 2
        cur = up_ext[POOL_HALO:POOL_HALO + ts, c0:c0 + POOL_GROUP_IN]
        pooled = win[POOL_HALO:, :] / jnp.minimum(pos, float(w)) - cur
        d0 = g * POOL_GROUP_OUT
        mix_ref[:, d0:d0 + POOL_GROUP_OUT] = (
            _dot(pooled.astype(_BF16), w_pool_ref[g]) * pscale_ref[:, d0:d0 + POOL_GROUP_OUT])

    u = cv_out[...]
    mu = jnp.mean(u, axis=-1, keepdims=True)
    uc = u - mu
    var = jnp.mean(uc * uc, axis=-1, keepdims=True)
    un = uc * lax.rsqrt(var + EPS) * ln_g_ref[...] + ln_b_ref[...]
    un = un * jax.nn.sigmoid(un)
    y_conv = _dot(un.astype(_BF16), w_pw_ref[...])
    mix = gate_ref[:, 0:D_MODEL] * mix_ref[...] + gate_ref[:, D_MODEL:] * y_conv

    o_ref[...] = x + _dot(mix.astype(_BF16), w_out_ref[...])


def _mixer(layer, x, g, w_in, w_pool, pscale, w_dw, b_dw, ln_g, ln_b, w_pw, w_out, job=None):
    B, S, D = x.shape
    ts = SEQ_TILE
    n_seq = S // ts
    shift_rows = ts + CONV_HALO - SUBLANES
    tile = pl.BlockSpec((None, ts, D), lambda b, s: (b, s, 0))
    params = (g, w_in, w_pool, pscale, w_dw, b_dw, ln_g, ln_b, w_pw, w_out)
    in_specs = [tile] + [_layer_spec(p, layer) for p in params]
    args = [x, *params]
    out_specs = [tile]
    out_shape = [jax.ShapeDtypeStruct(x.shape, x.dtype)]
    aliases, cast = {}, None
    if job is not None:
        assert B * n_seq == _cast_slices(job) // 2
        extra = _cast_plumbing(job, lambda b, s: b * n_seq + s, len(in_specs), len(out_specs))
        in_specs += extra[0]
        args += extra[1]
        out_specs += extra[2]
        out_shape += extra[3]
        aliases, cast = extra[4], job.gu_bf is not None
    out = pl.pallas_call(
        functools.partial(_mixer_kernel, cast=cast),
        grid=(B, n_seq),
        in_specs=in_specs,
        out_specs=out_specs,
        out_shape=out_shape,
        input_output_aliases=aliases,
        scratch_shapes=[pltpu.VMEM((POOL_HALO + ts, POOL_WIDTH), _F32),
                        pltpu.VMEM((CONV_HALO + ts, CONV_WIDTH), _F32),
                        pltpu.VMEM((SUBLANES - 1, shift_rows, CONV_WIDTH), _F32),
                        pltpu.VMEM((ts, CONV_WIDTH), _F32),
                        pltpu.VMEM((ts, D), _F32),
                        pltpu.VMEM((ts, 2 * D), _F32)],
        compiler_params=pltpu.CompilerParams(
            dimension_semantics=("parallel", "arbitrary"), vmem_limit_bytes=VMEM_LIMIT),
        name="mixer",
    )(*args)
    return out[0] if job is None else tuple(out)


def _kv_kernel(m_ref, g_ref, w_ref, k_ref, v_ref):
    m = _rms(m_ref[...], g_ref[...]).astype(_BF16)
    k_ref[...] = (_dot(m, w_ref[:, 0:D_MODEL]) * (XHEAD_DIM ** -0.5)).astype(_BF16)
    v_ref[...] = _dot(m, w_ref[:, D_MODEL:2 * D_MODEL]).astype(_BF16)


def _kv_proj(mem, g, w_kv):
    B, M, D = mem.shape
    L = w_kv.shape[0]
    rows = B * M
    out_blk = pl.BlockSpec((None, rows, D), lambda l: (l, 0, 0))
    k, v = pl.pallas_call(
        _kv_kernel,
        grid=(L,),
        in_specs=[pl.BlockSpec((rows, D), lambda l: (0, 0)),
                  pl.BlockSpec((None,) + g.shape[1:], lambda l: (l, 0, 0)),
                  pl.BlockSpec((None,) + w_kv.shape[1:], lambda l: (l, 0, 0))],
        out_specs=[out_blk, out_blk],
        out_shape=[jax.ShapeDtypeStruct((L, rows, D), _BF16)] * 2,
        compiler_params=pltpu.CompilerParams(
            dimension_semantics=("parallel",), vmem_limit_bytes=VMEM_LIMIT),
        name="kv_proj",
    )(mem.reshape(rows, D), g, w_kv)
    return k.reshape(L, B, M, D), v.reshape(L, B, M, D)


class _CastJob(NamedTuple):
    moe_layer: int
    first_slice: int
    w_gu: jax.Array
    w_down: jax.Array
    gu_bf: Optional[jax.Array]
    down_bf: Optional[jax.Array]


def _cast_slices(job):
    return job.w_gu.shape[1] * CAST_PARTS


def _cast_plumbing(job, step_of, n_in, n_out):
    _, _, d, f2 = job.w_gu.shape
    f = job.w_down.shape[2]

    def part(*grid_idx):
        q = jnp.minimum(job.first_slice + step_of(*grid_idx), _cast_slices(job) - 1)
        return q // CAST_PARTS, q % CAST_PARTS

    def src_idx(*grid_idx):
        return (job.moe_layer,) + part(*grid_idx) + (0,)

    def dst_idx(*grid_idx):
        return part(*grid_idx) + (0,)

    in_specs = [pl.BlockSpec((None, None, d // CAST_PARTS, f2), src_idx),
                pl.BlockSpec((None, None, f // CAST_PARTS, d), src_idx)]
    out_specs = [pl.BlockSpec((None, d // CAST_PARTS, f2), dst_idx),
                 pl.BlockSpec((None, f // CAST_PARTS, d), dst_idx)]
    args, aliases = [job.w_gu, job.w_down], {}
    if job.gu_bf is not None:
        in_specs += [pl.BlockSpec(memory_space=pl.ANY)] * 2
        args += [job.gu_bf, job.down_bf]
        aliases = {n_in + 2: n_out, n_in + 3: n_out + 1}
    out_shapes = [jax.ShapeDtypeStruct(job.w_gu.shape[1:], _BF16),
                  jax.ShapeDtypeStruct(job.w_down.shape[1:], _BF16)]
    return in_specs, args, out_specs, out_shapes, aliases


def _split_cast_refs(rest, n_out, cast_fill):
    n_cast_in = 4 if cast_fill else 2
    gu_in, down_in = rest[0], rest[1]
    host_out = rest[n_cast_in:n_cast_in + n_out]
    gu_out, down_out = rest[n_cast_in + n_out], rest[n_cast_in + n_out + 1]

    def cast():
        gu_out[...] = gu_in[...].astype(_BF16)
        down_out[...] = down_in[...].astype(_BF16)

    return tuple(host_out) + tuple(rest[n_cast_in + n_out + 2:]), cast


def _xattn_kernel(x_ref, g_ref, wq_ref, k_ref, v_ref, wo_ref, o_ref, att_ref):
    x = x_ref[...]
    h = _rms(x, g_ref[...]).astype(_BF16)
    q = _dot(h, wq_ref[...]).astype(_BF16)
    for hd in range(N_XHEADS):
        c0 = hd * XHEAD_DIM
        sc = lax.dot_general(q[:, c0:c0 + XHEAD_DIM], k_ref[:, c0:c0 + XHEAD_DIM],
                             (((1,), (1,)), ((), ())), preferred_element_type=_F32)
        p = jnp.exp(sc - jnp.max(sc, axis=-1, keepdims=True))
        denom = jnp.sum(p, axis=-1, keepdims=True)
        att = _dot(p.astype(_BF16), v_ref[:, c0:c0 + XHEAD_DIM]) / denom
        att_ref[:, c0:c0 + XHEAD_DIM] = att.astype(_BF16)
    o_ref[...] = x + _dot(att_ref[...], wo_ref[...])


def _xattn(layer, x, g, w_q, k, v, w_o):
    B, S, D = x.shape
    M = k.shape[2]
    ts = SEQ_TILE
    tile = pl.BlockSpec((None, ts, D), lambda b, s: (b, s, 0))
    kv_blk = pl.BlockSpec((None, None, M, D), lambda b, s: (layer, b, 0, 0))
    return pl.pallas_call(
        _xattn_kernel,
        grid=(B, S // ts),
        in_specs=[tile, _layer_spec(g, layer), _layer_spec(w_q, layer), kv_blk, kv_blk,
                  _layer_spec(w_o, layer)],
        out_specs=tile,
        out_shape=jax.ShapeDtypeStruct(x.shape, x.dtype),
        scratch_shapes=[pltpu.VMEM((ts, D), _BF16)],
        compiler_params=pltpu.CompilerParams(
            dimension_semantics=("parallel", "parallel"), vmem_limit_bytes=VMEM_LIMIT),
        name="xattn",
    )(x, g, w_q, k, v, w_o)


def _ffn_kernel(x_ref, g_ref, w_gu_ref, w_down_ref, *rest, cast):
    if cast is not None:
        rest, cast_slice = _split_cast_refs(rest, 1, cast)
        cast_slice()
    (o_ref,) = rest
    x = x_ref[...]
    h = _rms(x, g_ref[...]).astype(_BF16)
    acc = x
    for c0, n in FF_CHUNKS:
        gate = _dot(h, w_gu_ref[:, c0:c0 + n])
        up = _dot(h, w_gu_ref[:, D_FF + c0:D_FF + c0 + n])
        act = (gate * jax.nn.sigmoid(gate) * up).astype(_BF16)
        acc = acc + _dot(act, w_down_ref[c0:c0 + n, :])
    o_ref[...] = acc


def _ffn(layer, ffn_layer, x2, g, w_gu, w_down, job=None):
    T, D = x2.shape
    tm = TOK_TILE
    tile = pl.BlockSpec((tm, D), lambda i: (i, 0))
    in_specs = [tile, _layer_spec(g, layer), _layer_spec(w_gu, ffn_layer),
                _layer_spec(w_down, ffn_layer)]
    args = [x2, g, w_gu, w_down]
    out_specs = [tile]
    out_shape = [jax.ShapeDtypeStruct(x2.shape, x2.dtype)]
    aliases, cast = {}, None
    if job is not None:
        assert T // tm == (job.w_gu.shape[1] * CAST_PARTS) // 2
        extra = _cast_plumbing(job, lambda i: i, len(in_specs), len(out_specs))
        in_specs += extra[0]
        args += extra[1]
        out_specs += extra[2]
        out_shape += extra[3]
        aliases, cast = extra[4], job.gu_bf is not None
    out = pl.pallas_call(
        functools.partial(_ffn_kernel, cast=cast),
        grid=(T // tm,),
        in_specs=in_specs,
        out_specs=out_specs,
        out_shape=out_shape,
        input_output_aliases=aliases,
        compiler_params=pltpu.CompilerParams(
            dimension_semantics=("parallel",), vmem_limit_bytes=VMEM_LIMIT),
        name="ffn",
    )(*args)
    return out[0] if job is None else tuple(out)


def _pack_halves(y):
    half = y.shape[1] // 2
    lo = lax.bitcast_convert_type(y[:, :half].astype(_BF16).astype(_F32), _U32)
    hi = lax.bitcast_convert_type(y[:, half:].astype(_BF16).astype(_F32), _U32)
    return (lo >> 16) | (hi & HIGH_HALF_MASK)


def _unpack_halves(p):
    lo = lax.bitcast_convert_type(p << 16, _F32)
    hi = lax.bitcast_convert_type(p & HIGH_HALF_MASK, _F32)
    return lo, hi


def _route_kernel(x_ref, g_ref, w_r_ref, b_r_ref, earlier_ref, lane_before_ref,
                  meta_ref, cnt_ref, tab_ref, hs_ref,
                  carry_ref, placed_smem, second_smem, sorted_buf, sem,
                  *, region_rows, sort_rows):
    i = pl.program_id(0)
    tm = x_ref.shape[0]

    @pl.when(i == 0)
    def _():
        carry_ref[...] = jnp.zeros_like(carry_ref)
        for e in range(N_EXPERTS):
            placed_smem[e] = 0
        sorted_buf[:, sort_rows:, :] = jnp.zeros(
            (sorted_buf.shape[0], sorted_buf.shape[1] - sort_rows, sorted_buf.shape[2]), _U32)

    hf = _rms(x_ref[...], g_ref[...])

    lane = lax.broadcasted_iota(jnp.int32, (tm, LANES), 1)
    h_hi = hf.astype(_BF16)
    h_lo = (hf - h_hi.astype(_F32)).astype(_BF16)
    w_r = w_r_ref[...]
    w_hi = w_r.astype(_BF16)
    w_lo = (w_r - w_hi.astype(_F32)).astype(_BF16)
    hi_terms = _dot(h_hi, jnp.concatenate([w_hi, w_lo], axis=1))
    logits = hi_terms[:, :LANES] + hi_terms[:, LANES:] + _dot(h_lo, w_hi) + b_r_ref[...]
    v1 = jnp.max(logits, axis=-1, keepdims=True)
    i1 = jnp.min(jnp.where(logits == v1, lane, LANES), axis=-1, keepdims=True)
    rest = jnp.where(lane == i1, NEG_BIG, logits)
    v2 = jnp.max(rest, axis=-1, keepdims=True)
    i2 = jnp.min(jnp.where(rest == v2, lane, LANES), axis=-1, keepdims=True)
    z = jnp.exp(v2 - v1)
    g1 = 1.0 / (1.0 + z)
    g2 = z / (1.0 + z)

    sel1 = lane == i1
    sel2 = lane == i2
    chosen = jnp.where(sel1 | sel2, 1.0, 0.0)
    before = _dot(earlier_ref[...], chosen.astype(_BF16))
    cnt = jnp.sum(chosen, axis=0, keepdims=True)
    cnt_pad = jnp.ceil(cnt * (1.0 / SUBLANES)) * SUBLANES
    group_off = _dot(jnp.broadcast_to(cnt_pad, (SUBLANES, LANES)).astype(_BF16),
                     lane_before_ref[...])[0:1, :]
    local = group_off + before
    loc1 = jnp.sum(jnp.where(sel1, local, 0.0), axis=-1, keepdims=True)
    loc2 = jnp.sum(jnp.where(sel2, local, 0.0), axis=-1, keepdims=True)

    slot = lax.rem(i, 2)
    loc_t = jnp.where(lane == 0, loc1, jnp.where(lane == 1, loc2, -1.0)).T
    row = lax.broadcasted_iota(jnp.int32, (sort_rows, tm), 0).astype(_F32)
    perm = jnp.where((row == loc_t[0:1, :]) | (row == loc_t[1:2, :]), 1.0, 0.0).astype(_BF16)
    sorted_buf[slot, 0:sort_rows, :] = _pack_halves(_dot(perm, h_hi))

    def window(src, dst):
        return pltpu.make_async_copy(sorted_buf.at[slot, pl.ds(src, GROUP_WINDOW)],
                                     hs_ref.at[pl.ds(dst, GROUP_WINDOW)], sem)

    def wait_windows(second_flags):
        for e in range(N_EXPERTS):
            window(0, 0).wait()

            @pl.when(second_flags(e) != 0)
            def _():
                window(0, 0).wait()

    @pl.when(i > 0)
    def _():
        wait_windows(lambda e: second_smem[e])

    off_i = group_off.astype(jnp.int32)
    cnt_i = cnt_pad.astype(jnp.int32)
    for e in range(N_EXPERTS):
        src = pl.multiple_of(off_i[0, e], SUBLANES)
        start = pl.multiple_of(e * region_rows + placed_smem[e], SUBLANES)
        rows = cnt_i[0, e]
        second = (rows > GROUP_WINDOW).astype(jnp.int32)
        window(src, start).start()

        @pl.when(second != 0)
        def _():
            window(src + GROUP_WINDOW, start + GROUP_WINDOW).start()

        second_smem[e] = second
        placed_smem[e] = placed_smem[e] + rows
        tab_ref[0, e] = start
        tab_ref[0, N_EXPERTS + e] = rows
    carry_ref[...] += cnt_pad
    cnt_ref[...] = carry_ref[...]

    @pl.when(i == pl.num_programs(0) - 1)
    def _():
        wait_windows(lambda e: second_smem[e])

        def tail(e, part):
            dst = pl.multiple_of(e * region_rows + placed_smem[e], SUBLANES) + part * GROUP_WINDOW
            return window(sort_rows, dst)

        for e in range(N_EXPERTS):
            for part in range(EXPERT_ROW_TILE // GROUP_WINDOW):
                tail(e, part).start()
        for e in range(N_EXPERTS):
            for part in range(EXPERT_ROW_TILE // GROUP_WINDOW):
                tail(e, part).wait()

    meta = jnp.where(lane == META_G1, g1, 0.0)
    meta = jnp.where(lane == META_G2, g2, meta)
    meta = jnp.where(lane == META_L1, loc1, meta)
    meta = jnp.where(lane == META_L2, loc2, meta)
    meta_ref[...] = meta


def _region_rows(n_tokens):
    pad = (n_tokens // ROUTE_TILE) * (SUBLANES - 1) + 2 * EXPERT_ROW_TILE
    return -(-(n_tokens + pad) // EXPERT_ROW_TILE) * EXPERT_ROW_TILE


def _route(layer, moe_layer, x2, g, w_r, b_r):
    T, D = x2.shape
    tm = ROUTE_TILE
    region_rows = _region_rows(T)
    sort_rows = TOP_K * tm + N_EXPERTS * SUBLANES
    earlier = jnp.asarray(np.tril(np.ones((tm, tm), np.float32), -1), _BF16)
    lane_before = jnp.asarray(np.triu(np.ones((LANES, LANES), np.float32), 1), _BF16)
    const = lambda n: pl.BlockSpec((n, n), lambda i: (0, 0), pipeline_mode=pl.Buffered(1))
    return pl.pallas_call(
        functools.partial(_route_kernel, region_rows=region_rows, sort_rows=sort_rows),
        grid=(T // tm,),
        in_specs=[pl.BlockSpec((tm, D), lambda i: (i, 0)), _layer_spec(g, layer),
                  _layer_spec(w_r, moe_layer), _layer_spec(b_r, moe_layer),
                  const(tm), const(LANES)],
        out_specs=[pl.BlockSpec((tm, LANES), lambda i: (i, 0)),
                   pl.BlockSpec((1, LANES), lambda i: (0, 0)),
                   pl.BlockSpec((None, 1, 2 * N_EXPERTS), lambda i: (i, 0, 0),
                                memory_space=pltpu.SMEM),
                   pl.BlockSpec(memory_space=pl.ANY)],
        out_shape=[jax.ShapeDtypeStruct((T, LANES), _F32),
                   jax.ShapeDtypeStruct((1, LANES), _F32),
                   jax.ShapeDtypeStruct((T // tm, 1, 2 * N_EXPERTS), jnp.int32),
                   jax.ShapeDtypeStruct((N_EXPERTS * region_rows, D // 2), _U32)],
        scratch_shapes=[pltpu.VMEM((1, LANES), _F32),
                        pltpu.SMEM((N_EXPERTS,), jnp.int32),
                        pltpu.SMEM((N_EXPERTS,), jnp.int32),
                        pltpu.VMEM((2, sort_rows + GROUP_WINDOW, D // 2), _U32),
                        pltpu.SemaphoreType.DMA],
        compiler_params=pltpu.CompilerParams(
            dimension_semantics=("arbitrary",), vmem_limit_bytes=VMEM_LIMIT),
        name="moe_route",
    )(x2, g, w_r, b_r, earlier, lane_before)


def _expert_kernel(te_ref, tb_ref, tv_ref, hs_ref, w_gu_ref, w_down_ref, ys_ref):
    del te_ref, tb_ref
    i = pl.program_id(0)

    rt = ys_ref.shape[0]
    half = rt // 2
    rows_used = tv_ref[i]

    def swiglu_rows(n_rows):
        lo, hi = _unpack_halves(hs_ref[0:n_rows, :])
        h = jnp.concatenate([lo.astype(_BF16), hi.astype(_BF16)], axis=1)
        acc = None
        for c0, n in FF_CHUNKS:
            gate = _dot(h, w_gu_ref[:, c0:c0 + n])
            up = _dot(h, w_gu_ref[:, D_FF + c0:D_FF + c0 + n])
            act = (gate * jax.nn.sigmoid(gate) * up).astype(_BF16)
            part = _dot(act, w_down_ref[c0:c0 + n, :])
            acc = part if acc is None else acc + part
        ys_ref[0:n_rows, :] = _pack_halves(acc)

    @pl.when(rows_used == 0)
    def _():
        ys_ref[...] = jnp.zeros_like(ys_ref)

    @pl.when((rows_used > 0) & (rows_used <= half))
    def _():
        swiglu_rows(half)
        ys_ref[half:, :] = jnp.zeros((rt - half, ys_ref.shape[1]), ys_ref.dtype)

    @pl.when(rows_used > half)
    def _():
        swiglu_rows(rt)


def _experts(tile_expert, tile_block, tile_rows, hs, w_gu, w_down):
    n_rows, W = hs.shape
    rt = EXPERT_ROW_TILE
    D = w_gu.shape[1]
    grid_spec = pltpu.PrefetchScalarGridSpec(
        num_scalar_prefetch=3,
        grid=(tile_expert.shape[0],),
        in_specs=[pl.BlockSpec((rt, W), lambda i, te, tb, tv: (jnp.where(tv[i] > 0, tb[i], 0), 0)),
                  pl.BlockSpec((None, D, 2 * D_FF), lambda i, te, tb, tv: (te[i], 0, 0)),
                  pl.BlockSpec((None, D_FF, D), lambda i, te, tb, tv: (te[i], 0, 0))],
        out_specs=pl.BlockSpec((rt, W), lambda i, te, tb, tv: (tb[i], 0)),
    )
    return pl.pallas_call(
        _expert_kernel,
        grid_spec=grid_spec,
        out_shape=jax.ShapeDtypeStruct((n_rows + rt, W), hs.dtype),
        compiler_params=pltpu.CompilerParams(
            dimension_semantics=("arbitrary",), vmem_limit_bytes=VMEM_LIMIT),
        name="moe_experts",
    )(tile_expert, tile_block, tile_rows, hs, w_gu, w_down)


def _combine_kernel(tab_ref, x_ref, meta_ref, g_final_ref, ys_ref, o_ref, ybuf, sem, *, final_norm):
    i = pl.program_id(0)
    n = pl.num_programs(0)
    tm, D = x_ref.shape
    slot = lax.rem(i, 2)

    @pl.when(i == 0)
    def _():
        ybuf[...] = jnp.zeros_like(ybuf)

    def fetch(tile, to_slot):
        off = 0
        for e in range(N_EXPERTS):
            start = tab_ref[tile, e]
            rows = tab_ref[tile, N_EXPERTS + e]
            size = ROUTE_TILE
            while size >= SUBLANES:
                above = rows & ~(2 * size - 1)
                src = pl.multiple_of(start + above, SUBLANES)
                dst = pl.multiple_of(off + above, SUBLANES)

                @pl.when((rows & size) != 0)
                def _():
                    pltpu.make_async_copy(ys_ref.at[pl.ds(src, size)],
                                          ybuf.at[to_slot, pl.ds(dst, size)], sem.at[to_slot]).start()
                size //= 2
            off = off + rows
        return off

    def total_rows(tile):
        total = 0
        for e in range(N_EXPERTS):
            total = total + tab_ref[tile, N_EXPERTS + e]
        return total

    @pl.when(i == 0)
    def _():
        fetch(0, 0)

    @pl.when(i + 1 < n)
    def _():
        fetch(i + 1, 1 - slot)

    landed = pl.multiple_of(total_rows(i), SUBLANES)
    pltpu.make_async_copy(ys_ref.at[pl.ds(0, landed)], ybuf.at[slot, pl.ds(0, landed)],
                          sem.at[slot]).wait()

    lo, hi = _unpack_halves(ybuf[slot])
    y = jnp.concatenate([lo.astype(_BF16), hi.astype(_BF16)], axis=1)
    col = lax.broadcasted_iota(jnp.int32, (tm, ybuf.shape[1]), 1).astype(_F32)
    pick1 = jnp.where(col == meta_ref[:, META_L1:META_L1 + 1], 1.0, 0.0).astype(_BF16)
    pick2 = jnp.where(col == meta_ref[:, META_L2:META_L2 + 1], 1.0, 0.0).astype(_BF16)
    out = (x_ref[...] + meta_ref[:, META_G1:META_G1 + 1] * _dot(pick1, y)
           + meta_ref[:, META_G2:META_G2 + 1] * _dot(pick2, y))
    if final_norm:
        out = _rms(out, g_final_ref[...])
    o_ref[...] = out


def _combine(tab, x2, meta, ys, g_final, final_norm):
    T, D = x2.shape
    tm = ROUTE_TILE
    sort_rows = -(-(TOP_K * tm + N_EXPERTS * SUBLANES) // MXU_DIM) * MXU_DIM
    tile = pl.BlockSpec((tm, D), lambda i, tab: (i, 0))
    grid_spec = pltpu.PrefetchScalarGridSpec(
        num_scalar_prefetch=1,
        grid=(T // tm,),
        in_specs=[tile, pl.BlockSpec((tm, LANES), lambda i, tab: (i, 0)),
                  pl.BlockSpec((1, D), lambda i, tab: (0, 0)), pl.BlockSpec(memory_space=pl.ANY)],
        out_specs=tile,
        scratch_shapes=[pltpu.VMEM((2, sort_rows, D // 2), _U32), pltpu.SemaphoreType.DMA((2,))],
    )
    return pl.pallas_call(
        functools.partial(_combine_kernel, final_norm=final_norm),
        grid_spec=grid_spec,
        out_shape=jax.ShapeDtypeStruct(x2.shape, x2.dtype),
        compiler_params=pltpu.CompilerParams(
            dimension_semantics=("arbitrary",), vmem_limit_bytes=VMEM_LIMIT),
        name="moe_combine",
    )(tab, x2, meta, g_final, ys)


def _moe(layer, moe_layer, x2, g, w_r, b_r, w_gu, w_down, g_final, final_norm):
    T, D = x2.shape
    E = w_gu.shape[0]
    rt = EXPERT_ROW_TILE
    meta, counts, tab, hs = _route(layer, moe_layer, x2, g, w_r, b_r)

    region_tiles = hs.shape[0] // (E * rt)
    max_rows = TOP_K * T + (T // ROUTE_TILE) * E * (SUBLANES - 1)
    n_tiles = -(-max_rows // rt) + E
    counts = counts[0, :E].astype(jnp.int32)
    group_tiles = (counts + rt - 1) // rt
    tile_end = jnp.cumsum(group_tiles)
    tile_ids = jnp.arange(n_tiles, dtype=jnp.int32)
    tile_expert = jnp.minimum(
        jnp.sum((tile_ids[:, None] >= tile_end[None, :]).astype(jnp.int32), axis=1), E - 1)
    tile_valid = (tile_ids < tile_end[E - 1]).astype(jnp.int32)
    in_group = tile_ids - (tile_end - group_tiles)[tile_expert]
    tile_block = jnp.where(tile_valid != 0, tile_expert * region_tiles + in_group, E * region_tiles)

    tile_rows = jnp.clip(counts[tile_expert] - in_group * rt, 0, rt) * tile_valid
    ys = _experts(tile_expert, tile_block.astype(jnp.int32), tile_rows.astype(jnp.int32), hs, w_gu, w_down)
    return _combine(tab.reshape(tab.shape[0], tab.shape[2]), x2, meta, ys, g_final, final_norm)


def _final_norm_kernel(x_ref, g_ref, o_ref):
    o_ref[...] = _rms(x_ref[...], g_ref[...])


def _final_norm(x2, g):
    T, D = x2.shape
    tm = TOK_TILE
    tile = pl.BlockSpec((tm, D), lambda i: (i, 0))
    return pl.pallas_call(
        _final_norm_kernel,
        grid=(T // tm,),
        in_specs=[tile, pl.BlockSpec((1, D), lambda i: (0, 0))],
        out_specs=tile,
        out_shape=jax.ShapeDtypeStruct(x2.shape, x2.dtype),
        compiler_params=pltpu.CompilerParams(dimension_semantics=("parallel",)),
        name="final_norm",
    )(x2, g)


def kernel(x, mem, g_mix, w_in, w_pool, pool_scale, w_dw, b_dw, conv_ln_g, conv_ln_b, w_conv_out, w_mix_out, g_xattn, g_mem, w_xq, w_xkv, w_xo, g_ffn, w_ffn_gu, w_ffn_down, w_router, b_router, w_moe_gu, w_moe_down, g_final):
    B, S, D = x.shape
    depth = g_mix.shape[0]
    bf = lambda a: a.astype(_BF16)
    rows = lambda a: a.reshape(a.shape[0], 1, a.shape[1])

    g_mix, pool_scale, b_dw, conv_ln_g, conv_ln_b = map(rows, (g_mix, pool_scale, b_dw, conv_ln_g, conv_ln_b))
    g_xattn, g_mem, g_ffn = map(rows, (g_xattn, g_mem, g_ffn))
    w_dw = w_dw.reshape(depth, CONV_KERNEL, CONV_WIDTH)
    w_in, w_pool, w_conv_out, w_mix_out = map(bf, (w_in, w_pool, w_conv_out, w_mix_out))
    w_xq, w_xkv, w_xo = map(bf, (w_xq, w_xkv, w_xo))
    w_ffn_gu, w_ffn_down = map(bf, (w_ffn_gu, w_ffn_down))
    w_r_pad = jnp.pad(w_router, ((0, 0), (0, 0), (0, LANES - N_EXPERTS)))
    b_r_pad = rows(jnp.pad(b_router, ((0, 0), (0, LANES - N_EXPERTS)), constant_values=NEG_BIG))
    g_final = g_final.reshape(1, D)

    half_slices = (w_moe_gu.shape[1] * CAST_PARTS) // 2
    k_all, v_all = _kv_proj(mem, g_mem, w_xkv)
    for i in range(depth):
        last = i == depth - 1
        mixer_args = (i, x, g_mix, w_in, w_pool, pool_scale, w_dw, b_dw, conv_ln_g, conv_ln_b,
                      w_conv_out, w_mix_out)
        if i % 2 == 0 and not last:
            job = lambda first, gu, down: _CastJob((i + 1) // 2, first, w_moe_gu, w_moe_down, gu, down)
            x, moe_gu_bf, moe_down_bf = _mixer(*mixer_args, job(0, None, None))
            x = _xattn(i, x, g_xattn, w_xq, k_all, v_all, w_xo)
            x2, moe_gu_bf, moe_down_bf = _ffn(i, i // 2, x.reshape(B * S, D), g_ffn, w_ffn_gu, w_ffn_down,
                                              job(half_slices, moe_gu_bf, moe_down_bf))
        else:
            x = _mixer(*mixer_args)
            x = _xattn(i, x, g_xattn, w_xq, k_all, v_all, w_xo)
            if i % 2 == 0:
                x2 = _final_norm(_ffn(i, i // 2, x.reshape(B * S, D), g_ffn, w_ffn_gu, w_ffn_down), g_final)
            else:
                x2 = _moe(i, i // 2, x.reshape(B * S, D), g_ffn, w_r_pad, b_r_pad, moe_gu_bf, moe_down_bf,
                          g_final, last)
        x = x2.reshape(B, S, D)
    return x
```

```python
import functools
from typing import NamedTuple, Optional

import jax
import jax.numpy as jnp
import numpy as np
from jax import lax
from jax.experimental import pallas as pl
from jax.experimental.pallas import tpu as pltpu

D_MODEL = 1024
POOL_WIDTH = 512
POOL_GROUPS = 4
POOL_WINDOWS = (2, 4, 8, 16)
POOL_GROUP_IN = POOL_WIDTH // POOL_GROUPS
POOL_GROUP_OUT = D_MODEL // POOL_GROUPS
CONV_WIDTH = 512
CONV_KERNEL = 31
IN_COLS = POOL_WIDTH + 2 * CONV_WIDTH + 2 * D_MODEL
N_XHEADS = 4
XHEAD_DIM = D_MODEL // N_XHEADS
D_FF = 2816
N_EXPERTS = 8
TOP_K = 2
EPS = 1e-6

COL_POOL = 0
COL_GLU_A = POOL_WIDTH
COL_GLU_B = POOL_WIDTH + CONV_WIDTH
COL_GATE_POOL = POOL_WIDTH + 2 * CONV_WIDTH
COL_GATE_CONV = COL_GATE_POOL + D_MODEL

LANES = 128
SUBLANES = 8
POOL_HALO = 16
CONV_HALO = 32
SEQ_TILE = 512
TOK_TILE = 512
MXU_DIM = 256
FF_CHUNKS = ((0, 6 * MXU_DIM), (6 * MXU_DIM, 5 * MXU_DIM))
assert sum(n for _, n in FF_CHUNKS) == D_FF
CONV_ROW_CHUNK = 64
ROUTE_TILE = 512
EXPERT_ROW_TILE = 512
GROUP_WINDOW = 256
CAST_PARTS = 8
VMEM_LIMIT = 56 * 1024 * 1024
NEG_BIG = -1e30
META_G1, META_G2, META_L1, META_L2 = range(4)

_F32 = jnp.float32
_BF16 = jnp.bfloat16
_U32 = jnp.uint32
HIGH_HALF_MASK = np.uint32(0xFFFF0000)


def _layer_spec(arr, layer):
    tail = (0,) * (arr.ndim - 1)
    return pl.BlockSpec((None,) + arr.shape[1:], lambda *_: (layer,) + tail,
                        pipeline_mode=pl.Buffered(1))


def _rms(x, g):
    return x * lax.rsqrt(jnp.mean(x * x, axis=-1, keepdims=True) + EPS) * g


def _dot(a, b):
    return jnp.dot(a, b, preferred_element_type=_F32)


def _zero_like_bits(v):
    u = lax.bitcast_convert_type(v, _U32)
    return lax.bitcast_convert_type((u >> 16) >> 16, _F32)


def _mixer_kernel(x_ref, g_ref, w_in_ref, w_pool_ref, pscale_ref, w_dw_ref, b_dw_ref,
                  ln_g_ref, ln_b_ref, w_pw_ref, w_out_ref, *rest, cast):
    if cast is not None:
        rest, cast_slice = _split_cast_refs(rest, 1, cast)
        cast_slice()
    o_ref, up_ext, cv_ext, cv_shift, cv_out, mix_ref, gate_ref = rest
    s = pl.program_id(1)
    ts = x_ref.shape[0]

    @pl.when(s == 0)
    def _():
        up_ext[0:POOL_HALO, :] = jnp.zeros((POOL_HALO, POOL_WIDTH), _F32)
        cv_ext[:, 0:CONV_HALO, :] = jnp.zeros((cv_ext.shape[0], CONV_HALO, LANES), _F32)

    @pl.when(s > 0)
    def _():
        up_ext[0:POOL_HALO, :] = up_ext[ts:ts + POOL_HALO, :]
        cv_ext[:, 0:CONV_HALO, :] = cv_ext[:, ts:ts + CONV_HALO, :]

    x = x_ref[...]
    h = _rms(x, g_ref[...]).astype(_BF16)

    glu_a = _dot(h, w_in_ref[:, COL_GLU_A:COL_GLU_A + CONV_WIDTH])
    glu_b = _dot(h, w_in_ref[:, COL_GLU_B:COL_GLU_B + CONV_WIDTH])
    glu = glu_a * jax.nn.sigmoid(glu_b)
    for blk in range(cv_ext.shape[0]):
        cv_ext[blk, CONV_HALO:CONV_HALO + ts, :] = glu[:, blk * LANES:(blk + 1) * LANES]

    shift_rows = cv_shift.shape[2]
    for r in range(1, SUBLANES):
        cv_shift[r - 1] = cv_ext[:, r:r + shift_rows, :]

    tap0 = CONV_HALO - (CONV_KERNEL - 1)
    def side_task(col0, store):
        res = _dot(h, w_in_ref[:, col0:col0 + MXU_DIM])
        store(res)
        return _zero_like_bits(res[ts - 1:ts, MXU_DIM - LANES:])

    def pool_store(c):
        def store(res):
            up_ext[POOL_HALO:POOL_HALO + ts, c:c + MXU_DIM] = res
        return store

    def gate_store(c):
        def store(res):
            gate_ref[:, c:c + MXU_DIM] = jax.nn.sigmoid(res)
        return store

    side_tasks = ([(COL_POOL + c, pool_store(c)) for c in range(0, POOL_WIDTH, MXU_DIM)]
                  + [(COL_GATE_POOL + c, gate_store(c)) for c in range(0, 2 * D_MODEL, MXU_DIM)])
    unit_rows = 2 * CONV_ROW_CHUNK
    units = [(c0, q0) for c0 in range(0, CONV_WIDTH, LANES) for q0 in range(0, ts, unit_rows)]
    assert len(side_tasks) <= len(units)
    pace = None
    for u, (c0, q0) in enumerate(units):
        bias = b_dw_ref[:, c0:c0 + LANES]
        if pace is not None:
            bias = bias + pace
        pace = side_task(*side_tasks[u]) if u < len(side_tasks) else None
        for r0 in range(q0, q0 + unit_rows, CONV_ROW_CHUNK):
            acc = jnp.broadcast_to(bias, (CONV_ROW_CHUNK, LANES))
            for k in range(CONV_KERNEL):
                off = tap0 + k
                mis = off % SUBLANES
                row = r0 + off - mis
                if mis == 0:
                    win = cv_ext[c0 // LANES, row:row + CONV_ROW_CHUNK, :]
                else:
                    win = cv_shift[mis - 1, c0 // LANES, row:row + CONV_ROW_CHUNK, :]
                acc = acc + w_dw_ref[k:k + 1, c0:c0 + LANES] * win
            cv_out[r0:r0 + CONV_ROW_CHUNK, c0:c0 + LANES] = acc

    pos = (s * ts + 1 + lax.broadcasted_iota(jnp.int32, (ts, 1), 0)).astype(_F32)
    for g, w in enumerate(POOL_WINDOWS):
        c0 = g * POOL_GROUP_IN
        win = up_ext[:, c0:c0 + POOL_GROUP_IN]
        span = 1
        while span < w:
            win = win + pltpu.roll(win, span, 0)
            span *= 2
        cur = up_ext[POOL_HALO:POOL_HALO + ts, c0:c0 + POOL_GROUP_IN]
        pooled = win[POOL_HALO:, :] / jnp.minimum(pos, float(w)) - cur
        d0 = g * POOL_GROUP_OUT
        mix_ref[:, d0:d0 + POOL_GROUP_OUT] = (
            _dot(pooled.astype(_BF16), w_pool_ref[g]) * pscale_ref[:, d0:d0 + POOL_GROUP_OUT])

    u = cv_out[...]
    mu = jnp.mean(u, axis=-1, keepdims=True)
    uc = u - mu
    var = jnp.mean(uc * uc, axis=-1, keepdims=True)
    un = uc * lax.rsqrt(var + EPS) * ln_g_ref[...] + ln_b_ref[...]
    un = un * jax.nn.sigmoid(un)
    y_conv = _dot(un.astype(_BF16), w_pw_ref[...])
    mix = gate_ref[:, 0:D_MODEL] * mix_ref[...] + gate_ref[:, D_MODEL:] * y_conv

    o_ref[...] = x + _dot(mix.astype(_BF16), w_out_ref[...])


def _mixer(layer, x, g, w_in, w_pool, pscale, w_dw, b_dw, ln_g, ln_b, w_pw, w_out, job=None):
    B, S, D = x.shape
    ts = SEQ_TILE
    n_seq = S // ts
    shift_rows = ts + CONV_HALO - SUBLANES
    tile = pl.BlockSpec((None, ts, D), lambda b, s: (b, s, 0))
    params = (g, w_in, w_pool, pscale, w_dw, b_dw, ln_g, ln_b, w_pw, w_out)
    in_specs = [tile] + [_layer_spec(p, layer) for p in params]
    args = [x, *params]
    out_specs = [tile]
    out_shape = [jax.ShapeDtypeStruct(x.shape, x.dtype)]
    aliases, cast = {}, None
    if job is not None:
        assert B * n_seq == _cast_slices(job) // 2
        extra = _cast_plumbing(job, lambda b, s: b * n_seq + s, len(in_specs), len(out_specs))
        in_specs += extra[0]
        args += extra[1]
        out_specs += extra[2]
        out_shape += extra[3]
        aliases, cast = extra[4], job.gu_bf is not None
    out = pl.pallas_call(
        functools.partial(_mixer_kernel, cast=cast),
        grid=(B, n_seq),
        in_specs=in_specs,
        out_specs=out_specs,
        out_shape=out_shape,
        input_output_aliases=aliases,
        scratch_shapes=[pltpu.VMEM((POOL_HALO + ts, POOL_WIDTH), _F32),
                        pltpu.VMEM((CONV_WIDTH // LANES, CONV_HALO + ts, LANES), _F32),
                        pltpu.VMEM((SUBLANES - 1, CONV_WIDTH // LANES, shift_rows, LANES), _F32),
                        pltpu.VMEM((ts, CONV_WIDTH), _F32),
                        pltpu.VMEM((ts, D), _F32),
                        pltpu.VMEM((ts, 2 * D), _F32)],
        compiler_params=pltpu.CompilerParams(
            dimension_semantics=("parallel", "arbitrary"), vmem_limit_bytes=VMEM_LIMIT),
        name="mixer",
    )(*args)
    return out[0] if job is None else tuple(out)


def _kv_kernel(m_ref, g_ref, w_ref, k_ref, v_ref):
    m = _rms(m_ref[...], g_ref[...]).astype(_BF16)
    k_ref[...] = (_dot(m, w_ref[:, 0:D_MODEL]) * (XHEAD_DIM ** -0.5)).astype(_BF16)
    v_ref[...] = _dot(m, w_ref[:, D_MODEL:2 * D_MODEL]).astype(_BF16)


def _kv_proj(mem, g, w_kv):
    B, M, D = mem.shape
    L = w_kv.shape[0]
    rows = B * M
    out_blk = pl.BlockSpec((None, rows, D), lambda l: (l, 0, 0))
    k, v = pl.pallas_call(
        _kv_kernel,
        grid=(L,),
        in_specs=[pl.BlockSpec((rows, D), lambda l: (0, 0)),
                  pl.BlockSpec((None,) + g.shape[1:], lambda l: (l, 0, 0)),
                  pl.BlockSpec((None,) + w_kv.shape[1:], lambda l: (l, 0, 0))],
        out_specs=[out_blk, out_blk],
        out_shape=[jax.ShapeDtypeStruct((L, rows, D), _BF16)] * 2,
        compiler_params=pltpu.CompilerParams(
            dimension_semantics=("parallel",), vmem_limit_bytes=VMEM_LIMIT),
        name="kv_proj",
    )(mem.reshape(rows, D), g, w_kv)
    return k.reshape(L, B, M, D), v.reshape(L, B, M, D)


class _CastJob(NamedTuple):
    moe_layer: int
    first_slice: int
    w_gu: jax.Array
    w_down: jax.Array
    gu_bf: Optional[jax.Array]
    down_bf: Optional[jax.Array]


def _cast_slices(job):
    return job.w_gu.shape[1] * CAST_PARTS


def _cast_plumbing(job, step_of, n_in, n_out):
    _, _, d, f2 = job.w_gu.shape
    f = job.w_down.shape[2]

    def part(*grid_idx):
        q = jnp.minimum(job.first_slice + step_of(*grid_idx), _cast_slices(job) - 1)
        return q // CAST_PARTS, q % CAST_PARTS

    def src_idx(*grid_idx):
        return (job.moe_layer,) + part(*grid_idx) + (0,)

    def dst_idx(*grid_idx):
        return part(*grid_idx) + (0,)

    in_specs = [pl.BlockSpec((None, None, d // CAST_PARTS, f2), src_idx),
                pl.BlockSpec((None, None, f // CAST_PARTS, d), src_idx)]
    out_specs = [pl.BlockSpec((None, d // CAST_PARTS, f2), dst_idx),
                 pl.BlockSpec((None, f // CAST_PARTS, d), dst_idx)]
    args, aliases = [job.w_gu, job.w_down], {}
    if job.gu_bf is not None:
        in_specs += [pl.BlockSpec(memory_space=pl.ANY)] * 2
        args += [job.gu_bf, job.down_bf]
        aliases = {n_in + 2: n_out, n_in + 3: n_out + 1}
    out_shapes = [jax.ShapeDtypeStruct(job.w_gu.shape[1:], _BF16),
                  jax.ShapeDtypeStruct(job.w_down.shape[1:], _BF16)]
    return in_specs, args, out_specs, out_shapes, aliases


def _split_cast_refs(rest, n_out, cast_fill):
    n_cast_in = 4 if cast_fill else 2
    gu_in, down_in = rest[0], rest[1]
    host_out = rest[n_cast_in:n_cast_in + n_out]
    gu_out, down_out = rest[n_cast_in + n_out], rest[n_cast_in + n_out + 1]

    def cast():
        gu_out[...] = gu_in[...].astype(_BF16)
        down_out[...] = down_in[...].astype(_BF16)

    return tuple(host_out) + tuple(rest[n_cast_in + n_out + 2:]), cast


def _xattn_kernel(x_ref, g_ref, wq_ref, k_ref, v_ref, wo_ref, o_ref, att_ref):
    x = x_ref[...]
    h = _rms(x, g_ref[...]).astype(_BF16)
    q = _dot(h, wq_ref[...]).astype(_BF16)
    for hd in range(N_XHEADS):
        c0 = hd * XHEAD_DIM
        sc = lax.dot_general(q[:, c0:c0 + XHEAD_DIM], k_ref[:, c0:c0 + XHEAD_DIM],
                             (((1,), (1,)), ((), ())), preferred_element_type=_F32)
        p = jnp.exp(sc - jnp.max(sc, axis=-1, keepdims=True))
        denom = jnp.sum(p, axis=-1, keepdims=True)
        att = _dot(p.astype(_BF16), v_ref[:, c0:c0 + XHEAD_DIM]) / denom
        att_ref[:, c0:c0 + XHEAD_DIM] = att.astype(_BF16)
    o_ref[...] = x + _dot(att_ref[...], wo_ref[...])


def _xattn(layer, x, g, w_q, k, v, w_o):
    B, S, D = x.shape
    M = k.shape[2]
    ts = SEQ_TILE
    tile = pl.BlockSpec((None, ts, D), lambda b, s: (b, s, 0))
    kv_blk = pl.BlockSpec((None, None, M, D), lambda b, s: (layer, b, 0, 0))
    return pl.pallas_call(
        _xattn_kernel,
        grid=(B, S // ts),
        in_specs=[tile, _layer_spec(g, layer), _layer_spec(w_q, layer), kv_blk, kv_blk,
                  _layer_spec(w_o, layer)],
        out_specs=tile,
        out_shape=jax.ShapeDtypeStruct(x.shape, x.dtype),
        scratch_shapes=[pltpu.VMEM((ts, D), _BF16)],
        compiler_params=pltpu.CompilerParams(
            dimension_semantics=("parallel", "parallel"), vmem_limit_bytes=VMEM_LIMIT),
        name="xattn",
    )(x, g, w_q, k, v, w_o)


def _ffn_kernel(x_ref, g_ref, w_gu_ref, w_down_ref, *rest, cast):
    if cast is not None:
        rest, cast_slice = _split_cast_refs(rest, 1, cast)
        cast_slice()
    (o_ref,) = rest
    x = x_ref[...]
    h = _rms(x, g_ref[...]).astype(_BF16)
    acc = x
    for c0, n in FF_CHUNKS:
        gate = _dot(h, w_gu_ref[:, c0:c0 + n])
        up = _dot(h, w_gu_ref[:, D_FF + c0:D_FF + c0 + n])
        act = (gate * jax.nn.sigmoid(gate) * up).astype(_BF16)
        acc = acc + _dot(act, w_down_ref[c0:c0 + n, :])
    o_ref[...] = acc


def _ffn(layer, ffn_layer, x2, g, w_gu, w_down, job=None):
    T, D = x2.shape
    tm = TOK_TILE
    tile = pl.BlockSpec((tm, D), lambda i: (i, 0))
    in_specs = [tile, _layer_spec(g, layer), _layer_spec(w_gu, ffn_layer),
                _layer_spec(w_down, ffn_layer)]
    args = [x2, g, w_gu, w_down]
    out_specs = [tile]
    out_shape = [jax.ShapeDtypeStruct(x2.shape, x2.dtype)]
    aliases, cast = {}, None
    if job is not None:
        assert T // tm == (job.w_gu.shape[1] * CAST_PARTS) // 2
        extra = _cast_plumbing(job, lambda i: i, len(in_specs), len(out_specs))
        in_specs += extra[0]
        args += extra[1]
        out_specs += extra[2]
        out_shape += extra[3]
        aliases, cast = extra[4], job.gu_bf is not None
    out = pl.pallas_call(
        functools.partial(_ffn_kernel, cast=cast),
        grid=(T // tm,),
        in_specs=in_specs,
        out_specs=out_specs,
        out_shape=out_shape,
        input_output_aliases=aliases,
        compiler_params=pltpu.CompilerParams(
            dimension_semantics=("parallel",), vmem_limit_bytes=VMEM_LIMIT),
        name="ffn",
    )(*args)
    return out[0] if job is None else tuple(out)


def _pack_halves(y):
    half = y.shape[1] // 2
    lo = lax.bitcast_convert_type(y[:, :half].astype(_BF16).astype(_F32), _U32)
    hi = lax.bitcast_convert_type(y[:, half:].astype(_BF16).astype(_F32), _U32)
    return (lo >> 16) | (hi & HIGH_HALF_MASK)


def _unpack_halves(p):
    lo = lax.bitcast_convert_type(p << 16, _F32)
    hi = lax.bitcast_convert_type(p & HIGH_HALF_MASK, _F32)
    return lo, hi


def _route_kernel(x_ref, g_ref, w_r_ref, b_r_ref, earlier_ref, lane_before_ref,
                  meta_ref, cnt_ref, tab_ref, hs_ref,
                  carry_ref, placed_smem, second_smem, sorted_buf, sem,
                  *, region_rows, sort_rows):
    i = pl.program_id(0)
    tm = x_ref.shape[0]

    @pl.when(i == 0)
    def _():
        carry_ref[...] = jnp.zeros_like(carry_ref)
        for e in range(N_EXPERTS):
            placed_smem[e] = 0
        sorted_buf[:, sort_rows:, :] = jnp.zeros(
            (sorted_buf.shape[0], sorted_buf.shape[1] - sort_rows, sorted_buf.shape[2]), _U32)

    hf = _rms(x_ref[...], g_ref[...])

    lane = lax.broadcasted_iota(jnp.int32, (tm, LANES), 1)
    h_hi = hf.astype(_BF16)
    h_lo = (hf - h_hi.astype(_F32)).astype(_BF16)
    w_r = w_r_ref[...]
    w_hi = w_r.astype(_BF16)
    w_lo = (w_r - w_hi.astype(_F32)).astype(_BF16)
    hi_terms = _dot(h_hi, jnp.concatenate([w_hi, w_lo], axis=1))
    logits = hi_terms[:, :LANES] + hi_terms[:, LANES:] + _dot(h_lo, w_hi) + b_r_ref[...]
    v1 = jnp.max(logits, axis=-1, keepdims=True)
    i1 = jnp.min(jnp.where(logits == v1, lane, LANES), axis=-1, keepdims=True)
    rest = jnp.where(lane == i1, NEG_BIG, logits)
    v2 = jnp.max(rest, axis=-1, keepdims=True)
    i2 = jnp.min(jnp.where(rest == v2, lane, LANES), axis=-1, keepdims=True)
    z = jnp.exp(v2 - v1)
    g1 = 1.0 / (1.0 + z)
    g2 = z / (1.0 + z)

    sel1 = lane == i1
    sel2 = lane == i2
    chosen = jnp.where(sel1 | sel2, 1.0, 0.0)
    before = _dot(earlier_ref[...], chosen.astype(_BF16))
    cnt = jnp.sum(chosen, axis=0, keepdims=True)
    cnt_pad = jnp.ceil(cnt * (1.0 / SUBLANES)) * SUBLANES
    group_off = _dot(jnp.broadcast_to(cnt_pad, (SUBLANES, LANES)).astype(_BF16),
                     lane_before_ref[...])[0:1, :]
    local = group_off + before
    loc1 = jnp.sum(jnp.where(sel1, local, 0.0), axis=-1, keepdims=True)
    loc2 = jnp.sum(jnp.where(sel2, local, 0.0), axis=-1, keepdims=True)

    slot = lax.rem(i, 2)
    loc_t = jnp.where(lane == 0, loc1, jnp.where(lane == 1, loc2, -1.0)).T
    row = lax.broadcasted_iota(jnp.int32, (sort_rows, tm), 0).astype(_F32)
    perm = jnp.where((row == loc_t[0:1, :]) | (row == loc_t[1:2, :]), 1.0, 0.0).astype(_BF16)
    sorted_buf[slot, 0:sort_rows, :] = _pack_halves(_dot(perm, h_hi))

    def window(src, dst):
        return pltpu.make_async_copy(sorted_buf.at[slot, pl.ds(src, GROUP_WINDOW)],
                                     hs_ref.at[pl.ds(dst, GROUP_WINDOW)], sem)

    def wait_windows(second_flags):
        for e in range(N_EXPERTS):
            window(0, 0).wait()

            @pl.when(second_flags(e) != 0)
            def _():
                window(0, 0).wait()

    @pl.when(i > 0)
    def _():
        wait_windows(lambda e: second_smem[e])

    off_i = group_off.astype(jnp.int32)
    cnt_i = cnt_pad.astype(jnp.int32)
    for e in range(N_EXPERTS):
        src = pl.multiple_of(off_i[0, e], SUBLANES)
        start = pl.multiple_of(e * region_rows + placed_smem[e], SUBLANES)
        rows = cnt_i[0, e]
        second = (rows > GROUP_WINDOW).astype(jnp.int32)
        window(src, start).start()

        @pl.when(second != 0)
        def _():
            window(src + GROUP_WINDOW, start + GROUP_WINDOW).start()

        second_smem[e] = second
        placed_smem[e] = placed_smem[e] + rows
        tab_ref[0, e] = start
        tab_ref[0, N_EXPERTS + e] = rows
    carry_ref[...] += cnt_pad
    cnt_ref[...] = carry_ref[...]

    @pl.when(i == pl.num_programs(0) - 1)
    def _():
        wait_windows(lambda e: second_smem[e])

        def tail(e, part):
            dst = pl.multiple_of(e * region_rows + placed_smem[e], SUBLANES) + part * GROUP_WINDOW
            return window(sort_rows, dst)

        for e in range(N_EXPERTS):
            for part in range(EXPERT_ROW_TILE // GROUP_WINDOW):
                tail(e, part).start()
        for e in range(N_EXPERTS):
            for part in range(EXPERT_ROW_TILE // GROUP_WINDOW):
                tail(e, part).wait()

    meta = jnp.where(lane == META_G1, g1, 0.0)
    meta = jnp.where(lane == META_G2, g2, meta)
    meta = jnp.where(lane == META_L1, loc1, meta)
    meta = jnp.where(lane == META_L2, loc2, meta)
    meta_ref[...] = meta


def _region_rows(n_tokens):
    pad = (n_tokens // ROUTE_TILE) * (SUBLANES - 1) + 2 * EXPERT_ROW_TILE
    return -(-(n_tokens + pad) // EXPERT_ROW_TILE) * EXPERT_ROW_TILE


def _route(layer, moe_layer, x2, g, w_r, b_r):
    T, D = x2.shape
    tm = ROUTE_TILE
    region_rows = _region_rows(T)
    sort_rows = TOP_K * tm + N_EXPERTS * SUBLANES
    earlier = jnp.asarray(np.tril(np.ones((tm, tm), np.float32), -1), _BF16)
    lane_before = jnp.asarray(np.triu(np.ones((LANES, LANES), np.float32), 1), _BF16)
    const = lambda n: pl.BlockSpec((n, n), lambda i: (0, 0), pipeline_mode=pl.Buffered(1))
    return pl.pallas_call(
        functools.partial(_route_kernel, region_rows=region_rows, sort_rows=sort_rows),
        grid=(T // tm,),
        in_specs=[pl.BlockSpec((tm, D), lambda i: (i, 0)), _layer_spec(g, layer),
                  _layer_spec(w_r, moe_layer), _layer_spec(b_r, moe_layer),
                  const(tm), const(LANES)],
        out_specs=[pl.BlockSpec((tm, LANES), lambda i: (i, 0)),
                   pl.BlockSpec((1, LANES), lambda i: (0, 0)),
                   pl.BlockSpec((None, 1, 2 * N_EXPERTS), lambda i: (i, 0, 0),
                                memory_space=pltpu.SMEM),
                   pl.BlockSpec(memory_space=pl.ANY)],
        out_shape=[jax.ShapeDtypeStruct((T, LANES), _F32),
                   jax.ShapeDtypeStruct((1, LANES), _F32),
                   jax.ShapeDtypeStruct((T // tm, 1, 2 * N_EXPERTS), jnp.int32),
                   jax.ShapeDtypeStruct((N_EXPERTS * region_rows, D // 2), _U32)],
        scratch_shapes=[pltpu.VMEM((1, LANES), _F32),
                        pltpu.SMEM((N_EXPERTS,), jnp.int32),
                        pltpu.SMEM((N_EXPERTS,), jnp.int32),
                        pltpu.VMEM((2, sort_rows + GROUP_WINDOW, D // 2), _U32),
                        pltpu.SemaphoreType.DMA],
        compiler_params=pltpu.CompilerParams(
            dimension_semantics=("arbitrary",), vmem_limit_bytes=VMEM_LIMIT),
        name="moe_route",
    )(x2, g, w_r, b_r, earlier, lane_before)


def _expert_kernel(te_ref, tb_ref, tv_ref, hs_ref, w_gu_ref, w_down_ref, ys_ref):
    del te_ref, tb_ref
    i = pl.program_id(0)

    rt = ys_ref.shape[0]
    half = rt // 2
    rows_used = tv_ref[i]

    def swiglu_rows(n_rows):
        lo, hi = _unpack_halves(hs_ref[0:n_rows, :])
        h = jnp.concatenate([lo.astype(_BF16), hi.astype(_BF16)], axis=1)
        acc = None
        for c0, n in FF_CHUNKS:
            gate = _dot(h, w_gu_ref[:, c0:c0 + n])
            up = _dot(h, w_gu_ref[:, D_FF + c0:D_FF + c0 + n])
            act = (gate * jax.nn.sigmoid(gate) * up).astype(_BF16)
            part = _dot(act, w_down_ref[c0:c0 + n, :])
            acc = part if acc is None else acc + part
        ys_ref[0:n_rows, :] = _pack_halves(acc)

    @pl.when(rows_used == 0)
    def _():
        ys_ref[...] = jnp.zeros_like(ys_ref)

    @pl.when((rows_used > 0) & (rows_used <= half))
    def _():
        swiglu_rows(half)
        ys_ref[half:, :] = jnp.zeros((rt - half, ys_ref.shape[1]), ys_ref.dtype)

    @pl.when(rows_used > half)
    def _():
        swiglu_rows(rt)


def _experts(tile_expert, tile_block, tile_rows, hs, w_gu, w_down):
    n_rows, W = hs.shape
    rt = EXPERT_ROW_TILE
    D = w_gu.shape[1]
    grid_spec = pltpu.PrefetchScalarGridSpec(
        num_scalar_prefetch=3,
        grid=(tile_expert.shape[0],),
        in_specs=[pl.BlockSpec((rt, W), lambda i, te, tb, tv: (jnp.where(tv[i] > 0, tb[i], 0), 0)),
                  pl.BlockSpec((None, D, 2 * D_FF), lambda i, te, tb, tv: (te[i], 0, 0)),
                  pl.BlockSpec((None, D_FF, D), lambda i, te, tb, tv: (te[i], 0, 0))],
        out_specs=pl.BlockSpec((rt, W), lambda i, te, tb, tv: (tb[i], 0)),
    )
    return pl.pallas_call(
        _expert_kernel,
        grid_spec=grid_spec,
        out_shape=jax.ShapeDtypeStruct((n_rows + rt, W), hs.dtype),
        compiler_params=pltpu.CompilerParams(
            dimension_semantics=("arbitrary",), vmem_limit_bytes=VMEM_LIMIT),
        name="moe_experts",
    )(tile_expert, tile_block, tile_rows, hs, w_gu, w_down)


def _combine_kernel(tab_ref, x_ref, meta_ref, g_final_ref, ys_ref, o_ref, ybuf, sem, *, final_norm):
    i = pl.program_id(0)
    n = pl.num_programs(0)
    tm, D = x_ref.shape
    slot = lax.rem(i, 2)

    @pl.when(i == 0)
    def _():
        ybuf[...] = jnp.zeros_like(ybuf)

    def fetch(tile, to_slot):
        off = 0
        for e in range(N_EXPERTS):
            start = tab_ref[tile, e]
            rows = tab_ref[tile, N_EXPERTS + e]
            size = ROUTE_TILE
            while size >= SUBLANES:
                above = rows & ~(2 * size - 1)
                src = pl.multiple_of(start + above, SUBLANES)
                dst = pl.multiple_of(off + above, SUBLANES)

                @pl.when((rows & size) != 0)
                def _():
                    pltpu.make_async_copy(ys_ref.at[pl.ds(src, size)],
                                          ybuf.at[to_slot, pl.ds(dst, size)], sem.at[to_slot]).start()
                size //= 2
            off = off + rows
        return off

    def total_rows(tile):
        total = 0
        for e in range(N_EXPERTS):
            total = total + tab_ref[tile, N_EXPERTS + e]
        return total

    @pl.when(i == 0)
    def _():
        fetch(0, 0)

    @pl.when(i + 1 < n)
    def _():
        fetch(i + 1, 1 - slot)

    landed = pl.multiple_of(total_rows(i), SUBLANES)
    pltpu.make_async_copy(ys_ref.at[pl.ds(0, landed)], ybuf.at[slot, pl.ds(0, landed)],
                          sem.at[slot]).wait()

    lo, hi = _unpack_halves(ybuf[slot])
    y = jnp.concatenate([lo.astype(_BF16), hi.astype(_BF16)], axis=1)
    col = lax.broadcasted_iota(jnp.int32, (tm, ybuf.shape[1]), 1).astype(_F32)
    pick1 = jnp.where(col == meta_ref[:, META_L1:META_L1 + 1], 1.0, 0.0).astype(_BF16)
    pick2 = jnp.where(col == meta_ref[:, META_L2:META_L2 + 1], 1.0, 0.0).astype(_BF16)
    out = (x_ref[...] + meta_ref[:, META_G1:META_G1 + 1] * _dot(pick1, y)
           + meta_ref[:, META_G2:META_G2 + 1] * _dot(pick2, y))
    if final_norm:
        out = _rms(out, g_final_ref[...])
    o_ref[...] = out


def _combine(tab, x2, meta, ys, g_final, final_norm):
    T, D = x2.shape
    tm = ROUTE_TILE
    sort_rows = -(-(TOP_K * tm + N_EXPERTS * SUBLANES) // MXU_DIM) * MXU_DIM
    tile = pl.BlockSpec((tm, D), lambda i, tab: (i, 0))
    grid_spec = pltpu.PrefetchScalarGridSpec(
        num_scalar_prefetch=1,
        grid=(T // tm,),
        in_specs=[tile, pl.BlockSpec((tm, LANES), lambda i, tab: (i, 0)),
                  pl.BlockSpec((1, D), lambda i, tab: (0, 0)), pl.BlockSpec(memory_space=pl.ANY)],
        out_specs=tile,
        scratch_shapes=[pltpu.VMEM((2, sort_rows, D // 2), _U32), pltpu.SemaphoreType.DMA((2,))],
    )
    return pl.pallas_call(
        functools.partial(_combine_kernel, final_norm=final_norm),
        grid_spec=grid_spec,
        out_shape=jax.ShapeDtypeStruct(x2.shape, x2.dtype),
        compiler_params=pltpu.CompilerParams(
            dimension_semantics=("arbitrary",), vmem_limit_bytes=VMEM_LIMIT),
        name="moe_combine",
    )(tab, x2, meta, g_final, ys)


def _moe(layer, moe_layer, x2, g, w_r, b_r, w_gu, w_down, g_final, final_norm):
    T, D = x2.shape
    E = w_gu.shape[0]
    rt = EXPERT_ROW_TILE
    meta, counts, tab, hs = _route(layer, moe_layer, x2, g, w_r, b_r)

    region_tiles = hs.shape[0] // (E * rt)
    max_rows = TOP_K * T + (T // ROUTE_TILE) * E * (SUBLANES - 1)
    n_tiles = -(-max_rows // rt) + E
    counts = counts[0, :E].astype(jnp.int32)
    group_tiles = (counts + rt - 1) // rt
    tile_end = jnp.cumsum(group_tiles)
    tile_ids = jnp.arange(n_tiles, dtype=jnp.int32)
    tile_expert = jnp.minimum(
        jnp.sum((tile_ids[:, None] >= tile_end[None, :]).astype(jnp.int32), axis=1), E - 1)
    tile_valid = (tile_ids < tile_end[E - 1]).astype(jnp.int32)
    in_group = tile_ids - (tile_end - group_tiles)[tile_expert]
    tile_block = jnp.where(tile_valid != 0, tile_expert * region_tiles + in_group, E * region_tiles)

    tile_rows = jnp.clip(counts[tile_expert] - in_group * rt, 0, rt) * tile_valid
    ys = _experts(tile_expert, tile_block.astype(jnp.int32), tile_rows.astype(jnp.int32), hs, w_gu, w_down)
    return _combine(tab.reshape(tab.shape[0], tab.shape[2]), x2, meta, ys, g_final, final_norm)


def _final_norm_kernel(x_ref, g_ref, o_ref):
    o_ref[...] = _rms(x_ref[...], g_ref[...])


def _final_norm(x2, g):
    T, D = x2.shape
    tm = TOK_TILE
    tile = pl.BlockSpec((tm, D), lambda i: (i, 0))
    return pl.pallas_call(
        _final_norm_kernel,
        grid=(T // tm,),
        in_specs=[tile, pl.BlockSpec((1, D), lambda i: (0, 0))],
        out_specs=tile,
        out_shape=jax.ShapeDtypeStruct(x2.shape, x2.dtype),
        compiler_params=pltpu.CompilerParams(dimension_semantics=("parallel",)),
        name="final_norm",
    )(x2, g)


def kernel(x, mem, g_mix, w_in, w_pool, pool_scale, w_dw, b_dw, conv_ln_g, conv_ln_b, w_conv_out, w_mix_out, g_xattn, g_mem, w_xq, w_xkv, w_xo, g_ffn, w_ffn_gu, w_ffn_down, w_router, b_router, w_moe_gu, w_moe_down, g_final):
    B, S, D = x.shape
    depth = g_mix.shape[0]
    bf = lambda a: a.astype(_BF16)
    rows = lambda a: a.reshape(a.shape[0], 1, a.shape[1])

    g_mix, pool_scale, b_dw, conv_ln_g, conv_ln_b = map(rows, (g_mix, pool_scale, b_dw, conv_ln_g, conv_ln_b))
    g_xattn, g_mem, g_ffn = map(rows, (g_xattn, g_mem, g_ffn))
    w_dw = w_dw.reshape(depth, CONV_KERNEL, CONV_WIDTH)
    w_in, w_pool, w_conv_out, w_mix_out = map(bf, (w_in, w_pool, w_conv_out, w_mix_out))
    w_xq, w_xkv, w_xo = map(bf, (w_xq, w_xkv, w_xo))
    w_ffn_gu, w_ffn_down = map(bf, (w_ffn_gu, w_ffn_down))
    w_r_pad = jnp.pad(w_router, ((0, 0), (0, 0), (0, LANES - N_EXPERTS)))
    b_r_pad = rows(jnp.pad(b_router, ((0, 0), (0, LANES - N_EXPERTS)), constant_values=NEG_BIG))
    g_final = g_final.reshape(1, D)

    half_slices = (w_moe_gu.shape[1] * CAST_PARTS) // 2
    k_all, v_all = _kv_proj(mem, g_mem, w_xkv)
    for i in range(depth):
        last = i == depth - 1
        mixer_args = (i, x, g_mix, w_in, w_pool, pool_scale, w_dw, b_dw, conv_ln_g, conv_ln_b,
                      w_conv_out, w_mix_out)
        if i % 2 == 0 and not last:
            job = lambda first, gu, down: _CastJob((i + 1) // 2, first, w_moe_gu, w_moe_down, gu, down)
            x, moe_gu_bf, moe_down_bf = _mixer(*mixer_args, job(0, None, None))
            x = _xattn(i, x, g_xattn, w_xq, k_all, v_all, w_xo)
            x2, moe_gu_bf, moe_down_bf = _ffn(i, i // 2, x.reshape(B * S, D), g_ffn, w_ffn_gu, w_ffn_down,
                                              job(half_slices, moe_gu_bf, moe_down_bf))
        else:
            x = _mixer(*mixer_args)
            x = _xattn(i, x, g_xattn, w_xq, k_all, v_all, w_xo)
            if i % 2 == 0:
                x2 = _final_norm(_ffn(i, i // 2, x.reshape(B * S, D), g_ffn, w_ffn_gu, w_ffn_down), g_final)
            else:
                x2 = _moe(i, i // 2, x.reshape(B * S, D), g_ffn, w_r_pad, b_r_pad, moe_gu_bf, moe_down_bf,
                          g_final, last)
        x = x2.reshape(B, S, D)
    return x
```

```python
import functools
from typing import NamedTuple, Optional

import jax
import jax.numpy as jnp
import numpy as np
from jax import lax
from jax.experimental import pallas as pl
from jax.experimental.pallas import tpu as pltpu

D_MODEL = 1024
POOL_WIDTH = 512
POOL_GROUPS = 4
POOL_WINDOWS = (2, 4, 8, 16)
POOL_GROUP_IN = POOL_WIDTH // POOL_GROUPS
POOL_GROUP_OUT = D_MODEL // POOL_GROUPS
CONV_WIDTH = 512
CONV_KERNEL = 31
IN_COLS = POOL_WIDTH + 2 * CONV_WIDTH + 2 * D_MODEL
N_XHEADS = 4
XHEAD_DIM = D_MODEL // N_XHEADS
D_FF = 2816
N_EXPERTS = 8
TOP_K = 2
EPS = 1e-6

COL_POOL = 0
COL_GLU_A = POOL_WIDTH
COL_GLU_B = POOL_WIDTH + CONV_WIDTH
COL_GATE_POOL = POOL_WIDTH + 2 * CONV_WIDTH
COL_GATE_CONV = COL_GATE_POOL + D_MODEL

LANES = 128
SUBLANES = 8
POOL_HALO = 16
CONV_HALO = 32
SEQ_TILE = 512
TOK_TILE = 512
MXU_DIM = 256
FF_CHUNKS = ((0, 6 * MXU_DIM), (6 * MXU_DIM, 5 * MXU_DIM))
assert sum(n for _, n in FF_CHUNKS) == D_FF
CONV_ROW_CHUNK = 64
ROUTE_TILE = 512
EXPERT_ROW_TILE = 512
GROUP_WINDOW = 256
CAST_PARTS = 8
VMEM_LIMIT = 56 * 1024 * 1024
NEG_BIG = -1e30
META_G1, META_G2, META_L1, META_L2 = range(4)

_F32 = jnp.float32
_BF16 = jnp.bfloat16
_U32 = jnp.uint32
HIGH_HALF_MASK = np.uint32(0xFFFF0000)


def _layer_spec(arr, layer):
    tail = (0,) * (arr.ndim - 1)
    return pl.BlockSpec((None,) + arr.shape[1:], lambda *_: (layer,) + tail,
                        pipeline_mode=pl.Buffered(1))


def _rms(x, g):
    return x * lax.rsqrt(jnp.mean(x * x, axis=-1, keepdims=True) + EPS) * g


def _dot(a, b):
    return jnp.dot(a, b, preferred_element_type=_F32)


def _zero_like_bits(v):
    u = lax.bitcast_convert_type(v, _U32)
    return lax.bitcast_convert_type((u >> 16) >> 16, _F32)


def _mixer_kernel(x_ref, g_ref, w_in_ref, w_pool_ref, pscale_ref, w_dw_ref, b_dw_ref,
                  ln_g_ref, ln_b_ref, w_pw_ref, w_out_ref, *rest, cast):
    if cast is not None:
        rest, cast_slice = _split_cast_refs(rest, 1, cast)
        cast_slice()
    o_ref, up_ext, cv_ext, cv_shift, cv_out, mix_ref, gate_ref, tap_w = rest
    s = pl.program_id(1)
    ts = x_ref.shape[0]

    @pl.when(s == 0)
    def _():
        up_ext[0:POOL_HALO, :] = jnp.zeros((POOL_HALO, POOL_WIDTH), _F32)
        cv_ext[:, 0:CONV_HALO, :] = jnp.zeros((cv_ext.shape[0], CONV_HALO, LANES), _F32)

    @pl.when(s > 0)
    def _():
        up_ext[0:POOL_HALO, :] = up_ext[ts:ts + POOL_HALO, :]
        cv_ext[:, 0:CONV_HALO, :] = cv_ext[:, ts:ts + CONV_HALO, :]

    x = x_ref[...]
    h = _rms(x, g_ref[...]).astype(_BF16)

    glu_a = _dot(h, w_in_ref[:, COL_GLU_A:COL_GLU_A + CONV_WIDTH])
    glu_b = _dot(h, w_in_ref[:, COL_GLU_B:COL_GLU_B + CONV_WIDTH])
    glu = glu_a * jax.nn.sigmoid(glu_b)
    for blk in range(cv_ext.shape[0]):
        cv_ext[blk, CONV_HALO:CONV_HALO + ts, :] = glu[:, blk * LANES:(blk + 1) * LANES]

    shift_rows = cv_shift.shape[2]
    for r in range(1, SUBLANES):
        cv_shift[r - 1] = cv_ext[:, r:r + shift_rows, :]

    tap0 = CONV_HALO - (CONV_KERNEL - 1)
    def side_task(col0, store):
        res = _dot(h, w_in_ref[:, col0:col0 + MXU_DIM])
        store(res)
        return _zero_like_bits(res[ts - 1:ts, MXU_DIM - LANES:])

    def pool_store(c):
        def store(res):
            up_ext[POOL_HALO:POOL_HALO + ts, c:c + MXU_DIM] = res
        return store

    def gate_store(c):
        def store(res):
            gate_ref[:, c:c + MXU_DIM] = jax.nn.sigmoid(res)
        return store

    side_tasks = ([(COL_POOL + c, pool_store(c)) for c in range(0, POOL_WIDTH, MXU_DIM)]
                  + [(COL_GATE_POOL + c, gate_store(c)) for c in range(0, 2 * D_MODEL, MXU_DIM)])
    unit_rows = 2 * CONV_ROW_CHUNK
    units = [(c0, q0) for c0 in range(0, CONV_WIDTH, LANES) for q0 in range(0, ts, unit_rows)]
    assert len(side_tasks) <= len(units)
    pace = None
    chunk_done = None
    n_chunk = 0
    for u, (c0, q0) in enumerate(units):
        bias = b_dw_ref[:, c0:c0 + LANES]
        if pace is not None:
            bias = bias + pace
        pace = side_task(*side_tasks[u]) if u < len(side_tasks) else None
        for r0 in range(q0, q0 + unit_rows, CONV_ROW_CHUNK):
            slot = n_chunk % 2
            n_chunk += 1
            taps = w_dw_ref[:, c0:c0 + LANES]
            if chunk_done is not None:
                taps = taps + chunk_done
            tap_w[slot, 0:CONV_KERNEL, :] = taps
            acc = jnp.broadcast_to(bias, (CONV_ROW_CHUNK, LANES))
            for k in range(CONV_KERNEL):
                off = tap0 + k
                mis = off % SUBLANES
                row = r0 + off - mis
                if mis == 0:
                    win = cv_ext[c0 // LANES, row:row + CONV_ROW_CHUNK, :]
                else:
                    win = cv_shift[mis - 1, c0 // LANES, row:row + CONV_ROW_CHUNK, :]
                acc = acc + tap_w[slot, k:k + 1, :] * win
            cv_out[r0:r0 + CONV_ROW_CHUNK, c0:c0 + LANES] = acc
            chunk_done = _zero_like_bits(acc[CONV_ROW_CHUNK - 1:, :])

    pos = (s * ts + 1 + lax.broadcasted_iota(jnp.int32, (ts, 1), 0)).astype(_F32)
    for g, w in enumerate(POOL_WINDOWS):
        c0 = g * POOL_GROUP_IN
        win = up_ext[:, c0:c0 + POOL_GROUP_IN]
        span = 1
        while span < w:
            win = win + pltpu.roll(win, span, 0)
            span *= 2
        cur = up_ext[POOL_HALO:POOL_HALO + ts, c0:c0 + POOL_GROUP_IN]
        pooled = win[POOL_HALO:, :] / jnp.minimum(pos, float(w)) - cur
        d0 = g * POOL_GROUP_OUT
        mix_ref[:, d0:d0 + POOL_GROUP_OUT] = (
            _dot(pooled.astype(_BF16), w_pool_ref[g]) * pscale_ref[:, d0:d0 + POOL_GROUP_OUT])

    u = cv_out[...]
    mu = jnp.mean(u, axis=-1, keepdims=True)
    uc = u - mu
    var = jnp.mean(uc * uc, axis=-1, keepdims=True)
    un = uc * lax.rsqrt(var + EPS) * ln_g_ref[...] + ln_b_ref[...]
    un = un * jax.nn.sigmoid(un)
    y_conv = _dot(un.astype(_BF16), w_pw_ref[...])
    mix = gate_ref[:, 0:D_MODEL] * mix_ref[...] + gate_ref[:, D_MODEL:] * y_conv

    o_ref[...] = x + _dot(mix.astype(_BF16), w_out_ref[...])


def _mixer(layer, x, g, w_in, w_pool, pscale, w_dw, b_dw, ln_g, ln_b, w_pw, w_out, job=None):
    B, S, D = x.shape
    ts = SEQ_TILE
    n_seq = S // ts
    shift_rows = ts + CONV_HALO - SUBLANES
    tile = pl.BlockSpec((None, ts, D), lambda b, s: (b, s, 0))
    params = (g, w_in, w_pool, pscale, w_dw, b_dw, ln_g, ln_b, w_pw, w_out)
    in_specs = [tile] + [_layer_spec(p, layer) for p in params]
    args = [x, *params]
    out_specs = [tile]
    out_shape = [jax.ShapeDtypeStruct(x.shape, x.dtype)]
    aliases, cast = {}, None
    if job is not None:
        assert B * n_seq == _cast_slices(job) // 2
        extra = _cast_plumbing(job, lambda b, s: b * n_seq + s, len(in_specs), len(out_specs))
        in_specs += extra[0]
        args += extra[1]
        out_specs += extra[2]
        out_shape += extra[3]
        aliases, cast = extra[4], job.gu_bf is not None
    out = pl.pallas_call(
        functools.partial(_mixer_kernel, cast=cast),
        grid=(B, n_seq),
        in_specs=in_specs,
        out_specs=out_specs,
        out_shape=out_shape,
        input_output_aliases=aliases,
        scratch_shapes=[pltpu.VMEM((POOL_HALO + ts, POOL_WIDTH), _F32),
                        pltpu.VMEM((CONV_WIDTH // LANES, CONV_HALO + ts, LANES), _F32),
                        pltpu.VMEM((SUBLANES - 1, CONV_WIDTH // LANES, shift_rows, LANES), _F32),
                        pltpu.VMEM((ts, CONV_WIDTH), _F32),
                        pltpu.VMEM((ts, D), _F32),
                        pltpu.VMEM((ts, 2 * D), _F32),
                        pltpu.VMEM((2, 4 * SUBLANES, LANES), _F32)],
        compiler_params=pltpu.CompilerParams(
            dimension_semantics=("parallel", "arbitrary"), vmem_limit_bytes=VMEM_LIMIT),
        name="mixer",
    )(*args)
    return out[0] if job is None else tuple(out)


def _kv_kernel(m_ref, g_ref, w_ref, k_ref, v_ref):
    m = _rms(m_ref[...], g_ref[...]).astype(_BF16)
    k_ref[...] = (_dot(m, w_ref[:, 0:D_MODEL]) * (XHEAD_DIM ** -0.5)).astype(_BF16)
    v_ref[...] = _dot(m, w_ref[:, D_MODEL:2 * D_MODEL]).astype(_BF16)


def _kv_proj(mem, g, w_kv):
    B, M, D = mem.shape
    L = w_kv.shape[0]
    rows = B * M
    out_blk = pl.BlockSpec((None, rows, D), lambda l: (l, 0, 0))
    k, v = pl.pallas_call(
        _kv_kernel,
        grid=(L,),
        in_specs=[pl.BlockSpec((rows, D), lambda l: (0, 0)),
                  pl.BlockSpec((None,) + g.shape[1:], lambda l: (l, 0, 0)),
                  pl.BlockSpec((None,) + w_kv.shape[1:], lambda l: (l, 0, 0))],
        out_specs=[out_blk, out_blk],
        out_shape=[jax.ShapeDtypeStruct((L, rows, D), _BF16)] * 2,
        compiler_params=pltpu.CompilerParams(
            dimension_semantics=("parallel",), vmem_limit_bytes=VMEM_LIMIT),
        name="kv_proj",
    )(mem.reshape(rows, D), g, w_kv)
    return k.reshape(L, B, M, D), v.reshape(L, B, M, D)


class _CastJob(NamedTuple):
    moe_layer: int
    first_slice: int
    w_gu: jax.Array
    w_down: jax.Array
    gu_bf: Optional[jax.Array]
    down_bf: Optional[jax.Array]


def _cast_slices(job):
    return job.w_gu.shape[1] * CAST_PARTS


def _cast_plumbing(job, step_of, n_in, n_out):
    _, _, d, f2 = job.w_gu.shape
    f = job.w_down.shape[2]

    def part(*grid_idx):
        q = jnp.minimum(job.first_slice + step_of(*grid_idx), _cast_slices(job) - 1)
        return q // CAST_PARTS, q % CAST_PARTS

    def src_idx(*grid_idx):
        return (job.moe_layer,) + part(*grid_idx) + (0,)

    def dst_idx(*grid_idx):
        return part(*grid_idx) + (0,)

    in_specs = [pl.BlockSpec((None, None, d // CAST_PARTS, f2), src_idx),
                pl.BlockSpec((None, None, f // CAST_PARTS, d), src_idx)]
    out_specs = [pl.BlockSpec((None, d // CAST_PARTS, f2), dst_idx),
                 pl.BlockSpec((None, f // CAST_PARTS, d), dst_idx)]
    args, aliases = [job.w_gu, job.w_down], {}
    if job.gu_bf is not None:
        in_specs += [pl.BlockSpec(memory_space=pl.ANY)] * 2
        args += [job.gu_bf, job.down_bf]
        aliases = {n_in + 2: n_out, n_in + 3: n_out + 1}
    out_shapes = [jax.ShapeDtypeStruct(job.w_gu.shape[1:], _BF16),
                  jax.ShapeDtypeStruct(job.w_down.shape[1:], _BF16)]
    return in_specs, args, out_specs, out_shapes, aliases


def _split_cast_refs(rest, n_out, cast_fill):
    n_cast_in = 4 if cast_fill else 2
    gu_in, down_in = rest[0], rest[1]
    host_out = rest[n_cast_in:n_cast_in + n_out]
    gu_out, down_out = rest[n_cast_in + n_out], rest[n_cast_in + n_out + 1]

    def cast():
        gu_out[...] = gu_in[...].astype(_BF16)
        down_out[...] = down_in[...].astype(_BF16)

    return tuple(host_out) + tuple(rest[n_cast_in + n_out + 2:]), cast


def _xattn_kernel(x_ref, g_ref, wq_ref, k_ref, v_ref, wo_ref, o_ref, att_ref):
    x = x_ref[...]
    h = _rms(x, g_ref[...]).astype(_BF16)
    q = _dot(h, wq_ref[...]).astype(_BF16)
    for hd in range(N_XHEADS):
        c0 = hd * XHEAD_DIM
        sc = lax.dot_general(q[:, c0:c0 + XHEAD_DIM], k_ref[:, c0:c0 + XHEAD_DIM],
                             (((1,), (1,)), ((), ())), preferred_element_type=_F32)
        p = jnp.exp(sc - jnp.max(sc, axis=-1, keepdims=True))
        denom = jnp.sum(p, axis=-1, keepdims=True)
        att = _dot(p.astype(_BF16), v_ref[:, c0:c0 + XHEAD_DIM]) / denom
        att_ref[:, c0:c0 + XHEAD_DIM] = att.astype(_BF16)
    o_ref[...] = x + _dot(att_ref[...], wo_ref[...])


def _xattn(layer, x, g, w_q, k, v, w_o):
    B, S, D = x.shape
    M = k.shape[2]
    ts = SEQ_TILE
    tile = pl.BlockSpec((None, ts, D), lambda b, s: (b, s, 0))
    kv_blk = pl.BlockSpec((None, None, M, D), lambda b, s: (layer, b, 0, 0))
    return pl.pallas_call(
        _xattn_kernel,
        grid=(B, S // ts),
        in_specs=[tile, _layer_spec(g, layer), _layer_spec(w_q, layer), kv_blk, kv_blk,
                  _layer_spec(w_o, layer)],
        out_specs=tile,
        out_shape=jax.ShapeDtypeStruct(x.shape, x.dtype),
        scratch_shapes=[pltpu.VMEM((ts, D), _BF16)],
        compiler_params=pltpu.CompilerParams(
            dimension_semantics=("parallel", "parallel"), vmem_limit_bytes=VMEM_LIMIT),
        name="xattn",
    )(x, g, w_q, k, v, w_o)


def _ffn_kernel(x_ref, g_ref, w_gu_ref, w_down_ref, *rest, cast):
    if cast is not None:
        rest, cast_slice = _split_cast_refs(rest, 1, cast)
        cast_slice()
    (o_ref,) = rest
    x = x_ref[...]
    h = _rms(x, g_ref[...]).astype(_BF16)
    acc = x
    for c0, n in FF_CHUNKS:
        gate = _dot(h, w_gu_ref[:, c0:c0 + n])
        up = _dot(h, w_gu_ref[:, D_FF + c0:D_FF + c0 + n])
        act = (gate * jax.nn.sigmoid(gate) * up).astype(_BF16)
        acc = acc + _dot(act, w_down_ref[c0:c0 + n, :])
    o_ref[...] = acc


def _ffn(layer, ffn_layer, x2, g, w_gu, w_down, job=None):
    T, D = x2.shape
    tm = TOK_TILE
    tile = pl.BlockSpec((tm, D), lambda i: (i, 0))
    in_specs = [tile, _layer_spec(g, layer), _layer_spec(w_gu, ffn_layer),
                _layer_spec(w_down, ffn_layer)]
    args = [x2, g, w_gu, w_down]
    out_specs = [tile]
    out_shape = [jax.ShapeDtypeStruct(x2.shape, x2.dtype)]
    aliases, cast = {}, None
    if job is not None:
        assert T // tm == (job.w_gu.shape[1] * CAST_PARTS) // 2
        extra = _cast_plumbing(job, lambda i: i, len(in_specs), len(out_specs))
        in_specs += extra[0]
        args += extra[1]
        out_specs += extra[2]
        out_shape += extra[3]
        aliases, cast = extra[4], job.gu_bf is not None
    out = pl.pallas_call(
        functools.partial(_ffn_kernel, cast=cast),
        grid=(T // tm,),
        in_specs=in_specs,
        out_specs=out_specs,
        out_shape=out_shape,
        input_output_aliases=aliases,
        compiler_params=pltpu.CompilerParams(
            dimension_semantics=("parallel",), vmem_limit_bytes=VMEM_LIMIT),
        name="ffn",
    )(*args)
    return out[0] if job is None else tuple(out)


def _pack_halves(y):
    half = y.shape[1] // 2
    lo = lax.bitcast_convert_type(y[:, :half].astype(_BF16).astype(_F32), _U32)
    hi = lax.bitcast_convert_type(y[:, half:].astype(_BF16).astype(_F32), _U32)
    return (lo >> 16) | (hi & HIGH_HALF_MASK)


def _unpack_halves(p):
    lo = lax.bitcast_convert_type(p << 16, _F32)
    hi = lax.bitcast_convert_type(p & HIGH_HALF_MASK, _F32)
    return lo, hi


def _route_kernel(x_ref, g_ref, w_r_ref, b_r_ref, earlier_ref, lane_before_ref,
                  meta_ref, cnt_ref, tab_ref, hs_ref,
                  carry_ref, placed_smem, second_smem, sorted_buf, sem,
                  *, region_rows, sort_rows):
    i = pl.program_id(0)
    tm = x_ref.shape[0]

    @pl.when(i == 0)
    def _():
        carry_ref[...] = jnp.zeros_like(carry_ref)
        for e in range(N_EXPERTS):
            placed_smem[e] = 0
        sorted_buf[:, sort_rows:, :] = jnp.zeros(
            (sorted_buf.shape[0], sorted_buf.shape[1] - sort_rows, sorted_buf.shape[2]), _U32)

    hf = _rms(x_ref[...], g_ref[...])

    lane = lax.broadcasted_iota(jnp.int32, (tm, LANES), 1)
    h_hi = hf.astype(_BF16)
    h_lo = (hf - h_hi.astype(_F32)).astype(_BF16)
    w_r = w_r_ref[...]
    w_hi = w_r.astype(_BF16)
    w_lo = (w_r - w_hi.astype(_F32)).astype(_BF16)
    hi_terms = _dot(h_hi, jnp.concatenate([w_hi, w_lo], axis=1))
    logits = hi_terms[:, :LANES] + hi_terms[:, LANES:] + _dot(h_lo, w_hi) + b_r_ref[...]
    v1 = jnp.max(logits, axis=-1, keepdims=True)
    i1 = jnp.min(jnp.where(logits == v1, lane, LANES), axis=-1, keepdims=True)
    rest = jnp.where(lane == i1, NEG_BIG, logits)
    v2 = jnp.max(rest, axis=-1, keepdims=True)
    i2 = jnp.min(jnp.where(rest == v2, lane, LANES), axis=-1, keepdims=True)
    z = jnp.exp(v2 - v1)
    g1 = 1.0 / (1.0 + z)
    g2 = z / (1.0 + z)

    sel1 = lane == i1
    sel2 = lane == i2
    chosen = jnp.where(sel1 | sel2, 1.0, 0.0)
    before = _dot(earlier_ref[...], chosen.astype(_BF16))
    cnt = jnp.sum(chosen, axis=0, keepdims=True)
    cnt_pad = jnp.ceil(cnt * (1.0 / SUBLANES)) * SUBLANES
    group_off = _dot(jnp.broadcast_to(cnt_pad, (SUBLANES, LANES)).astype(_BF16),
                     lane_before_ref[...])[0:1, :]
    local = group_off + before
    loc1 = jnp.sum(jnp.where(sel1, local, 0.0), axis=-1, keepdims=True)
    loc2 = jnp.sum(jnp.where(sel2, local, 0.0), axis=-1, keepdims=True)

    slot = lax.rem(i, 2)
    loc_t = jnp.where(lane == 0, loc1, jnp.where(lane == 1, loc2, -1.0)).T
    row = lax.broadcasted_iota(jnp.int32, (sort_rows, tm), 0).astype(_F32)
    perm = jnp.where((row == loc_t[0:1, :]) | (row == loc_t[1:2, :]), 1.0, 0.0).astype(_BF16)
    sorted_buf[slot, 0:sort_rows, :] = _pack_halves(_dot(perm, h_hi))

    def window(src, dst):
        return pltpu.make_async_copy(sorted_buf.at[slot, pl.ds(src, GROUP_WINDOW)],
                                     hs_ref.at[pl.ds(dst, GROUP_WINDOW)], sem)

    def wait_windows(second_flags):
        for e in range(N_EXPERTS):
            window(0, 0).wait()

            @pl.when(second_flags(e) != 0)
            def _():
                window(0, 0).wait()

    @pl.when(i > 0)
    def _():
        wait_windows(lambda e: second_smem[e])

    off_i = group_off.astype(jnp.int32)
    cnt_i = cnt_pad.astype(jnp.int32)
    for e in range(N_EXPERTS):
        src = pl.multiple_of(off_i[0, e], SUBLANES)
        start = pl.multiple_of(e * region_rows + placed_smem[e], SUBLANES)
        rows = cnt_i[0, e]
        second = (rows > GROUP_WINDOW).astype(jnp.int32)
        window(src, start).start()

        @pl.when(second != 0)
        def _():
            window(src + GROUP_WINDOW, start + GROUP_WINDOW).start()

        second_smem[e] = second
        placed_smem[e] = placed_smem[e] + rows
        tab_ref[0, e] = start
        tab_ref[0, N_EXPERTS + e] = rows
    carry_ref[...] += cnt_pad
    cnt_ref[...] = carry_ref[...]

    @pl.when(i == pl.num_programs(0) - 1)
    def _():
        wait_windows(lambda e: second_smem[e])

        def tail(e, part):
            dst = pl.multiple_of(e * region_rows + placed_smem[e], SUBLANES) + part * GROUP_WINDOW
            return window(sort_rows, dst)

        for e in range(N_EXPERTS):
            for part in range(EXPERT_ROW_TILE // GROUP_WINDOW):
                tail(e, part).start()
        for e in range(N_EXPERTS):
            for part in range(EXPERT_ROW_TILE // GROUP_WINDOW):
                tail(e, part).wait()

    meta = jnp.where(lane == META_G1, g1, 0.0)
    meta = jnp.where(lane == META_G2, g2, meta)
    meta = jnp.where(lane == META_L1, loc1, meta)
    meta = jnp.where(lane == META_L2, loc2, meta)
    meta_ref[...] = meta


def _region_rows(n_tokens):
    pad = (n_tokens // ROUTE_TILE) * (SUBLANES - 1) + 2 * EXPERT_ROW_TILE
    return -(-(n_tokens + pad) // EXPERT_ROW_TILE) * EXPERT_ROW_TILE


def _route(layer, moe_layer, x2, g, w_r, b_r):
    T, D = x2.shape
    tm = ROUTE_TILE
    region_rows = _region_rows(T)
    sort_rows = TOP_K * tm + N_EXPERTS * SUBLANES
    earlier = jnp.asarray(np.tril(np.ones((tm, tm), np.float32), -1), _BF16)
    lane_before = jnp.asarray(np.triu(np.ones((LANES, LANES), np.float32), 1), _BF16)
    const = lambda n: pl.BlockSpec((n, n), lambda i: (0, 0), pipeline_mode=pl.Buffered(1))
    return pl.pallas_call(
        functools.partial(_route_kernel, region_rows=region_rows, sort_rows=sort_rows),
        grid=(T // tm,),
        in_specs=[pl.BlockSpec((tm, D), lambda i: (i, 0)), _layer_spec(g, layer),
                  _layer_spec(w_r, moe_layer), _layer_spec(b_r, moe_layer),
                  const(tm), const(LANES)],
        out_specs=[pl.BlockSpec((tm, LANES), lambda i: (i, 0)),
                   pl.BlockSpec((1, LANES), lambda i: (0, 0)),
                   pl.BlockSpec((None, 1, 2 * N_EXPERTS), lambda i: (i, 0, 0),
                                memory_space=pltpu.SMEM),
                   pl.BlockSpec(memory_space=pl.ANY)],
        out_shape=[jax.ShapeDtypeStruct((T, LANES), _F32),
                   jax.ShapeDtypeStruct((1, LANES), _F32),
                   jax.ShapeDtypeStruct((T // tm, 1, 2 * N_EXPERTS), jnp.int32),
                   jax.ShapeDtypeStruct((N_EXPERTS * region_rows, D // 2), _U32)],
        scratch_shapes=[pltpu.VMEM((1, LANES), _F32),
                        pltpu.SMEM((N_EXPERTS,), jnp.int32),
                        pltpu.SMEM((N_EXPERTS,), jnp.int32),
                        pltpu.VMEM((2, sort_rows + GROUP_WINDOW, D // 2), _U32),
                        pltpu.SemaphoreType.DMA],
        compiler_params=pltpu.CompilerParams(
            dimension_semantics=("arbitrary",), vmem_limit_bytes=VMEM_LIMIT),
        name="moe_route",
    )(x2, g, w_r, b_r, earlier, lane_before)


def _expert_kernel(te_ref, tb_ref, tv_ref, hs_ref, w_gu_ref, w_down_ref, ys_ref):
    del te_ref, tb_ref
    i = pl.program_id(0)

    rt = ys_ref.shape[0]
    half = rt // 2
    rows_used = tv_ref[i]

    def swiglu_rows(n_rows):
        lo, hi = _unpack_halves(hs_ref[0:n_rows, :])
        h = jnp.concatenate([lo.astype(_BF16), hi.astype(_BF16)], axis=1)
        acc = None
        for c0, n in FF_CHUNKS:
            gate = _dot(h, w_gu_ref[:, c0:c0 + n])
            up = _dot(h, w_gu_ref[:, D_FF + c0:D_FF + c0 + n])
            act = (gate * jax.nn.sigmoid(gate) * up).astype(_BF16)
            part = _dot(act, w_down_ref[c0:c0 + n, :])
            acc = part if acc is None else acc + part
        ys_ref[0:n_rows, :] = _pack_halves(acc)

    @pl.when(rows_used == 0)
    def _():
        ys_ref[...] = jnp.zeros_like(ys_ref)

    @pl.when((rows_used > 0) & (rows_used <= half))
    def _():
        swiglu_rows(half)
        ys_ref[half:, :] = jnp.zeros((rt - half, ys_ref.shape[1]), ys_ref.dtype)

    @pl.when(rows_used > half)
    def _():
        swiglu_rows(rt)


def _experts(tile_expert, tile_block, tile_rows, hs, w_gu, w_down):
    n_rows, W = hs.shape
    rt = EXPERT_ROW_TILE
    D = w_gu.shape[1]
    grid_spec = pltpu.PrefetchScalarGridSpec(
        num_scalar_prefetch=3,
        grid=(tile_expert.shape[0],),
        in_specs=[pl.BlockSpec((rt, W), lambda i, te, tb, tv: (jnp.where(tv[i] > 0, tb[i], 0), 0)),
                  pl.BlockSpec((None, D, 2 * D_FF), lambda i, te, tb, tv: (te[i], 0, 0)),
                  pl.BlockSpec((None, D_FF, D), lambda i, te, tb, tv: (te[i], 0, 0))],
        out_specs=pl.BlockSpec((rt, W), lambda i, te, tb, tv: (tb[i], 0)),
    )
    return pl.pallas_call(
        _expert_kernel,
        grid_spec=grid_spec,
        out_shape=jax.ShapeDtypeStruct((n_rows + rt, W), hs.dtype),
        compiler_params=pltpu.CompilerParams(
            dimension_semantics=("arbitrary",), vmem_limit_bytes=VMEM_LIMIT),
        name="moe_experts",
    )(tile_expert, tile_block, tile_rows, hs, w_gu, w_down)


def _combine_kernel(tab_ref, x_ref, meta_ref, g_final_ref, ys_ref, o_ref, ybuf, sem, *, final_norm):
    i = pl.program_id(0)
    n = pl.num_programs(0)
    tm, D = x_ref.shape
    slot = lax.rem(i, 2)

    @pl.when(i == 0)
    def _():
        ybuf[...] = jnp.zeros_like(ybuf)

    def fetch(tile, to_slot):
        off = 0
        for e in range(N_EXPERTS):
            start = tab_ref[tile, e]
            rows = tab_ref[tile, N_EXPERTS + e]
            size = ROUTE_TILE
            while size >= SUBLANES:
                above = rows & ~(2 * size - 1)
                src = pl.multiple_of(start + above, SUBLANES)
                dst = pl.multiple_of(off + above, SUBLANES)

                @pl.when((rows & size) != 0)
                def _():
                    pltpu.make_async_copy(ys_ref.at[pl.ds(src, size)],
                                          ybuf.at[to_slot, pl.ds(dst, size)], sem.at[to_slot]).start()
                size //= 2
            off = off + rows
        return off

    def total_rows(tile):
        total = 0
        for e in range(N_EXPERTS):
            total = total + tab_ref[tile, N_EXPERTS + e]
        return total

    @pl.when(i == 0)
    def _():
        fetch(0, 0)

    @pl.when(i + 1 < n)
    def _():
        fetch(i + 1, 1 - slot)

    landed = pl.multiple_of(total_rows(i), SUBLANES)
    pltpu.make_async_copy(ys_ref.at[pl.ds(0, landed)], ybuf.at[slot, pl.ds(0, landed)],
                          sem.at[slot]).wait()

    lo, hi = _unpack_halves(ybuf[slot])
    y = jnp.concatenate([lo.astype(_BF16), hi.astype(_BF16)], axis=1)
    col = lax.broadcasted_iota(jnp.int32, (tm, ybuf.shape[1]), 1).astype(_F32)
    pick1 = jnp.where(col == meta_ref[:, META_L1:META_L1 + 1], 1.0, 0.0).astype(_BF16)
    pick2 = jnp.where(col == meta_ref[:, META_L2:META_L2 + 1], 1.0, 0.0).astype(_BF16)
    out = (x_ref[...] + meta_ref[:, META_G1:META_G1 + 1] * _dot(pick1, y)
           + meta_ref[:, META_G2:META_G2 + 1] * _dot(pick2, y))
    if final_norm:
        out = _rms(out, g_final_ref[...])
    o_ref[...] = out


def _combine(tab, x2, meta, ys, g_final, final_norm):
    T, D = x2.shape
    tm = ROUTE_TILE
    sort_rows = -(-(TOP_K * tm + N_EXPERTS * SUBLANES) // MXU_DIM) * MXU_DIM
    tile = pl.BlockSpec((tm, D), lambda i, tab: (i, 0))
    grid_spec = pltpu.PrefetchScalarGridSpec(
        num_scalar_prefetch=1,
        grid=(T // tm,),
        in_specs=[tile, pl.BlockSpec((tm, LANES), lambda i, tab: (i, 0)),
                  pl.BlockSpec((1, D), lambda i, tab: (0, 0)), pl.BlockSpec(memory_space=pl.ANY)],
        out_specs=tile,
        scratch_shapes=[pltpu.VMEM((2, sort_rows, D // 2), _U32), pltpu.SemaphoreType.DMA((2,))],
    )
    return pl.pallas_call(
        functools.partial(_combine_kernel, final_norm=final_norm),
        grid_spec=grid_spec,
        out_shape=jax.ShapeDtypeStruct(x2.shape, x2.dtype),
        compiler_params=pltpu.CompilerParams(
            dimension_semantics=("arbitrary",), vmem_limit_bytes=VMEM_LIMIT),
        name="moe_combine",
    )(tab, x2, meta, g_final, ys)


def _moe(layer, moe_layer, x2, g, w_r, b_r, w_gu, w_down, g_final, final_norm):
    T, D = x2.shape
    E = w_gu.shape[0]
    rt = EXPERT_ROW_TILE
    meta, counts, tab, hs = _route(layer, moe_layer, x2, g, w_r, b_r)

    region_tiles = hs.shape[0] // (E * rt)
    max_rows = TOP_K * T + (T // ROUTE_TILE) * E * (SUBLANES - 1)
    n_tiles = -(-max_rows // rt) + E
    counts = counts[0, :E].astype(jnp.int32)
    group_tiles = (counts + rt - 1) // rt
    tile_end = jnp.cumsum(group_tiles)
    tile_ids = jnp.arange(n_tiles, dtype=jnp.int32)
    tile_expert = jnp.minimum(
        jnp.sum((tile_ids[:, None] >= tile_end[None, :]).astype(jnp.int32), axis=1), E - 1)
    tile_valid = (tile_ids < tile_end[E - 1]).astype(jnp.int32)
    in_group = tile_ids - (tile_end - group_tiles)[tile_expert]
    tile_block = jnp.where(tile_valid != 0, tile_expert * region_tiles + in_group, E * region_tiles)

    tile_rows = jnp.clip(counts[tile_expert] - in_group * rt, 0, rt) * tile_valid
    ys = _experts(tile_expert, tile_block.astype(jnp.int32), tile_rows.astype(jnp.int32), hs, w_gu, w_down)
    return _combine(tab.reshape(tab.shape[0], tab.shape[2]), x2, meta, ys, g_final, final_norm)


def _final_norm_kernel(x_ref, g_ref, o_ref):
    o_ref[...] = _rms(x_ref[...], g_ref[...])


def _final_norm(x2, g):
    T, D = x2.shape
    tm = TOK_TILE
    tile = pl.BlockSpec((tm, D), lambda i: (i, 0))
    return pl.pallas_call(
        _final_norm_kernel,
        grid=(T // tm,),
        in_specs=[tile, pl.BlockSpec((1, D), lambda i: (0, 0))],
        out_specs=tile,
        out_shape=jax.ShapeDtypeStruct(x2.shape, x2.dtype),
        compiler_params=pltpu.CompilerParams(dimension_semantics=("parallel",)),
        name="final_norm",
    )(x2, g)


def kernel(x, mem, g_mix, w_in, w_pool, pool_scale, w_dw, b_dw, conv_ln_g, conv_ln_b, w_conv_out, w_mix_out, g_xattn, g_mem, w_xq, w_xkv, w_xo, g_ffn, w_ffn_gu, w_ffn_down, w_router, b_router, w_moe_gu, w_moe_down, g_final):
    B, S, D = x.shape
    depth = g_mix.shape[0]
    bf = lambda a: a.astype(_BF16)
    rows = lambda a: a.reshape(a.shape[0], 1, a.shape[1])

    g_mix, pool_scale, b_dw, conv_ln_g, conv_ln_b = map(rows, (g_mix, pool_scale, b_dw, conv_ln_g, conv_ln_b))
    g_xattn, g_mem, g_ffn = map(rows, (g_xattn, g_mem, g_ffn))
    w_dw = w_dw.reshape(depth, CONV_KERNEL, CONV_WIDTH)
    w_in, w_pool, w_conv_out, w_mix_out = map(bf, (w_in, w_pool, w_conv_out, w_mix_out))
    w_xq, w_xkv, w_xo = map(bf, (w_xq, w_xkv, w_xo))
    w_ffn_gu, w_ffn_down = map(bf, (w_ffn_gu, w_ffn_down))
    w_r_pad = jnp.pad(w_router, ((0, 0), (0, 0), (0, LANES - N_EXPERTS)))
    b_r_pad = rows(jnp.pad(b_router, ((0, 0), (0, LANES - N_EXPERTS)), constant_values=NEG_BIG))
    g_final = g_final.reshape(1, D)

    half_slices = (w_moe_gu.shape[1] * CAST_PARTS) // 2
    k_all, v_all = _kv_proj(mem, g_mem, w_xkv)
    for i in range(depth):
        last = i == depth - 1
        mixer_args = (i, x, g_mix, w_in, w_pool, pool_scale, w_dw, b_dw, conv_ln_g, conv_ln_b,
                      w_conv_out, w_mix_out)
        if i % 2 == 0 and not last:
            job = lambda first, gu, down: _CastJob((i + 1) // 2, first, w_moe_gu, w_moe_down, gu, down)
            x, moe_gu_bf, moe_down_bf = _mixer(*mixer_args, job(0, None, None))
            x = _xattn(i, x, g_xattn, w_xq, k_all, v_all, w_xo)
            x2, moe_gu_bf, moe_down_bf = _ffn(i, i // 2, x.reshape(B * S, D), g_ffn, w_ffn_gu, w_ffn_down,
                                              job(half_slices, moe_gu_bf, moe_down_bf))
        else:
            x = _mixer(*mixer_args)
            x = _xattn(i, x, g_xattn, w_xq, k_all, v_all, w_xo)
            if i % 2 == 0:
                x2 = _final_norm(_ffn(i, i // 2, x.reshape(B * S, D), g_ffn, w_ffn_gu, w_ffn_down), g_final)
            else:
                x2 = _moe(i, i // 2, x.reshape(B * S, D), g_ffn, w_r_pad, b_r_pad, moe_gu_bf, moe_down_bf,
                          g_final, last)
        x = x2.reshape(B, S, D)
    return x
```

```python
import functools
from typing import NamedTuple, Optional

import jax
import jax.numpy as jnp
import numpy as np
from jax import lax
from jax.experimental import pallas as pl
from jax.experimental.pallas import tpu as pltpu

D_MODEL = 1024
POOL_WIDTH = 512
POOL_GROUPS = 4
POOL_WINDOWS = (2, 4, 8, 16)
POOL_GROUP_IN = POOL_WIDTH // POOL_GROUPS
POOL_GROUP_OUT = D_MODEL // POOL_GROUPS
CONV_WIDTH = 512
CONV_KERNEL = 31
IN_COLS = POOL_WIDTH + 2 * CONV_WIDTH + 2 * D_MODEL
N_XHEADS = 4
XHEAD_DIM = D_MODEL // N_XHEADS
D_FF = 2816
N_EXPERTS = 8
TOP_K = 2
EPS = 1e-6

COL_POOL = 0
COL_GLU_A = POOL_WIDTH
COL_GLU_B = POOL_WIDTH + CONV_WIDTH
COL_GATE_POOL = POOL_WIDTH + 2 * CONV_WIDTH
COL_GATE_CONV = COL_GATE_POOL + D_MODEL

LANES = 128
SUBLANES = 8
POOL_HALO = 16
CONV_HALO = 32
SEQ_TILE = 512
XATTN_TILE = 1024
TOK_TILE = 512
MXU_DIM = 256
FF_CHUNKS = ((0, 6 * MXU_DIM), (6 * MXU_DIM, 5 * MXU_DIM))
assert sum(n for _, n in FF_CHUNKS) == D_FF
CONV_ROW_CHUNK = 64
ROUTE_TILE = 512
EXPERT_ROW_TILE = 512
GROUP_WINDOW = 256
CAST_PARTS = 8
VMEM_LIMIT = 56 * 1024 * 1024
NEG_BIG = -1e30
META_G1, META_G2, META_L1, META_L2 = range(4)

_F32 = jnp.float32
_BF16 = jnp.bfloat16
_U32 = jnp.uint32
HIGH_HALF_MASK = np.uint32(0xFFFF0000)


def _layer_spec(arr, layer):
    tail = (0,) * (arr.ndim - 1)
    return pl.BlockSpec((None,) + arr.shape[1:], lambda *_: (layer,) + tail,
                        pipeline_mode=pl.Buffered(1))


def _rms(x, g):
    return x * lax.rsqrt(jnp.mean(x * x, axis=-1, keepdims=True) + EPS) * g


def _dot(a, b):
    return jnp.dot(a, b, preferred_element_type=_F32)


def _zero_like_bits(v):
    u = lax.bitcast_convert_type(v, _U32)
    return lax.bitcast_convert_type((u >> 16) >> 16, _F32)


def _mixer_kernel(x_ref, g_ref, w_in_ref, w_pool_ref, pscale_ref, w_dw_ref, b_dw_ref,
                  ln_g_ref, ln_b_ref, w_pw_ref, w_out_ref, *rest, cast):
    if cast is not None:
        rest, cast_slice = _split_cast_refs(rest, 1, cast)
        cast_slice()
    o_ref, up_ext, cv_ext, cv_shift, cv_out, mix_ref, gate_ref = rest
    s = pl.program_id(1)
    ts = x_ref.shape[0]

    @pl.when(s == 0)
    def _():
        up_ext[0:POOL_HALO, :] = jnp.zeros((POOL_HALO, POOL_WIDTH), _F32)
        cv_ext[:, 0:CONV_HALO, :] = jnp.zeros((cv_ext.shape[0], CONV_HALO, LANES), _F32)

    @pl.when(s > 0)
    def _():
        up_ext[0:POOL_HALO, :] = up_ext[ts:ts + POOL_HALO, :]
        cv_ext[:, 0:CONV_HALO, :] = cv_ext[:, ts:ts + CONV_HALO, :]

    x = x_ref[...]
    h = _rms(x, g_ref[...]).astype(_BF16)

    glu_a = _dot(h, w_in_ref[:, COL_GLU_A:COL_GLU_A + CONV_WIDTH])
    glu_b = _dot(h, w_in_ref[:, COL_GLU_B:COL_GLU_B + CONV_WIDTH])
    glu = glu_a * jax.nn.sigmoid(glu_b)
    for blk in range(cv_ext.shape[0]):
        cv_ext[blk, CONV_HALO:CONV_HALO + ts, :] = glu[:, blk * LANES:(blk + 1) * LANES]

    shift_rows = cv_shift.shape[2]
    for r in range(1, SUBLANES):
        cv_shift[r - 1] = cv_ext[:, r:r + shift_rows, :]

    tap0 = CONV_HALO - (CONV_KERNEL - 1)
    def side_task(col0, store):
        res = _dot(h, w_in_ref[:, col0:col0 + MXU_DIM])
        store(res)
        return _zero_like_bits(res[ts - 1:ts, MXU_DIM - LANES:])

    def pool_store(c):
        def store(res):
            up_ext[POOL_HALO:POOL_HALO + ts, c:c + MXU_DIM] = res
        return store

    def gate_store(c):
        def store(res):
            gate_ref[:, c:c + MXU_DIM] = jax.nn.sigmoid(res)
        return store

    side_tasks = ([(COL_POOL + c, pool_store(c)) for c in range(0, POOL_WIDTH, MXU_DIM)]
                  + [(COL_GATE_POOL + c, gate_store(c)) for c in range(0, 2 * D_MODEL, MXU_DIM)])
    unit_rows = 2 * CONV_ROW_CHUNK
    units = [(c0, q0) for c0 in range(0, CONV_WIDTH, LANES) for q0 in range(0, ts, unit_rows)]
    assert len(side_tasks) <= len(units)
    pace = None
    for u, (c0, q0) in enumerate(units):
        bias = b_dw_ref[:, c0:c0 + LANES]
        if pace is not None:
            bias = bias + pace
        pace = side_task(*side_tasks[u]) if u < len(side_tasks) else None
        for r0 in range(q0, q0 + unit_rows, CONV_ROW_CHUNK):
            acc = jnp.broadcast_to(bias, (CONV_ROW_CHUNK, LANES))
            for k in range(CONV_KERNEL):
                off = tap0 + k
                mis = off % SUBLANES
                row = r0 + off - mis
                if mis == 0:
                    win = cv_ext[c0 // LANES, row:row + CONV_ROW_CHUNK, :]
                else:
                    win = cv_shift[mis - 1, c0 // LANES, row:row + CONV_ROW_CHUNK, :]
                acc = acc + w_dw_ref[k:k + 1, c0:c0 + LANES] * win
            cv_out[r0:r0 + CONV_ROW_CHUNK, c0:c0 + LANES] = acc

    pos = (s * ts + 1 + lax.broadcasted_iota(jnp.int32, (ts, 1), 0)).astype(_F32)
    for g, w in enumerate(POOL_WINDOWS):
        c0 = g * POOL_GROUP_IN
        win = up_ext[:, c0:c0 + POOL_GROUP_IN]
        span = 1
        while span < w:
            win = win + pltpu.roll(win, span, 0)
            span *= 2
        cur = up_ext[POOL_HALO:POOL_HALO + ts, c0:c0 + POOL_GROUP_IN]
        pooled = win[POOL_HALO:, :] / jnp.minimum(pos, float(w)) - cur
        d0 = g * POOL_GROUP_OUT
        mix_ref[:, d0:d0 + POOL_GROUP_OUT] = (
            _dot(pooled.astype(_BF16), w_pool_ref[g]) * pscale_ref[:, d0:d0 + POOL_GROUP_OUT])

    u = cv_out[...]
    mu = jnp.mean(u, axis=-1, keepdims=True)
    uc = u - mu
    var = jnp.mean(uc * uc, axis=-1, keepdims=True)
    un = uc * lax.rsqrt(var + EPS) * ln_g_ref[...] + ln_b_ref[...]
    un = un * jax.nn.sigmoid(un)
    y_conv = _dot(un.astype(_BF16), w_pw_ref[...])
    mix = gate_ref[:, 0:D_MODEL] * mix_ref[...] + gate_ref[:, D_MODEL:] * y_conv

    o_ref[...] = x + _dot(mix.astype(_BF16), w_out_ref[...])


def _mixer(layer, x, g, w_in, w_pool, pscale, w_dw, b_dw, ln_g, ln_b, w_pw, w_out, job=None):
    B, S, D = x.shape
    ts = SEQ_TILE
    n_seq = S // ts
    shift_rows = ts + CONV_HALO - SUBLANES
    tile = pl.BlockSpec((None, ts, D), lambda b, s: (b, s, 0))
    params = (g, w_in, w_pool, pscale, w_dw, b_dw, ln_g, ln_b, w_pw, w_out)
    in_specs = [tile] + [_layer_spec(p, layer) for p in params]
    args = [x, *params]
    out_specs = [tile]
    out_shape = [jax.ShapeDtypeStruct(x.shape, x.dtype)]
    aliases, cast = {}, None
    if job is not None:
        assert B * n_seq == _cast_slices(job) // 2
        extra = _cast_plumbing(job, lambda b, s: b * n_seq + s, len(in_specs), len(out_specs))
        in_specs += extra[0]
        args += extra[1]
        out_specs += extra[2]
        out_shape += extra[3]
        aliases, cast = extra[4], job.gu_bf is not None
    out = pl.pallas_call(
        functools.partial(_mixer_kernel, cast=cast),
        grid=(B, n_seq),
        in_specs=in_specs,
        out_specs=out_specs,
        out_shape=out_shape,
        input_output_aliases=aliases,
        scratch_shapes=[pltpu.VMEM((POOL_HALO + ts, POOL_WIDTH), _F32),
                        pltpu.VMEM((CONV_WIDTH // LANES, CONV_HALO + ts, LANES), _F32),
                        pltpu.VMEM((SUBLANES - 1, CONV_WIDTH // LANES, shift_rows, LANES), _F32),
                        pltpu.VMEM((ts, CONV_WIDTH), _F32),
                        pltpu.VMEM((ts, D), _F32),
                        pltpu.VMEM((ts, 2 * D), _F32)],
        compiler_params=pltpu.CompilerParams(
            dimension_semantics=("parallel", "arbitrary"), vmem_limit_bytes=VMEM_LIMIT),
        name="mixer",
    )(*args)
    return out[0] if job is None else tuple(out)


def _kv_kernel(m_ref, g_ref, w_ref, k_ref, v_ref):
    m = _rms(m_ref[...], g_ref[...]).astype(_BF16)
    k_ref[...] = (_dot(m, w_ref[:, 0:D_MODEL]) * (XHEAD_DIM ** -0.5)).astype(_BF16)
    v_ref[...] = _dot(m, w_ref[:, D_MODEL:2 * D_MODEL]).astype(_BF16)


def _kv_proj(mem, g, w_kv):
    B, M, D = mem.shape
    L = w_kv.shape[0]
    rows = B * M
    out_blk = pl.BlockSpec((None, rows, D), lambda l: (l, 0, 0))
    k, v = pl.pallas_call(
        _kv_kernel,
        grid=(L,),
        in_specs=[pl.BlockSpec((rows, D), lambda l: (0, 0)),
                  pl.BlockSpec((None,) + g.shape[1:], lambda l: (l, 0, 0)),
                  pl.BlockSpec((None,) + w_kv.shape[1:], lambda l: (l, 0, 0))],
        out_specs=[out_blk, out_blk],
        out_shape=[jax.ShapeDtypeStruct((L, rows, D), _BF16)] * 2,
        compiler_params=pltpu.CompilerParams(
            dimension_semantics=("parallel",), vmem_limit_bytes=VMEM_LIMIT),
        name="kv_proj",
    )(mem.reshape(rows, D), g, w_kv)
    return k.reshape(L, B, M, D), v.reshape(L, B, M, D)


class _CastJob(NamedTuple):
    moe_layer: int
    first_slice: int
    w_gu: jax.Array
    w_down: jax.Array
    gu_bf: Optional[jax.Array]
    down_bf: Optional[jax.Array]


def _cast_slices(job):
    return job.w_gu.shape[1] * CAST_PARTS


def _cast_plumbing(job, step_of, n_in, n_out):
    _, _, d, f2 = job.w_gu.shape
    f = job.w_down.shape[2]

    def part(*grid_idx):
        q = jnp.minimum(job.first_slice + step_of(*grid_idx), _cast_slices(job) - 1)
        return q // CAST_PARTS, q % CAST_PARTS

    def src_idx(*grid_idx):
        return (job.moe_layer,) + part(*grid_idx) + (0,)

    def dst_idx(*grid_idx):
        return part(*grid_idx) + (0,)

    in_specs = [pl.BlockSpec((None, None, d // CAST_PARTS, f2), src_idx),
                pl.BlockSpec((None, None, f // CAST_PARTS, d), src_idx)]
    out_specs = [pl.BlockSpec((None, d // CAST_PARTS, f2), dst_idx),
                 pl.BlockSpec((None, f // CAST_PARTS, d), dst_idx)]
    args, aliases = [job.w_gu, job.w_down], {}
    if job.gu_bf is not None:
        in_specs += [pl.BlockSpec(memory_space=pl.ANY)] * 2
        args += [job.gu_bf, job.down_bf]
        aliases = {n_in + 2: n_out, n_in + 3: n_out + 1}
    out_shapes = [jax.ShapeDtypeStruct(job.w_gu.shape[1:], _BF16),
                  jax.ShapeDtypeStruct(job.w_down.shape[1:], _BF16)]
    return in_specs, args, out_specs, out_shapes, aliases


def _split_cast_refs(rest, n_out, cast_fill):
    n_cast_in = 4 if cast_fill else 2
    gu_in, down_in = rest[0], rest[1]
    host_out = rest[n_cast_in:n_cast_in + n_out]
    gu_out, down_out = rest[n_cast_in + n_out], rest[n_cast_in + n_out + 1]

    def cast():
        gu_out[...] = gu_in[...].astype(_BF16)
        down_out[...] = down_in[...].astype(_BF16)

    return tuple(host_out) + tuple(rest[n_cast_in + n_out + 2:]), cast


def _xattn_kernel(x_ref, g_ref, wq_ref, k_ref, v_ref, wo_ref, o_ref, att_ref):
    x = x_ref[...]
    h = _rms(x, g_ref[...]).astype(_BF16)
    q = _dot(h, wq_ref[...]).astype(_BF16)
    for hd in range(N_XHEADS):
        c0 = hd * XHEAD_DIM
        sc = lax.dot_general(q[:, c0:c0 + XHEAD_DIM], k_ref[:, c0:c0 + XHEAD_DIM],
                             (((1,), (1,)), ((), ())), preferred_element_type=_F32)
        p = jnp.exp(sc - jnp.max(sc, axis=-1, keepdims=True))
        denom = jnp.sum(p, axis=-1, keepdims=True)
        att = _dot(p.astype(_BF16), v_ref[:, c0:c0 + XHEAD_DIM]) / denom
        att_ref[:, c0:c0 + XHEAD_DIM] = att.astype(_BF16)
    o_ref[...] = x + _dot(att_ref[...], wo_ref[...])


def _xattn(layer, x, g, w_q, k, v, w_o):
    B, S, D = x.shape
    M = k.shape[2]
    ts = XATTN_TILE
    tile = pl.BlockSpec((None, ts, D), lambda b, s: (b, s, 0))
    kv_blk = pl.BlockSpec((None, None, M, D), lambda b, s: (layer, b, 0, 0))
    return pl.pallas_call(
        _xattn_kernel,
        grid=(B, S // ts),
        in_specs=[tile, _layer_spec(g, layer), _layer_spec(w_q, layer), kv_blk, kv_blk,
                  _layer_spec(w_o, layer)],
        out_specs=tile,
        out_shape=jax.ShapeDtypeStruct(x.shape, x.dtype),
        scratch_shapes=[pltpu.VMEM((ts, D), _BF16)],
        compiler_params=pltpu.CompilerParams(
            dimension_semantics=("parallel", "parallel"), vmem_limit_bytes=VMEM_LIMIT),
        name="xattn",
    )(x, g, w_q, k, v, w_o)


def _ffn_kernel(x_ref, g_ref, w_gu_ref, w_down_ref, *rest, cast):
    if cast is not None:
        rest, cast_slice = _split_cast_refs(rest, 1, cast)
        cast_slice()
    (o_ref,) = rest
    x = x_ref[...]
    h = _rms(x, g_ref[...]).astype(_BF16)
    acc = x
    for c0, n in FF_CHUNKS:
        gate = _dot(h, w_gu_ref[:, c0:c0 + n])
        up = _dot(h, w_gu_ref[:, D_FF + c0:D_FF + c0 + n])
        act = (gate * jax.nn.sigmoid(gate) * up).astype(_BF16)
        acc = acc + _dot(act, w_down_ref[c0:c0 + n, :])
    o_ref[...] = acc


def _ffn(layer, ffn_layer, x2, g, w_gu, w_down, job=None):
    T, D = x2.shape
    tm = TOK_TILE
    tile = pl.BlockSpec((tm, D), lambda i: (i, 0))
    in_specs = [tile, _layer_spec(g, layer), _layer_spec(w_gu, ffn_layer),
                _layer_spec(w_down, ffn_layer)]
    args = [x2, g, w_gu, w_down]
    out_specs = [tile]
    out_shape = [jax.ShapeDtypeStruct(x2.shape, x2.dtype)]
    aliases, cast = {}, None
    if job is not None:
        assert T // tm == (job.w_gu.shape[1] * CAST_PARTS) // 2
        extra = _cast_plumbing(job, lambda i: i, len(in_specs), len(out_specs))
        in_specs += extra[0]
        args += extra[1]
        out_specs += extra[2]
        out_shape += extra[3]
        aliases, cast = extra[4], job.gu_bf is not None
    out = pl.pallas_call(
        functools.partial(_ffn_kernel, cast=cast),
        grid=(T // tm,),
        in_specs=in_specs,
        out_specs=out_specs,
        out_shape=out_shape,
        input_output_aliases=aliases,
        compiler_params=pltpu.CompilerParams(
            dimension_semantics=("parallel",), vmem_limit_bytes=VMEM_LIMIT),
        name="ffn",
    )(*args)
    return out[0] if job is None else tuple(out)


def _pack_halves(y):
    half = y.shape[1] // 2
    lo = lax.bitcast_convert_type(y[:, :half].astype(_BF16).astype(_F32), _U32)
    hi = lax.bitcast_convert_type(y[:, half:].astype(_BF16).astype(_F32), _U32)
    return (lo >> 16) | (hi & HIGH_HALF_MASK)


def _unpack_halves(p):
    lo = lax.bitcast_convert_type(p << 16, _F32)
    hi = lax.bitcast_convert_type(p & HIGH_HALF_MASK, _F32)
    return lo, hi


def _route_kernel(x_ref, g_ref, w_r_ref, b_r_ref, earlier_ref, lane_before_ref,
                  meta_ref, cnt_ref, tab_ref, hs_ref,
                  carry_ref, placed_smem, second_smem, sorted_buf, sem,
                  *, region_rows, sort_rows):
    i = pl.program_id(0)
    tm = x_ref.shape[0]

    @pl.when(i == 0)
    def _():
        carry_ref[...] = jnp.zeros_like(carry_ref)
        for e in range(N_EXPERTS):
            placed_smem[e] = 0
        sorted_buf[:, sort_rows:, :] = jnp.zeros(
            (sorted_buf.shape[0], sorted_buf.shape[1] - sort_rows, sorted_buf.shape[2]), _U32)

    hf = _rms(x_ref[...], g_ref[...])

    lane = lax.broadcasted_iota(jnp.int32, (tm, LANES), 1)
    h_hi = hf.astype(_BF16)
    h_lo = (hf - h_hi.astype(_F32)).astype(_BF16)
    w_r = w_r_ref[...]
    w_hi = w_r.astype(_BF16)
    w_lo = (w_r - w_hi.astype(_F32)).astype(_BF16)
    hi_terms = _dot(h_hi, jnp.concatenate([w_hi, w_lo], axis=1))
    logits = hi_terms[:, :LANES] + hi_terms[:, LANES:] + _dot(h_lo, w_hi) + b_r_ref[...]
    v1 = jnp.max(logits, axis=-1, keepdims=True)
    i1 = jnp.min(jnp.where(logits == v1, lane, LANES), axis=-1, keepdims=True)
    rest = jnp.where(lane == i1, NEG_BIG, logits)
    v2 = jnp.max(rest, axis=-1, keepdims=True)
    i2 = jnp.min(jnp.where(rest == v2, lane, LANES), axis=-1, keepdims=True)
    z = jnp.exp(v2 - v1)
    g1 = 1.0 / (1.0 + z)
    g2 = z / (1.0 + z)

    sel1 = lane == i1
    sel2 = lane == i2
    chosen = jnp.where(sel1 | sel2, 1.0, 0.0)
    before = _dot(earlier_ref[...], chosen.astype(_BF16))
    cnt = jnp.sum(chosen, axis=0, keepdims=True)
    cnt_pad = jnp.ceil(cnt * (1.0 / SUBLANES)) * SUBLANES
    group_off = _dot(jnp.broadcast_to(cnt_pad, (SUBLANES, LANES)).astype(_BF16),
                     lane_before_ref[...])[0:1, :]
    local = group_off + before
    loc1 = jnp.sum(jnp.where(sel1, local, 0.0), axis=-1, keepdims=True)
    loc2 = jnp.sum(jnp.where(sel2, local, 0.0), axis=-1, keepdims=True)

    slot = lax.rem(i, 2)
    loc_t = jnp.where(lane == 0, loc1, jnp.where(lane == 1, loc2, -1.0)).T
    row = lax.broadcasted_iota(jnp.int32, (sort_rows, tm), 0).astype(_F32)
    perm = jnp.where((row == loc_t[0:1, :]) | (row == loc_t[1:2, :]), 1.0, 0.0).astype(_BF16)
    sorted_buf[slot, 0:sort_rows, :] = _pack_halves(_dot(perm, h_hi))

    def window(src, dst):
        return pltpu.make_async_copy(sorted_buf.at[slot, pl.ds(src, GROUP_WINDOW)],
                                     hs_ref.at[pl.ds(dst, GROUP_WINDOW)], sem)

    def wait_windows(second_flags):
        for e in range(N_EXPERTS):
            window(0, 0).wait()

            @pl.when(second_flags(e) != 0)
            def _():
                window(0, 0).wait()

    @pl.when(i > 0)
    def _():
        wait_windows(lambda e: second_smem[e])

    off_i = group_off.astype(jnp.int32)
    cnt_i = cnt_pad.astype(jnp.int32)
    for e in range(N_EXPERTS):
        src = pl.multiple_of(off_i[0, e], SUBLANES)
        start = pl.multiple_of(e * region_rows + placed_smem[e], SUBLANES)
        rows = cnt_i[0, e]
        second = (rows > GROUP_WINDOW).astype(jnp.int32)
        window(src, start).start()

        @pl.when(second != 0)
        def _():
            window(src + GROUP_WINDOW, start + GROUP_WINDOW).start()

        second_smem[e] = second
        placed_smem[e] = placed_smem[e] + rows
        tab_ref[0, e] = start
        tab_ref[0, N_EXPERTS + e] = rows
    carry_ref[...] += cnt_pad
    cnt_ref[...] = carry_ref[...]

    @pl.when(i == pl.num_programs(0) - 1)
    def _():
        wait_windows(lambda e: second_smem[e])

        def tail(e, part):
            dst = pl.multiple_of(e * region_rows + placed_smem[e], SUBLANES) + part * GROUP_WINDOW
            return window(sort_rows, dst)

        for e in range(N_EXPERTS):
            for part in range(EXPERT_ROW_TILE // GROUP_WINDOW):
                tail(e, part).start()
        for e in range(N_EXPERTS):
            for part in range(EXPERT_ROW_TILE // GROUP_WINDOW):
                tail(e, part).wait()

    meta = jnp.where(lane == META_G1, g1, 0.0)
    meta = jnp.where(lane == META_G2, g2, meta)
    meta = jnp.where(lane == META_L1, loc1, meta)
    meta = jnp.where(lane == META_L2, loc2, meta)
    meta_ref[...] = meta


def _region_rows(n_tokens):
    pad = (n_tokens // ROUTE_TILE) * (SUBLANES - 1) + 2 * EXPERT_ROW_TILE
    return -(-(n_tokens + pad) // EXPERT_ROW_TILE) * EXPERT_ROW_TILE


def _route(layer, moe_layer, x2, g, w_r, b_r):
    T, D = x2.shape
    tm = ROUTE_TILE
    region_rows = _region_rows(T)
    sort_rows = TOP_K * tm + N_EXPERTS * SUBLANES
    earlier = jnp.asarray(np.tril(np.ones((tm, tm), np.float32), -1), _BF16)
    lane_before = jnp.asarray(np.triu(np.ones((LANES, LANES), np.float32), 1), _BF16)
    const = lambda n: pl.BlockSpec((n, n), lambda i: (0, 0), pipeline_mode=pl.Buffered(1))
    return pl.pallas_call(
        functools.partial(_route_kernel, region_rows=region_rows, sort_rows=sort_rows),
        grid=(T // tm,),
        in_specs=[pl.BlockSpec((tm, D), lambda i: (i, 0)), _layer_spec(g, layer),
                  _layer_spec(w_r, moe_layer), _layer_spec(b_r, moe_layer),
                  const(tm), const(LANES)],
        out_specs=[pl.BlockSpec((tm, LANES), lambda i: (i, 0)),
                   pl.BlockSpec((1, LANES), lambda i: (0, 0)),
                   pl.BlockSpec((None, 1, 2 * N_EXPERTS), lambda i: (i, 0, 0),
                                memory_space=pltpu.SMEM),
                   pl.BlockSpec(memory_space=pl.ANY)],
        out_shape=[jax.ShapeDtypeStruct((T, LANES), _F32),
                   jax.ShapeDtypeStruct((1, LANES), _F32),
                   jax.ShapeDtypeStruct((T // tm, 1, 2 * N_EXPERTS), jnp.int32),
                   jax.ShapeDtypeStruct((N_EXPERTS * region_rows, D // 2), _U32)],
        scratch_shapes=[pltpu.VMEM((1, LANES), _F32),
                        pltpu.SMEM((N_EXPERTS,), jnp.int32),
                        pltpu.SMEM((N_EXPERTS,), jnp.int32),
                        pltpu.VMEM((2, sort_rows + GROUP_WINDOW, D // 2), _U32),
                        pltpu.SemaphoreType.DMA],
        compiler_params=pltpu.CompilerParams(
            dimension_semantics=("arbitrary",), vmem_limit_bytes=VMEM_LIMIT),
        name="moe_route",
    )(x2, g, w_r, b_r, earlier, lane_before)


def _expert_kernel(te_ref, tb_ref, tv_ref, hs_ref, w_gu_ref, w_down_ref, ys_ref):
    del te_ref, tb_ref
    i = pl.program_id(0)

    rt = ys_ref.shape[0]
    half = rt // 2
    rows_used = tv_ref[i]

    def swiglu_rows(n_rows):
        lo, hi = _unpack_halves(hs_ref[0:n_rows, :])
        h = jnp.concatenate([lo.astype(_BF16), hi.astype(_BF16)], axis=1)
        acc = None
        for c0, n in FF_CHUNKS:
            gate = _dot(h, w_gu_ref[:, c0:c0 + n])
            up = _dot(h, w_gu_ref[:, D_FF + c0:D_FF + c0 + n])
            act = (gate * jax.nn.sigmoid(gate) * up).astype(_BF16)
            part = _dot(act, w_down_ref[c0:c0 + n, :])
            acc = part if acc is None else acc + part
        ys_ref[0:n_rows, :] = _pack_halves(acc)

    @pl.when(rows_used == 0)
    def _():
        ys_ref[...] = jnp.zeros_like(ys_ref)

    @pl.when((rows_used > 0) & (rows_used <= half))
    def _():
        swiglu_rows(half)
        ys_ref[half:, :] = jnp.zeros((rt - half, ys_ref.shape[1]), ys_ref.dtype)

    @pl.when(rows_used > half)
    def _():
        swiglu_rows(rt)


def _experts(tile_expert, tile_block, tile_rows, hs, w_gu, w_down):
    n_rows, W = hs.shape
    rt = EXPERT_ROW_TILE
    D = w_gu.shape[1]
    grid_spec = pltpu.PrefetchScalarGridSpec(
        num_scalar_prefetch=3,
        grid=(tile_expert.shape[0],),
        in_specs=[pl.BlockSpec((rt, W), lambda i, te, tb, tv: (jnp.where(tv[i] > 0, tb[i], 0), 0)),
                  pl.BlockSpec((None, D, 2 * D_FF), lambda i, te, tb, tv: (te[i], 0, 0)),
                  pl.BlockSpec((None, D_FF, D), lambda i, te, tb, tv: (te[i], 0, 0))],
        out_specs=pl.BlockSpec((rt, W), lambda i, te, tb, tv: (tb[i], 0)),
    )
    return pl.pallas_call(
        _expert_kernel,
        grid_spec=grid_spec,
        out_shape=jax.ShapeDtypeStruct((n_rows + rt, W), hs.dtype),
        compiler_params=pltpu.CompilerParams(
            dimension_semantics=("arbitrary",), vmem_limit_bytes=VMEM_LIMIT),
        name="moe_experts",
    )(tile_expert, tile_block, tile_rows, hs, w_gu, w_down)


def _combine_kernel(tab_ref, x_ref, meta_ref, g_final_ref, ys_ref, o_ref, ybuf, sem, *, final_norm):
    i = pl.program_id(0)
    n = pl.num_programs(0)
    tm, D = x_ref.shape
    slot = lax.rem(i, 2)

    @pl.when(i == 0)
    def _():
        ybuf[...] = jnp.zeros_like(ybuf)

    def fetch(tile, to_slot):
        off = 0
        for e in range(N_EXPERTS):
            start = tab_ref[tile, e]
            rows = tab_ref[tile, N_EXPERTS + e]
            size = ROUTE_TILE
            while size >= SUBLANES:
                above = rows & ~(2 * size - 1)
                src = pl.multiple_of(start + above, SUBLANES)
                dst = pl.multiple_of(off + above, SUBLANES)

                @pl.when((rows & size) != 0)
                def _():
                    pltpu.make_async_copy(ys_ref.at[pl.ds(src, size)],
                                          ybuf.at[to_slot, pl.ds(dst, size)], sem.at[to_slot]).start()
                size //= 2
            off = off + rows
        return off

    def total_rows(tile):
        total = 0
        for e in range(N_EXPERTS):
            total = total + tab_ref[tile, N_EXPERTS + e]
        return total

    @pl.when(i == 0)
    def _():
        fetch(0, 0)

    @pl.when(i + 1 < n)
    def _():
        fetch(i + 1, 1 - slot)

    landed = pl.multiple_of(total_rows(i), SUBLANES)
    pltpu.make_async_copy(ys_ref.at[pl.ds(0, landed)], ybuf.at[slot, pl.ds(0, landed)],
                          sem.at[slot]).wait()

    lo, hi = _unpack_halves(ybuf[slot])
    y = jnp.concatenate([lo.astype(_BF16), hi.astype(_BF16)], axis=1)
    col = lax.broadcasted_iota(jnp.int32, (tm, ybuf.shape[1]), 1).astype(_F32)
    pick1 = jnp.where(col == meta_ref[:, META_L1:META_L1 + 1], 1.0, 0.0).astype(_BF16)
    pick2 = jnp.where(col == meta_ref[:, META_L2:META_L2 + 1], 1.0, 0.0).astype(_BF16)
    out = (x_ref[...] + meta_ref[:, META_G1:META_G1 + 1] * _dot(pick1, y)
           + meta_ref[:, META_G2:META_G2 + 1] * _dot(pick2, y))
    if final_norm:
        out = _rms(out, g_final_ref[...])
    o_ref[...] = out


def _combine(tab, x2, meta, ys, g_final, final_norm):
    T, D = x2.shape
    tm = ROUTE_TILE
    sort_rows = -(-(TOP_K * tm + N_EXPERTS * SUBLANES) // MXU_DIM) * MXU_DIM
    tile = pl.BlockSpec((tm, D), lambda i, tab: (i, 0))
    grid_spec = pltpu.PrefetchScalarGridSpec(
        num_scalar_prefetch=1,
        grid=(T // tm,),
        in_specs=[tile, pl.BlockSpec((tm, LANES), lambda i, tab: (i, 0)),
                  pl.BlockSpec((1, D), lambda i, tab: (0, 0)), pl.BlockSpec(memory_space=pl.ANY)],
        out_specs=tile,
        scratch_shapes=[pltpu.VMEM((2, sort_rows, D // 2), _U32), pltpu.SemaphoreType.DMA((2,))],
    )
    return pl.pallas_call(
        functools.partial(_combine_kernel, final_norm=final_norm),
        grid_spec=grid_spec,
        out_shape=jax.ShapeDtypeStruct(x2.shape, x2.dtype),
        compiler_params=pltpu.CompilerParams(
            dimension_semantics=("arbitrary",), vmem_limit_bytes=VMEM_LIMIT),
        name="moe_combine",
    )(tab, x2, meta, g_final, ys)


def _moe(layer, moe_layer, x2, g, w_r, b_r, w_gu, w_down, g_final, final_norm):
    T, D = x2.shape
    E = w_gu.shape[0]
    rt = EXPERT_ROW_TILE
    meta, counts, tab, hs = _route(layer, moe_layer, x2, g, w_r, b_r)

    region_tiles = hs.shape[0] // (E * rt)
    max_rows = TOP_K * T + (T // ROUTE_TILE) * E * (SUBLANES - 1)
    n_tiles = -(-max_rows // rt) + E
    counts = counts[0, :E].astype(jnp.int32)
    group_tiles = (counts + rt - 1) // rt
    tile_end = jnp.cumsum(group_tiles)
    tile_ids = jnp.arange(n_tiles, dtype=jnp.int32)
    tile_expert = jnp.minimum(
        jnp.sum((tile_ids[:, None] >= tile_end[None, :]).astype(jnp.int32), axis=1), E - 1)
    tile_valid = (tile_ids < tile_end[E - 1]).astype(jnp.int32)
    in_group = tile_ids - (tile_end - group_tiles)[tile_expert]
    tile_block = jnp.where(tile_valid != 0, tile_expert * region_tiles + in_group, E * region_tiles)

    tile_rows = jnp.clip(counts[tile_expert] - in_group * rt, 0, rt) * tile_valid
    ys = _experts(tile_expert, tile_block.astype(jnp.int32), tile_rows.astype(jnp.int32), hs, w_gu, w_down)
    return _combine(tab.reshape(tab.shape[0], tab.shape[2]), x2, meta, ys, g_final, final_norm)


def _final_norm_kernel(x_ref, g_ref, o_ref):
    o_ref[...] = _rms(x_ref[...], g_ref[...])


def _final_norm(x2, g):
    T, D = x2.shape
    tm = TOK_TILE
    tile = pl.BlockSpec((tm, D), lambda i: (i, 0))
    return pl.pallas_call(
        _final_norm_kernel,
        grid=(T // tm,),
        in_specs=[tile, pl.BlockSpec((1, D), lambda i: (0, 0))],
        out_specs=tile,
        out_shape=jax.ShapeDtypeStruct(x2.shape, x2.dtype),
        compiler_params=pltpu.CompilerParams(dimension_semantics=("parallel",)),
        name="final_norm",
    )(x2, g)


def kernel(x, mem, g_mix, w_in, w_pool, pool_scale, w_dw, b_dw, conv_ln_g, conv_ln_b, w_conv_out, w_mix_out, g_xattn, g_mem, w_xq, w_xkv, w_xo, g_ffn, w_ffn_gu, w_ffn_down, w_router, b_router, w_moe_gu, w_moe_down, g_final):
    B, S, D = x.shape
    depth = g_mix.shape[0]
    bf = lambda a: a.astype(_BF16)
    rows = lambda a: a.reshape(a.shape[0], 1, a.shape[1])

    g_mix, pool_scale, b_dw, conv_ln_g, conv_ln_b = map(rows, (g_mix, pool_scale, b_dw, conv_ln_g, conv_ln_b))
    g_xattn, g_mem, g_ffn = map(rows, (g_xattn, g_mem, g_ffn))
    w_dw = w_dw.reshape(depth, CONV_KERNEL, CONV_WIDTH)
    w_in, w_pool, w_conv_out, w_mix_out = map(bf, (w_in, w_pool, w_conv_out, w_mix_out))
    w_xq, w_xkv, w_xo = map(bf, (w_xq, w_xkv, w_xo))
    w_ffn_gu, w_ffn_down = map(bf, (w_ffn_gu, w_ffn_down))
    w_r_pad = jnp.pad(w_router, ((0, 0), (0, 0), (0, LANES - N_EXPERTS)))
    b_r_pad = rows(jnp.pad(b_router, ((0, 0), (0, LANES - N_EXPERTS)), constant_values=NEG_BIG))
    g_final = g_final.reshape(1, D)

    half_slices = (w_moe_gu.shape[1] * CAST_PARTS) // 2
    k_all, v_all = _kv_proj(mem, g_mem, w_xkv)
    for i in range(depth):
        last = i == depth - 1
        mixer_args = (i, x, g_mix, w_in, w_pool, pool_scale, w_dw, b_dw, conv_ln_g, conv_ln_b,
                      w_conv_out, w_mix_out)
        if i % 2 == 0 and not last:
            job = lambda first, gu, down: _CastJob((i + 1) // 2, first, w_moe_gu, w_moe_down, gu, down)
            x, moe_gu_bf, moe_down_bf = _mixer(*mixer_args, job(0, None, None))
            x = _xattn(i, x, g_xattn, w_xq, k_all, v_all, w_xo)
            x2, moe_gu_bf, moe_down_bf = _ffn(i, i // 2, x.reshape(B * S, D), g_ffn, w_ffn_gu, w_ffn_down,
                                              job(half_slices, moe_gu_bf, moe_down_bf))
        else:
            x = _mixer(*mixer_args)
            x = _xattn(i, x, g_xattn, w_xq, k_all, v_all, w_xo)
            if i % 2 == 0:
                x2 = _final_norm(_ffn(i, i // 2, x.reshape(B * S, D), g_ffn, w_ffn_gu, w_ffn_down), g_final)
            else:
                x2 = _moe(i, i // 2, x.reshape(B * S, D), g_ffn, w_r_pad, b_r_pad, moe_gu_bf, moe_down_bf,
                          g_final, last)
        x = x2.reshape(B, S, D)
    return x
```

```python
import functools
from typing import NamedTuple, Optional

import jax
import jax.numpy as jnp
import numpy as np
from jax import lax
from jax.experimental import pallas as pl
from jax.experimental.pallas import tpu as pltpu

D_MODEL = 1024
POOL_WIDTH = 512
POOL_GROUPS = 4
POOL_WINDOWS = (2, 4, 8, 16)
POOL_GROUP_IN = POOL_WIDTH // POOL_GROUPS
POOL_GROUP_OUT = D_MODEL // POOL_GROUPS
CONV_WIDTH = 512
CONV_KERNEL = 31
IN_COLS = POOL_WIDTH + 2 * CONV_WIDTH + 2 * D_MODEL
N_XHEADS = 4
XHEAD_DIM = D_MODEL // N_XHEADS
D_FF = 2816
N_EXPERTS = 8
TOP_K = 2
EPS = 1e-6

COL_POOL = 0
COL_GLU_A = POOL_WIDTH
COL_GLU_B = POOL_WIDTH + CONV_WIDTH
COL_GATE_POOL = POOL_WIDTH + 2 * CONV_WIDTH
COL_GATE_CONV = COL_GATE_POOL + D_MODEL

LANES = 128
SUBLANES = 8
POOL_HALO = 16
CONV_HALO = 32
SEQ_TILE = 512
XATTN_TILE = 1024
TOK_TILE = 512
MXU_DIM = 256
FF_CHUNKS = ((0, 6 * MXU_DIM), (6 * MXU_DIM, 5 * MXU_DIM))
assert sum(n for _, n in FF_CHUNKS) == D_FF
CONV_ROW_CHUNK = 64
ROUTE_TILE = 512
EXPERT_ROW_TILE = 512
GROUP_WINDOW = 256
CAST_PARTS = 8
VMEM_LIMIT = 56 * 1024 * 1024
NEG_BIG = -1e30
META_G1, META_G2, META_L1, META_L2 = range(4)

_F32 = jnp.float32
_BF16 = jnp.bfloat16
_U32 = jnp.uint32
HIGH_HALF_MASK = np.uint32(0xFFFF0000)


def _layer_spec(arr, layer):
    tail = (0,) * (arr.ndim - 1)
    return pl.BlockSpec((None,) + arr.shape[1:], lambda *_: (layer,) + tail,
                        pipeline_mode=pl.Buffered(1))


def _rms(x, g):
    return x * lax.rsqrt(jnp.mean(x * x, axis=-1, keepdims=True) + EPS) * g


def _dot(a, b):
    return jnp.dot(a, b, preferred_element_type=_F32)


def _zero_like_bits(v):
    u = lax.bitcast_convert_type(v, _U32)
    return lax.bitcast_convert_type((u >> 16) >> 16, _F32)


def _mixer_kernel(x_ref, g_ref, w_in_ref, w_pool_ref, pscale_ref, w_dw_ref, b_dw_ref,
                  ln_g_ref, ln_b_ref, w_pw_ref, w_out_ref, *rest, cast):
    if cast is not None:
        rest, cast_slice = _split_cast_refs(rest, 1, cast)
        cast_slice()
    o_ref, up_ext, cv_ext, cv_shift, cv_out, mix_ref, gate_ref = rest
    s = pl.program_id(1)
    ts = x_ref.shape[0]

    @pl.when(s == 0)
    def _():
        up_ext[0:POOL_HALO, :] = jnp.zeros((POOL_HALO, POOL_WIDTH), _F32)
        cv_ext[:, 0:CONV_HALO, :] = jnp.zeros((cv_ext.shape[0], CONV_HALO, LANES), _F32)

    @pl.when(s > 0)
    def _():
        up_ext[0:POOL_HALO, :] = up_ext[ts:ts + POOL_HALO, :]
        cv_ext[:, 0:CONV_HALO, :] = cv_ext[:, ts:ts + CONV_HALO, :]

    x = x_ref[...]
    h = _rms(x, g_ref[...]).astype(_BF16)

    glu_a = _dot(h, w_in_ref[:, COL_GLU_A:COL_GLU_A + CONV_WIDTH])
    glu_b = _dot(h, w_in_ref[:, COL_GLU_B:COL_GLU_B + CONV_WIDTH])
    glu = glu_a * jax.nn.sigmoid(glu_b)
    for blk in range(cv_ext.shape[0]):
        cv_ext[blk, CONV_HALO:CONV_HALO + ts, :] = glu[:, blk * LANES:(blk + 1) * LANES]

    shift_rows = cv_shift.shape[2]
    for r in range(1, SUBLANES):
        cv_shift[r - 1] = cv_ext[:, r:r + shift_rows, :]

    tap0 = CONV_HALO - (CONV_KERNEL - 1)
    def side_task(col0, store):
        res = _dot(h, w_in_ref[:, col0:col0 + MXU_DIM])
        store(res)
        return _zero_like_bits(res[ts - 1:ts, MXU_DIM - LANES:])

    def pool_store(c):
        def store(res):
            up_ext[POOL_HALO:POOL_HALO + ts, c:c + MXU_DIM] = res
        return store

    def gate_store(c):
        def store(res):
            gate_ref[:, c:c + MXU_DIM] = jax.nn.sigmoid(res)
        return store

    side_tasks = ([(COL_POOL + c, pool_store(c)) for c in range(0, POOL_WIDTH, MXU_DIM)]
                  + [(COL_GATE_POOL + c, gate_store(c)) for c in range(0, 2 * D_MODEL, MXU_DIM)])
    unit_rows = 2 * CONV_ROW_CHUNK
    units = [(c0, q0) for c0 in range(0, CONV_WIDTH, LANES) for q0 in range(0, ts, unit_rows)]
    assert len(side_tasks) <= len(units)
    pace = None
    for u, (c0, q0) in enumerate(units):
        bias = b_dw_ref[:, c0:c0 + LANES]
        if pace is not None:
            bias = bias + pace
        pace = side_task(*side_tasks[u]) if u < len(side_tasks) else None
        for r0 in range(q0, q0 + unit_rows, CONV_ROW_CHUNK):
            acc = jnp.broadcast_to(bias, (CONV_ROW_CHUNK, LANES))
            for k in range(CONV_KERNEL):
                off = tap0 + k
                mis = off % SUBLANES
                row = r0 + off - mis
                if mis == 0:
                    win = cv_ext[c0 // LANES, row:row + CONV_ROW_CHUNK, :]
                else:
                    win = cv_shift[mis - 1, c0 // LANES, row:row + CONV_ROW_CHUNK, :]
                acc = acc + w_dw_ref[k:k + 1, c0:c0 + LANES] * win
            cv_out[r0:r0 + CONV_ROW_CHUNK, c0:c0 + LANES] = acc

    pos = (s * ts + 1 + lax.broadcasted_iota(jnp.int32, (ts, 1), 0)).astype(_F32)
    for g, w in enumerate(POOL_WINDOWS):
        c0 = g * POOL_GROUP_IN
        win = up_ext[:, c0:c0 + POOL_GROUP_IN]
        span = 1
        while span < w:
            win = win + pltpu.roll(win, span, 0)
            span *= 2
        cur = up_ext[POOL_HALO:POOL_HALO + ts, c0:c0 + POOL_GROUP_IN]
        pooled = win[POOL_HALO:, :] / jnp.minimum(pos, float(w)) - cur
        d0 = g * POOL_GROUP_OUT
        mix_ref[:, d0:d0 + POOL_GROUP_OUT] = (
            _dot(pooled.astype(_BF16), w_pool_ref[g]) * pscale_ref[:, d0:d0 + POOL_GROUP_OUT])

    u = cv_out[...]
    mu = jnp.mean(u, axis=-1, keepdims=True)
    uc = u - mu
    var = jnp.mean(uc * uc, axis=-1, keepdims=True)
    un = uc * lax.rsqrt(var + EPS) * ln_g_ref[...] + ln_b_ref[...]
    un = un * jax.nn.sigmoid(un)
    y_conv = _dot(un.astype(_BF16), w_pw_ref[...])
    mix = gate_ref[:, 0:D_MODEL] * mix_ref[...] + gate_ref[:, D_MODEL:] * y_conv

    o_ref[...] = x + _dot(mix.astype(_BF16), w_out_ref[...])


def _mixer(layer, x, g, w_in, w_pool, pscale, w_dw, b_dw, ln_g, ln_b, w_pw, w_out, job=None):
    B, S, D = x.shape
    ts = SEQ_TILE
    n_seq = S // ts
    shift_rows = ts + CONV_HALO - SUBLANES
    tile = pl.BlockSpec((None, ts, D), lambda b, s: (b, s, 0))
    params = (g, w_in, w_pool, pscale, w_dw, b_dw, ln_g, ln_b, w_pw, w_out)
    in_specs = [tile] + [_layer_spec(p, layer) for p in params]
    args = [x, *params]
    out_specs = [tile]
    out_shape = [jax.ShapeDtypeStruct(x.shape, x.dtype)]
    aliases, cast = {}, None
    if job is not None:
        assert B * n_seq == _cast_slices(job) // 2
        extra = _cast_plumbing(job, lambda b, s: b * n_seq + s, len(in_specs), len(out_specs))
        in_specs += extra[0]
        args += extra[1]
        out_specs += extra[2]
        out_shape += extra[3]
        aliases, cast = extra[4], job.gu_bf is not None
    out = pl.pallas_call(
        functools.partial(_mixer_kernel, cast=cast),
        grid=(B, n_seq),
        in_specs=in_specs,
        out_specs=out_specs,
        out_shape=out_shape,
        input_output_aliases=aliases,
        scratch_shapes=[pltpu.VMEM((POOL_HALO + ts, POOL_WIDTH), _F32),
                        pltpu.VMEM((CONV_WIDTH // LANES, CONV_HALO + ts, LANES), _F32),
                        pltpu.VMEM((SUBLANES - 1, CONV_WIDTH // LANES, shift_rows, LANES), _F32),
                        pltpu.VMEM((ts, CONV_WIDTH), _F32),
                        pltpu.VMEM((ts, D), _F32),
                        pltpu.VMEM((ts, 2 * D), _F32)],
        compiler_params=pltpu.CompilerParams(
            dimension_semantics=("parallel", "arbitrary"), vmem_limit_bytes=VMEM_LIMIT),
        name="mixer",
    )(*args)
    return out[0] if job is None else tuple(out)


def _kv_kernel(m_ref, g_ref, w_ref, k_ref, v_ref):
    m = _rms(m_ref[...], g_ref[...]).astype(_BF16)
    k_ref[...] = (_dot(m, w_ref[:, 0:D_MODEL]) * (XHEAD_DIM ** -0.5)).astype(_BF16)
    v_ref[...] = _dot(m, w_ref[:, D_MODEL:2 * D_MODEL]).astype(_BF16)


def _kv_proj(mem, g, w_kv):
    B, M, D = mem.shape
    L = w_kv.shape[0]
    rows = B * M
    out_blk = pl.BlockSpec((None, rows, D), lambda l: (l, 0, 0))
    k, v = pl.pallas_call(
        _kv_kernel,
        grid=(L,),
        in_specs=[pl.BlockSpec((rows, D), lambda l: (0, 0)),
                  pl.BlockSpec((None,) + g.shape[1:], lambda l: (l, 0, 0)),
                  pl.BlockSpec((None,) + w_kv.shape[1:], lambda l: (l, 0, 0))],
        out_specs=[out_blk, out_blk],
        out_shape=[jax.ShapeDtypeStruct((L, rows, D), _BF16)] * 2,
        compiler_params=pltpu.CompilerParams(
            dimension_semantics=("parallel",), vmem_limit_bytes=VMEM_LIMIT),
        name="kv_proj",
    )(mem.reshape(rows, D), g, w_kv)
    return k.reshape(L, B, M, D), v.reshape(L, B, M, D)


class _CastJob(NamedTuple):
    moe_layer: int
    first_slice: int
    w_gu: jax.Array
    w_down: jax.Array
    gu_bf: Optional[jax.Array]
    down_bf: Optional[jax.Array]


def _cast_slices(job):
    return job.w_gu.shape[1] * CAST_PARTS


def _cast_plumbing(job, step_of, n_in, n_out):
    _, _, d, f2 = job.w_gu.shape
    f = job.w_down.shape[2]

    def part(*grid_idx):
        q = jnp.minimum(job.first_slice + step_of(*grid_idx), _cast_slices(job) - 1)
        return q // CAST_PARTS, q % CAST_PARTS

    def src_idx(*grid_idx):
        return (job.moe_layer,) + part(*grid_idx) + (0,)

    def dst_idx(*grid_idx):
        return part(*grid_idx) + (0,)

    in_specs = [pl.BlockSpec((None, None, d // CAST_PARTS, f2), src_idx),
                pl.BlockSpec((None, None, f // CAST_PARTS, d), src_idx)]
    out_specs = [pl.BlockSpec((None, d // CAST_PARTS, f2), dst_idx),
                 pl.BlockSpec((None, f // CAST_PARTS, d), dst_idx)]
    args, aliases = [job.w_gu, job.w_down], {}
    if job.gu_bf is not None:
        in_specs += [pl.BlockSpec(memory_space=pl.ANY)] * 2
        args += [job.gu_bf, job.down_bf]
        aliases = {n_in + 2: n_out, n_in + 3: n_out + 1}
    out_shapes = [jax.ShapeDtypeStruct(job.w_gu.shape[1:], _BF16),
                  jax.ShapeDtypeStruct(job.w_down.shape[1:], _BF16)]
    return in_specs, args, out_specs, out_shapes, aliases


def _split_cast_refs(rest, n_out, cast_fill):
    n_cast_in = 4 if cast_fill else 2
    gu_in, down_in = rest[0], rest[1]
    host_out = rest[n_cast_in:n_cast_in + n_out]
    gu_out, down_out = rest[n_cast_in + n_out], rest[n_cast_in + n_out + 1]

    def cast():
        gu_out[...] = gu_in[...].astype(_BF16)
        down_out[...] = down_in[...].astype(_BF16)

    return tuple(host_out) + tuple(rest[n_cast_in + n_out + 2:]), cast


def _xattn_kernel(x_ref, g_ref, wq_ref, k_ref, v_ref, wo_ref, o_ref, att_ref):
    x = x_ref[...]
    h = _rms(x, g_ref[...]).astype(_BF16)
    q = _dot(h, wq_ref[...]).astype(_BF16)
    for hd in range(N_XHEADS):
        c0 = hd * XHEAD_DIM
        sc = lax.dot_general(q[:, c0:c0 + XHEAD_DIM], k_ref[:, c0:c0 + XHEAD_DIM],
                             (((1,), (1,)), ((), ())), preferred_element_type=_F32)
        p = jnp.exp(sc - jnp.max(sc, axis=-1, keepdims=True))
        denom = jnp.sum(p, axis=-1, keepdims=True)
        att = _dot(p.astype(_BF16), v_ref[:, c0:c0 + XHEAD_DIM]) / denom
        att_ref[:, c0:c0 + XHEAD_DIM] = att.astype(_BF16)
    o_ref[...] = x + _dot(att_ref[...], wo_ref[...])


def _xattn(layer, x, g, w_q, k, v, w_o):
    B, S, D = x.shape
    M = k.shape[2]
    ts = XATTN_TILE
    tile = pl.BlockSpec((None, ts, D), lambda b, s: (b, s, 0))
    kv_blk = pl.BlockSpec((None, None, M, D), lambda b, s: (layer, b, 0, 0))
    return pl.pallas_call(
        _xattn_kernel,
        grid=(B, S // ts),
        in_specs=[tile, _layer_spec(g, layer), _layer_spec(w_q, layer), kv_blk, kv_blk,
                  _layer_spec(w_o, layer)],
        out_specs=tile,
        out_shape=jax.ShapeDtypeStruct(x.shape, x.dtype),
        scratch_shapes=[pltpu.VMEM((ts, D), _BF16)],
        compiler_params=pltpu.CompilerParams(
            dimension_semantics=("parallel", "parallel"), vmem_limit_bytes=VMEM_LIMIT),
        name="xattn",
    )(x, g, w_q, k, v, w_o)


def _ffn_kernel(x_ref, g_ref, w_gu_ref, w_down_ref, *rest, cast):
    if cast is not None:
        rest, cast_slice = _split_cast_refs(rest, 1, cast)
        cast_slice()
    (o_ref,) = rest
    x = x_ref[...]
    h = _rms(x, g_ref[...]).astype(_BF16)
    acc = x
    for c0, n in FF_CHUNKS:
        gate = _dot(h, w_gu_ref[:, c0:c0 + n])
        up = _dot(h, w_gu_ref[:, D_FF + c0:D_FF + c0 + n])
        act = (gate * jax.nn.sigmoid(gate) * up).astype(_BF16)
        acc = acc + _dot(act, w_down_ref[c0:c0 + n, :])
    o_ref[...] = acc


def _ffn(layer, ffn_layer, x2, g, w_gu, w_down, job=None):
    T, D = x2.shape
    tm = TOK_TILE
    tile = pl.BlockSpec((tm, D), lambda i: (i, 0))
    in_specs = [tile, _layer_spec(g, layer), _layer_spec(w_gu, ffn_layer),
                _layer_spec(w_down, ffn_layer)]
    args = [x2, g, w_gu, w_down]
    out_specs = [tile]
    out_shape = [jax.ShapeDtypeStruct(x2.shape, x2.dtype)]
    aliases, cast = {}, None
    if job is not None:
        assert T // tm == (job.w_gu.shape[1] * CAST_PARTS) // 2
        extra = _cast_plumbing(job, lambda i: i, len(in_specs), len(out_specs))
        in_specs += extra[0]
        args += extra[1]
        out_specs += extra[2]
        out_shape += extra[3]
        aliases, cast = extra[4], job.gu_bf is not None
    out = pl.pallas_call(
        functools.partial(_ffn_kernel, cast=cast),
        grid=(T // tm,),
        in_specs=in_specs,
        out_specs=out_specs,
        out_shape=out_shape,
        input_output_aliases=aliases,
        compiler_params=pltpu.CompilerParams(
            dimension_semantics=("parallel",), vmem_limit_bytes=VMEM_LIMIT),
        name="ffn",
    )(*args)
    return out[0] if job is None else tuple(out)


def _pack_halves(y):
    half = y.shape[1] // 2
    lo = lax.bitcast_convert_type(y[:, :half].astype(_BF16).astype(_F32), _U32)
    hi = lax.bitcast_convert_type(y[:, half:].astype(_BF16).astype(_F32), _U32)
    return (lo >> 16) | (hi & HIGH_HALF_MASK)


def _unpack_halves(p):
    lo = lax.bitcast_convert_type(p << 16, _F32)
    hi = lax.bitcast_convert_type(p & HIGH_HALF_MASK, _F32)
    return lo, hi


def _route_kernel(x_ref, g_ref, w_r_ref, b_r_ref, earlier_ref, lane_before_ref,
                  meta_ref, cnt_ref, tab_ref, hs_ref,
                  carry_ref, placed_smem, second_smem, sorted_buf, sem,
                  *, region_rows, sort_rows):
    i = pl.program_id(0)
    tm = x_ref.shape[0]

    @pl.when(i == 0)
    def _():
        carry_ref[...] = jnp.zeros_like(carry_ref)
        for e in range(N_EXPERTS):
            placed_smem[e] = 0
        sorted_buf[:, sort_rows:, :] = jnp.zeros(
            (sorted_buf.shape[0], sorted_buf.shape[1] - sort_rows, sorted_buf.shape[2]), _U32)

    hf = _rms(x_ref[...], g_ref[...])

    lane = lax.broadcasted_iota(jnp.int32, (tm, LANES), 1)
    h_hi = hf.astype(_BF16)
    h_lo = (hf - h_hi.astype(_F32)).astype(_BF16)
    w_r = w_r_ref[...]
    w_hi = w_r.astype(_BF16)
    w_lo = (w_r - w_hi.astype(_F32)).astype(_BF16)
    hi_terms = _dot(h_hi, jnp.concatenate([w_hi, w_lo], axis=1))
    logits = hi_terms[:, :LANES] + hi_terms[:, LANES:] + _dot(h_lo, w_hi) + b_r_ref[...]
    v1 = jnp.max(logits, axis=-1, keepdims=True)
    i1 = jnp.min(jnp.where(logits == v1, lane, LANES), axis=-1, keepdims=True)
    rest = jnp.where(lane == i1, NEG_BIG, logits)
    v2 = jnp.max(rest, axis=-1, keepdims=True)
    i2 = jnp.min(jnp.where(rest == v2, lane, LANES), axis=-1, keepdims=True)
    z = jnp.exp(v2 - v1)
    g1 = 1.0 / (1.0 + z)
    g2 = z / (1.0 + z)

    sel1 = lane == i1
    sel2 = lane == i2
    chosen = jnp.where(sel1 | sel2, 1.0, 0.0)
    before = _dot(earlier_ref[...], chosen.astype(_BF16))
    cnt = jnp.sum(chosen, axis=0, keepdims=True)
    cnt_pad = jnp.ceil(cnt * (1.0 / SUBLANES)) * SUBLANES
    group_off = _dot(jnp.broadcast_to(cnt_pad, (SUBLANES, LANES)).astype(_BF16),
                     lane_before_ref[...])[0:1, :]
    local = group_off + before
    loc1 = jnp.sum(jnp.where(sel1, local, 0.0), axis=-1, keepdims=True)
    loc2 = jnp.sum(jnp.where(sel2, local, 0.0), axis=-1, keepdims=True)

    slot = lax.rem(i, 2)
    loc_t = jnp.where(lane == 0, loc1, jnp.where(lane == 1, loc2, -1.0)).T
    row = lax.broadcasted_iota(jnp.int32, (sort_rows, tm), 0).astype(_F32)
    perm = jnp.where((row == loc_t[0:1, :]) | (row == loc_t[1:2, :]), 1.0, 0.0).astype(_BF16)
    sorted_buf[slot, 0:sort_rows, :] = _pack_halves(_dot(perm, h_hi))

    def window(src, dst):
        return pltpu.make_async_copy(sorted_buf.at[slot, pl.ds(src, GROUP_WINDOW)],
                                     hs_ref.at[pl.ds(dst, GROUP_WINDOW)], sem)

    def wait_windows(second_flags):
        for e in range(N_EXPERTS):
            window(0, 0).wait()

            @pl.when(second_flags(e) != 0)
            def _():
                window(0, 0).wait()

    @pl.when(i > 0)
    def _():
        wait_windows(lambda e: second_smem[e])

    off_i = group_off.astype(jnp.int32)
    cnt_i = cnt_pad.astype(jnp.int32)
    for e in range(N_EXPERTS):
        src = pl.multiple_of(off_i[0, e], SUBLANES)
        start = pl.multiple_of(e * region_rows + placed_smem[e], SUBLANES)
        rows = cnt_i[0, e]
        second = (rows > GROUP_WINDOW).astype(jnp.int32)
        window(src, start).start()

        @pl.when(second != 0)
        def _():
            window(src + GROUP_WINDOW, start + GROUP_WINDOW).start()

        second_smem[e] = second
        placed_smem[e] = placed_smem[e] + rows
        tab_ref[0, e] = start
        tab_ref[0, N_EXPERTS + e] = rows
    carry_ref[...] += cnt_pad
    cnt_ref[...] = carry_ref[...]

    @pl.when(i == pl.num_programs(0) - 1)
    def _():
        wait_windows(lambda e: second_smem[e])

        def tail(e, part):
            dst = pl.multiple_of(e * region_rows + placed_smem[e], SUBLANES) + part * GROUP_WINDOW
            return window(sort_rows, dst)

        for e in range(N_EXPERTS):
            for part in range(EXPERT_ROW_TILE // GROUP_WINDOW):
                tail(e, part).start()
        for e in range(N_EXPERTS):
            for part in range(EXPERT_ROW_TILE // GROUP_WINDOW):
                tail(e, part).wait()

    meta = jnp.where(lane == META_G1, g1, 0.0)
    meta = jnp.where(lane == META_G2, g2, meta)
    meta = jnp.where(lane == META_L1, loc1, meta)
    meta = jnp.where(lane == META_L2, loc2, meta)
    meta_ref[...] = meta


def _region_rows(n_tokens):
    pad = (n_tokens // ROUTE_TILE) * (SUBLANES - 1) + 2 * EXPERT_ROW_TILE
    return -(-(n_tokens + pad) // EXPERT_ROW_TILE) * EXPERT_ROW_TILE


def _route(layer, moe_layer, x2, g, w_r, b_r):
    T, D = x2.shape
    tm = ROUTE_TILE
    region_rows = _region_rows(T)
    sort_rows = TOP_K * tm + N_EXPERTS * SUBLANES
    earlier = jnp.asarray(np.tril(np.ones((tm, tm), np.float32), -1), _BF16)
    lane_before = jnp.asarray(np.triu(np.ones((LANES, LANES), np.float32), 1), _BF16)
    const = lambda n: pl.BlockSpec((n, n), lambda i: (0, 0), pipeline_mode=pl.Buffered(1))
    return pl.pallas_call(
        functools.partial(_route_kernel, region_rows=region_rows, sort_rows=sort_rows),
        grid=(T // tm,),
        in_specs=[pl.BlockSpec((tm, D), lambda i: (i, 0)), _layer_spec(g, layer),
                  _layer_spec(w_r, moe_layer), _layer_spec(b_r, moe_layer),
                  const(tm), const(LANES)],
        out_specs=[pl.BlockSpec((tm, LANES), lambda i: (i, 0)),
                   pl.BlockSpec((1, LANES), lambda i: (0, 0)),
                   pl.BlockSpec((None, 1, 2 * N_EXPERTS), lambda i: (i, 0, 0),
                                memory_space=pltpu.SMEM),
                   pl.BlockSpec(memory_space=pl.ANY)],
        out_shape=[jax.ShapeDtypeStruct((T, LANES), _F32),
                   jax.ShapeDtypeStruct((1, LANES), _F32),
                   jax.ShapeDtypeStruct((T // tm, 1, 2 * N_EXPERTS), jnp.int32),
                   jax.ShapeDtypeStruct((N_EXPERTS * region_rows, D // 2), _U32)],
        scratch_shapes=[pltpu.VMEM((1, LANES), _F32),
                        pltpu.SMEM((N_EXPERTS,), jnp.int32),
                        pltpu.SMEM((N_EXPERTS,), jnp.int32),
                        pltpu.VMEM((2, sort_rows + GROUP_WINDOW, D // 2), _U32),
                        pltpu.SemaphoreType.DMA],
        compiler_params=pltpu.CompilerParams(
            dimension_semantics=("arbitrary",), vmem_limit_bytes=VMEM_LIMIT),
        name="moe_route",
    )(x2, g, w_r, b_r, earlier, lane_before)


def _expert_kernel(te_ref, tb_ref, tv_ref, hs_ref, w_gu_ref, w_down_ref, ys_ref):
    del te_ref, tb_ref
    i = pl.program_id(0)

    rt = ys_ref.shape[0]
    half = rt // 2
    rows_used = tv_ref[i]

    def swiglu_rows(n_rows):
        lo, hi = _unpack_halves(hs_ref[0:n_rows, :])
        h = jnp.concatenate([lo.astype(_BF16), hi.astype(_BF16)], axis=1)
        acc = None
        for c0, n in FF_CHUNKS:
            gate = _dot(h, w_gu_ref[:, c0:c0 + n])
            up = _dot(h, w_gu_ref[:, D_FF + c0:D_FF + c0 + n])
            act = (gate * jax.nn.sigmoid(gate) * up).astype(_BF16)
            part = _dot(act, w_down_ref[c0:c0 + n, :])
            acc = part if acc is None else acc + part
        ys_ref[0:n_rows, :] = _pack_halves(acc)

    @pl.when(rows_used == 0)
    def _():
        ys_ref[...] = jnp.zeros_like(ys_ref)

    @pl.when((rows_used > 0) & (rows_used <= half))
    def _():
        swiglu_rows(half)
        ys_ref[half:, :] = jnp.zeros((rt - half, ys_ref.shape[1]), ys_ref.dtype)

    @pl.when(rows_used > half)
    def _():
        swiglu_rows(rt)


def _experts(tile_expert, tile_block, tile_rows, hs, w_gu, w_down):
    n_rows, W = hs.shape
    rt = EXPERT_ROW_TILE
    D = w_gu.shape[1]
    grid_spec = pltpu.PrefetchScalarGridSpec(
        num_scalar_prefetch=3,
        grid=(tile_expert.shape[0],),
        in_specs=[pl.BlockSpec((rt, W), lambda i, te, tb, tv: (jnp.where(tv[i] > 0, tb[i], 0), 0)),
                  pl.BlockSpec((None, D, 2 * D_FF), lambda i, te, tb, tv: (te[i], 0, 0)),
                  pl.BlockSpec((None, D_FF, D), lambda i, te, tb, tv: (te[i], 0, 0))],
        out_specs=pl.BlockSpec((rt, W), lambda i, te, tb, tv: (tb[i], 0)),
    )
    return pl.pallas_call(
        _expert_kernel,
        grid_spec=grid_spec,
        out_shape=jax.ShapeDtypeStruct((n_rows + rt, W), hs.dtype),
        compiler_params=pltpu.CompilerParams(
            dimension_semantics=("arbitrary",), vmem_limit_bytes=VMEM_LIMIT),
        name="moe_experts",
    )(tile_expert, tile_block, tile_rows, hs, w_gu, w_down)


def _combine_kernel(tab_ref, x_ref, meta_ref, g_final_ref, ys_ref, o_ref, ybuf, sem, *, final_norm):
    i = pl.program_id(0)
    n = pl.num_programs(0)
    tm, D = x_ref.shape
    slot = lax.rem(i, 2)

    @pl.when(i == 0)
    def _():
        ybuf[...] = jnp.zeros_like(ybuf)

    def fetch(tile, to_slot):
        off = 0
        for e in range(N_EXPERTS):
            start = tab_ref[tile, e]
            rows = tab_ref[tile, N_EXPERTS + e]
            size = ROUTE_TILE
            while size >= SUBLANES:
                above = rows & ~(2 * size - 1)
                src = pl.multiple_of(start + above, SUBLANES)
                dst = pl.multiple_of(off + above, SUBLANES)

                @pl.when((rows & size) != 0)
                def _():
                    pltpu.make_async_copy(ys_ref.at[pl.ds(src, size)],
                                          ybuf.at[to_slot, pl.ds(dst, size)], sem.at[to_slot]).start()
                size //= 2
            off = off + rows
        return off

    def total_rows(tile):
        total = 0
        for e in range(N_EXPERTS):
            total = total + tab_ref[tile, N_EXPERTS + e]
        return total

    @pl.when(i == 0)
    def _():
        fetch(0, 0)

    @pl.when(i + 1 < n)
    def _():
        fetch(i + 1, 1 - slot)

    landed = pl.multiple_of(total_rows(i), SUBLANES)
    pltpu.make_async_copy(ys_ref.at[pl.ds(0, landed)], ybuf.at[slot, pl.ds(0, landed)],
                          sem.at[slot]).wait()

    lo, hi = _unpack_halves(ybuf[slot])
    y = jnp.concatenate([lo.astype(_BF16), hi.astype(_BF16)], axis=1)
    col = lax.broadcasted_iota(jnp.int32, (tm, ybuf.shape[1]), 1).astype(_F32)
    pick1 = jnp.where(col == meta_ref[:, META_L1:META_L1 + 1], 1.0, 0.0).astype(_BF16)
    pick2 = jnp.where(col == meta_ref[:, META_L2:META_L2 + 1], 1.0, 0.0).astype(_BF16)
    out = (x_ref[...] + meta_ref[:, META_G1:META_G1 + 1] * _dot(pick1, y)
           + meta_ref[:, META_G2:META_G2 + 1] * _dot(pick2, y))
    if final_norm:
        out = _rms(out, g_final_ref[...])
    o_ref[...] = out


def _combine(tab, x2, meta, ys, g_final, final_norm):
    T, D = x2.shape
    tm = ROUTE_TILE
    sort_rows = -(-(TOP_K * tm + N_EXPERTS * SUBLANES) // MXU_DIM) * MXU_DIM
    tile = pl.BlockSpec((tm, D), lambda i, tab: (i, 0))
    grid_spec = pltpu.PrefetchScalarGridSpec(
        num_scalar_prefetch=1,
        grid=(T // tm,),
        in_specs=[tile, pl.BlockSpec((tm, LANES), lambda i, tab: (i, 0)),
                  pl.BlockSpec((1, D), lambda i, tab: (0, 0)), pl.BlockSpec(memory_space=pl.ANY)],
        out_specs=tile,
        scratch_shapes=[pltpu.VMEM((2, sort_rows, D // 2), _U32), pltpu.SemaphoreType.DMA((2,))],
    )
    return pl.pallas_call(
        functools.partial(_combine_kernel, final_norm=final_norm),
        grid_spec=grid_spec,
        out_shape=jax.ShapeDtypeStruct(x2.shape, x2.dtype),
        compiler_params=pltpu.CompilerParams(
            dimension_semantics=("arbitrary",), vmem_limit_bytes=VMEM_LIMIT),
        name="moe_combine",
    )(tab, x2, meta, g_final, ys)


def _moe(layer, moe_layer, x2, g, w_r, b_r, w_gu, w_down, g_final, final_norm):
    T, D = x2.shape
    E = w_gu.shape[0]
    rt = EXPERT_ROW_TILE
    meta, counts, tab, hs = _route(layer, moe_layer, x2, g, w_r, b_r)

    region_tiles = hs.shape[0] // (E * rt)
    max_rows = TOP_K * T + (T // ROUTE_TILE) * E * (SUBLANES - 1)
    n_tiles = -(-max_rows // rt) + E
    counts = counts[0, :E].astype(jnp.int32)
    group_tiles = (counts + rt - 1) // rt
    tile_end = jnp.cumsum(group_tiles)
    tile_ids = jnp.arange(n_tiles, dtype=jnp.int32)
    tile_expert = jnp.minimum(
        jnp.sum((tile_ids[:, None] >= tile_end[None, :]).astype(jnp.int32), axis=1), E - 1)
    tile_valid = (tile_ids < tile_end[E - 1]).astype(jnp.int32)
    own = tile_expert[:, None] == jnp.arange(E, dtype=jnp.int32)[None, :]
    lookup = lambda table: jnp.sum(jnp.where(own, table[None, :], 0), axis=1)
    in_group = tile_ids - lookup(tile_end - group_tiles)
    tile_block = jnp.where(tile_valid != 0, tile_expert * region_tiles + in_group, E * region_tiles)

    tile_rows = jnp.clip(lookup(counts) - in_group * rt, 0, rt) * tile_valid
    ys = _experts(tile_expert, tile_block.astype(jnp.int32), tile_rows.astype(jnp.int32), hs, w_gu, w_down)
    return _combine(tab.reshape(tab.shape[0], tab.shape[2]), x2, meta, ys, g_final, final_norm)


def _final_norm_kernel(x_ref, g_ref, o_ref):
    o_ref[...] = _rms(x_ref[...], g_ref[...])


def _final_norm(x2, g):
    T, D = x2.shape
    tm = TOK_TILE
    tile = pl.BlockSpec((tm, D), lambda i: (i, 0))
    return pl.pallas_call(
        _final_norm_kernel,
        grid=(T // tm,),
        in_specs=[tile, pl.BlockSpec((1, D), lambda i: (0, 0))],
        out_specs=tile,
        out_shape=jax.ShapeDtypeStruct(x2.shape, x2.dtype),
        compiler_params=pltpu.CompilerParams(dimension_semantics=("parallel",)),
        name="final_norm",
    )(x2, g)


def kernel(x, mem, g_mix, w_in, w_pool, pool_scale, w_dw, b_dw, conv_ln_g, conv_ln_b, w_conv_out, w_mix_out, g_xattn, g_mem, w_xq, w_xkv, w_xo, g_ffn, w_ffn_gu, w_ffn_down, w_router, b_router, w_moe_gu, w_moe_down, g_final):
    B, S, D = x.shape
    depth = g_mix.shape[0]
    bf = lambda a: a.astype(_BF16)
    rows = lambda a: a.reshape(a.shape[0], 1, a.shape[1])

    g_mix, pool_scale, b_dw, conv_ln_g, conv_ln_b = map(rows, (g_mix, pool_scale, b_dw, conv_ln_g, conv_ln_b))
    g_xattn, g_mem, g_ffn = map(rows, (g_xattn, g_mem, g_ffn))
    w_dw = w_dw.reshape(depth, CONV_KERNEL, CONV_WIDTH)
    w_in, w_pool, w_conv_out, w_mix_out = map(bf, (w_in, w_pool, w_conv_out, w_mix_out))
    w_xq, w_xkv, w_xo = map(bf, (w_xq, w_xkv, w_xo))
    w_ffn_gu, w_ffn_down = map(bf, (w_ffn_gu, w_ffn_down))
    w_r_pad = jnp.pad(w_router, ((0, 0), (0, 0), (0, LANES - N_EXPERTS)))
    b_r_pad = rows(jnp.pad(b_router, ((0, 0), (0, LANES - N_EXPERTS)), constant_values=NEG_BIG))
    g_final = g_final.reshape(1, D)

    half_slices = (w_moe_gu.shape[1] * CAST_PARTS) // 2
    k_all, v_all = _kv_proj(mem, g_mem, w_xkv)
    for i in range(depth):
        last = i == depth - 1
        mixer_args = (i, x, g_mix, w_in, w_pool, pool_scale, w_dw, b_dw, conv_ln_g, conv_ln_b,
                      w_conv_out, w_mix_out)
        if i % 2 == 0 and not last:
            job = lambda first, gu, down: _CastJob((i + 1) // 2, first, w_moe_gu, w_moe_down, gu, down)
            x, moe_gu_bf, moe_down_bf = _mixer(*mixer_args, job(0, None, None))
            x = _xattn(i, x, g_xattn, w_xq, k_all, v_all, w_xo)
            x2, moe_gu_bf, moe_down_bf = _ffn(i, i // 2, x.reshape(B * S, D), g_ffn, w_ffn_gu, w_ffn_down,
                                              job(half_slices, moe_gu_bf, moe_down_bf))
        else:
            x = _mixer(*mixer_args)
            x = _xattn(i, x, g_xattn, w_xq, k_all, v_all, w_xo)
            if i % 2 == 0:
                x2 = _final_norm(_ffn(i, i // 2, x.reshape(B * S, D), g_ffn, w_ffn_gu, w_ffn_down), g_final)
            else:
                x2 = _moe(i, i // 2, x.reshape(B * S, D), g_ffn, w_r_pad, b_r_pad, moe_gu_bf, moe_down_bf,
                          g_final, last)
        x = x2.reshape(B, S, D)
    return x
```

```python
import functools
from typing import NamedTuple, Optional

import jax
import jax.numpy as jnp
import numpy as np
from jax import lax
from jax.experimental import pallas as pl
from jax.experimental.pallas import tpu as pltpu

D_MODEL = 1024
POOL_WIDTH = 512
POOL_GROUPS = 4
POOL_WINDOWS = (2, 4, 8, 16)
POOL_GROUP_IN = POOL_WIDTH // POOL_GROUPS
POOL_GROUP_OUT = D_MODEL // POOL_GROUPS
CONV_WIDTH = 512
CONV_KERNEL = 31
IN_COLS = POOL_WIDTH + 2 * CONV_WIDTH + 2 * D_MODEL
N_XHEADS = 4
XHEAD_DIM = D_MODEL // N_XHEADS
D_FF = 2816
N_EXPERTS = 8
TOP_K = 2
EPS = 1e-6

COL_POOL = 0
COL_GLU_A = POOL_WIDTH
COL_GLU_B = POOL_WIDTH + CONV_WIDTH
COL_GATE_POOL = POOL_WIDTH + 2 * CONV_WIDTH
COL_GATE_CONV = COL_GATE_POOL + D_MODEL

LANES = 128
SUBLANES = 8
POOL_HALO = 16
CONV_HALO = 32
SEQ_TILE = 256
XATTN_TILE = 1024
TOK_TILE = 512
MXU_DIM = 256
FF_CHUNKS = ((0, 6 * MXU_DIM), (6 * MXU_DIM, 5 * MXU_DIM))
assert sum(n for _, n in FF_CHUNKS) == D_FF
CONV_ROW_CHUNK = 64
ROUTE_TILE = 512
EXPERT_ROW_TILE = 512
GROUP_WINDOW = 256
CAST_PARTS = 8
VMEM_LIMIT = 56 * 1024 * 1024
NEG_BIG = -1e30
META_G1, META_G2, META_L1, META_L2 = range(4)

_F32 = jnp.float32
_BF16 = jnp.bfloat16
_U32 = jnp.uint32
HIGH_HALF_MASK = np.uint32(0xFFFF0000)


def _layer_spec(arr, layer):
    tail = (0,) * (arr.ndim - 1)
    return pl.BlockSpec((None,) + arr.shape[1:], lambda *_: (layer,) + tail,
                        pipeline_mode=pl.Buffered(1))


def _rms(x, g):
    return x * lax.rsqrt(jnp.mean(x * x, axis=-1, keepdims=True) + EPS) * g


def _dot(a, b):
    return jnp.dot(a, b, preferred_element_type=_F32)


def _zero_like_bits(v):
    u = lax.bitcast_convert_type(v, _U32)
    return lax.bitcast_convert_type((u >> 16) >> 16, _F32)


def _mixer_kernel(x_ref, g_ref, w_in_ref, w_pool_ref, pscale_ref, w_dw_ref, b_dw_ref,
                  ln_g_ref, ln_b_ref, w_pw_ref, w_out_ref, *rest, cast):
    if cast is not None:
        rest, cast_slice = _split_cast_refs(rest, 1, cast)
        cast_slice()
    o_ref, up_ext, cv_ext, cv_shift, cv_out, mix_ref, gate_ref = rest
    s = pl.program_id(1)
    ts = x_ref.shape[0]

    @pl.when(s == 0)
    def _():
        up_ext[0:POOL_HALO, :] = jnp.zeros((POOL_HALO, POOL_WIDTH), _F32)
        cv_ext[:, 0:CONV_HALO, :] = jnp.zeros((cv_ext.shape[0], CONV_HALO, LANES), _F32)

    @pl.when(s > 0)
    def _():
        up_ext[0:POOL_HALO, :] = up_ext[ts:ts + POOL_HALO, :]
        cv_ext[:, 0:CONV_HALO, :] = cv_ext[:, ts:ts + CONV_HALO, :]

    x = x_ref[...]
    h = _rms(x, g_ref[...]).astype(_BF16)

    glu_a = _dot(h, w_in_ref[:, COL_GLU_A:COL_GLU_A + CONV_WIDTH])
    glu_b = _dot(h, w_in_ref[:, COL_GLU_B:COL_GLU_B + CONV_WIDTH])
    glu = glu_a * jax.nn.sigmoid(glu_b)
    for blk in range(cv_ext.shape[0]):
        cv_ext[blk, CONV_HALO:CONV_HALO + ts, :] = glu[:, blk * LANES:(blk + 1) * LANES]

    shift_rows = cv_shift.shape[2]
    for r in range(1, SUBLANES):
        cv_shift[r - 1] = cv_ext[:, r:r + shift_rows, :]

    tap0 = CONV_HALO - (CONV_KERNEL - 1)
    def side_task(col0, store):
        res = _dot(h, w_in_ref[:, col0:col0 + MXU_DIM])
        store(res)
        return _zero_like_bits(res[ts - 1:ts, MXU_DIM - LANES:])

    def pool_store(c):
        def store(res):
            up_ext[POOL_HALO:POOL_HALO + ts, c:c + MXU_DIM] = res
        return store

    def gate_store(c):
        def store(res):
            gate_ref[:, c:c + MXU_DIM] = jax.nn.sigmoid(res)
        return store

    side_tasks = ([(COL_POOL + c, pool_store(c)) for c in range(0, POOL_WIDTH, MXU_DIM)]
                  + [(COL_GATE_POOL + c, gate_store(c)) for c in range(0, 2 * D_MODEL, MXU_DIM)])
    unit_rows = CONV_ROW_CHUNK
    units = [(c0, q0) for c0 in range(0, CONV_WIDTH, LANES) for q0 in range(0, ts, unit_rows)]
    assert len(side_tasks) <= len(units)
    pace = None
    for u, (c0, q0) in enumerate(units):
        bias = b_dw_ref[:, c0:c0 + LANES]
        if pace is not None:
            bias = bias + pace
        pace = side_task(*side_tasks[u]) if u < len(side_tasks) else None
        for r0 in range(q0, q0 + unit_rows, CONV_ROW_CHUNK):
            acc = jnp.broadcast_to(bias, (CONV_ROW_CHUNK, LANES))
            for k in range(CONV_KERNEL):
                off = tap0 + k
                mis = off % SUBLANES
                row = r0 + off - mis
                if mis == 0:
                    win = cv_ext[c0 // LANES, row:row + CONV_ROW_CHUNK, :]
                else:
                    win = cv_shift[mis - 1, c0 // LANES, row:row + CONV_ROW_CHUNK, :]
                acc = acc + w_dw_ref[k:k + 1, c0:c0 + LANES] * win
            cv_out[r0:r0 + CONV_ROW_CHUNK, c0:c0 + LANES] = acc

    pos = (s * ts + 1 + lax.broadcasted_iota(jnp.int32, (ts, 1), 0)).astype(_F32)
    for g, w in enumerate(POOL_WINDOWS):
        c0 = g * POOL_GROUP_IN
        win = up_ext[:, c0:c0 + POOL_GROUP_IN]
        span = 1
        while span < w:
            win = win + pltpu.roll(win, span, 0)
            span *= 2
        cur = up_ext[POOL_HALO:POOL_HALO + ts, c0:c0 + POOL_GROUP_IN]
        pooled = win[POOL_HALO:, :] / jnp.minimum(pos, float(w)) - cur
        d0 = g * POOL_GROUP_OUT
        mix_ref[:, d0:d0 + POOL_GROUP_OUT] = (
            _dot(pooled.astype(_BF16), w_pool_ref[g]) * pscale_ref[:, d0:d0 + POOL_GROUP_OUT])

    u = cv_out[...]
    mu = jnp.mean(u, axis=-1, keepdims=True)
    uc = u - mu
    var = jnp.mean(uc * uc, axis=-1, keepdims=True)
    un = uc * lax.rsqrt(var + EPS) * ln_g_ref[...] + ln_b_ref[...]
    un = un * jax.nn.sigmoid(un)
    y_conv = _dot(un.astype(_BF16), w_pw_ref[...])
    mix = gate_ref[:, 0:D_MODEL] * mix_ref[...] + gate_ref[:, D_MODEL:] * y_conv

    o_ref[...] = x + _dot(mix.astype(_BF16), w_out_ref[...])


def _mixer(layer, x, g, w_in, w_pool, pscale, w_dw, b_dw, ln_g, ln_b, w_pw, w_out, job=None):
    B, S, D = x.shape
    ts = SEQ_TILE
    n_seq = S // ts
    shift_rows = ts + CONV_HALO - SUBLANES
    tile = pl.BlockSpec((None, ts, D), lambda b, s: (b, s, 0))
    params = (g, w_in, w_pool, pscale, w_dw, b_dw, ln_g, ln_b, w_pw, w_out)
    in_specs = [tile] + [_layer_spec(p, layer) for p in params]
    args = [x, *params]
    out_specs = [tile]
    out_shape = [jax.ShapeDtypeStruct(x.shape, x.dtype)]
    aliases, cast = {}, None
    if job is not None:
        assert B * n_seq == _cast_slices(job)
        extra = _cast_plumbing(job, lambda b, s: b * n_seq + s, len(in_specs), len(out_specs))
        in_specs += extra[0]
        args += extra[1]
        out_specs += extra[2]
        out_shape += extra[3]
        aliases, cast = extra[4], job.gu_bf is not None
    out = pl.pallas_call(
        functools.partial(_mixer_kernel, cast=cast),
        grid=(B, n_seq),
        in_specs=in_specs,
        out_specs=out_specs,
        out_shape=out_shape,
        input_output_aliases=aliases,
        scratch_shapes=[pltpu.VMEM((POOL_HALO + ts, POOL_WIDTH), _F32),
                        pltpu.VMEM((CONV_WIDTH // LANES, CONV_HALO + ts, LANES), _F32),
                        pltpu.VMEM((SUBLANES - 1, CONV_WIDTH // LANES, shift_rows, LANES), _F32),
                        pltpu.VMEM((ts, CONV_WIDTH), _F32),
                        pltpu.VMEM((ts, D), _F32),
                        pltpu.VMEM((ts, 2 * D), _F32)],
        compiler_params=pltpu.CompilerParams(
            dimension_semantics=("parallel", "arbitrary"), vmem_limit_bytes=VMEM_LIMIT),
        name="mixer",
    )(*args)
    return out[0] if job is None else tuple(out)


def _kv_kernel(m_ref, g_ref, w_ref, k_ref, v_ref):
    m = _rms(m_ref[...], g_ref[...]).astype(_BF16)
    k_ref[...] = (_dot(m, w_ref[:, 0:D_MODEL]) * (XHEAD_DIM ** -0.5)).astype(_BF16)
    v_ref[...] = _dot(m, w_ref[:, D_MODEL:2 * D_MODEL]).astype(_BF16)


def _kv_proj(mem, g, w_kv):
    B, M, D = mem.shape
    L = w_kv.shape[0]
    rows = B * M
    out_blk = pl.BlockSpec((None, rows, D), lambda l: (l, 0, 0))
    k, v = pl.pallas_call(
        _kv_kernel,
        grid=(L,),
        in_specs=[pl.BlockSpec((rows, D), lambda l: (0, 0)),
                  pl.BlockSpec((None,) + g.shape[1:], lambda l: (l, 0, 0)),
                  pl.BlockSpec((None,) + w_kv.shape[1:], lambda l: (l, 0, 0))],
        out_specs=[out_blk, out_blk],
        out_shape=[jax.ShapeDtypeStruct((L, rows, D), _BF16)] * 2,
        compiler_params=pltpu.CompilerParams(
            dimension_semantics=("parallel",), vmem_limit_bytes=VMEM_LIMIT),
        name="kv_proj",
    )(mem.reshape(rows, D), g, w_kv)
    return k.reshape(L, B, M, D), v.reshape(L, B, M, D)


class _CastJob(NamedTuple):
    moe_layer: int
    first_slice: int
    w_gu: jax.Array
    w_down: jax.Array
    gu_bf: Optional[jax.Array]
    down_bf: Optional[jax.Array]


def _cast_slices(job):
    return job.w_gu.shape[1] * CAST_PARTS


def _cast_plumbing(job, step_of, n_in, n_out):
    _, _, d, f2 = job.w_gu.shape
    f = job.w_down.shape[2]

    def part(*grid_idx):
        q = jnp.minimum(job.first_slice + step_of(*grid_idx), _cast_slices(job) - 1)
        return q // CAST_PARTS, q % CAST_PARTS

    def src_idx(*grid_idx):
        return (job.moe_layer,) + part(*grid_idx) + (0,)

    def dst_idx(*grid_idx):
        return part(*grid_idx) + (0,)

    in_specs = [pl.BlockSpec((None, None, d // CAST_PARTS, f2), src_idx),
                pl.BlockSpec((None, None, f // CAST_PARTS, d), src_idx)]
    out_specs = [pl.BlockSpec((None, d // CAST_PARTS, f2), dst_idx),
                 pl.BlockSpec((None, f // CAST_PARTS, d), dst_idx)]
    args, aliases = [job.w_gu, job.w_down], {}
    if job.gu_bf is not None:
        in_specs += [pl.BlockSpec(memory_space=pl.ANY)] * 2
        args += [job.gu_bf, job.down_bf]
        aliases = {n_in + 2: n_out, n_in + 3: n_out + 1}
    out_shapes = [jax.ShapeDtypeStruct(job.w_gu.shape[1:], _BF16),
                  jax.ShapeDtypeStruct(job.w_down.shape[1:], _BF16)]
    return in_specs, args, out_specs, out_shapes, aliases


def _split_cast_refs(rest, n_out, cast_fill):
    n_cast_in = 4 if cast_fill else 2
    gu_in, down_in = rest[0], rest[1]
    host_out = rest[n_cast_in:n_cast_in + n_out]
    gu_out, down_out = rest[n_cast_in + n_out], rest[n_cast_in + n_out + 1]

    def cast():
        gu_out[...] = gu_in[...].astype(_BF16)
        down_out[...] = down_in[...].astype(_BF16)

    return tuple(host_out) + tuple(rest[n_cast_in + n_out + 2:]), cast


def _xattn_kernel(x_ref, g_ref, wq_ref, k_ref, v_ref, wo_ref, o_ref, att_ref):
    x = x_ref[...]
    h = _rms(x, g_ref[...]).astype(_BF16)
    q = _dot(h, wq_ref[...]).astype(_BF16)
    for hd in range(N_XHEADS):
        c0 = hd * XHEAD_DIM
        sc = lax.dot_general(q[:, c0:c0 + XHEAD_DIM], k_ref[:, c0:c0 + XHEAD_DIM],
                             (((1,), (1,)), ((), ())), preferred_element_type=_F32)
        p = jnp.exp(sc - jnp.max(sc, axis=-1, keepdims=True))
        denom = jnp.sum(p, axis=-1, keepdims=True)
        att = _dot(p.astype(_BF16), v_ref[:, c0:c0 + XHEAD_DIM]) / denom
        att_ref[:, c0:c0 + XHEAD_DIM] = att.astype(_BF16)
    o_ref[...] = x + _dot(att_ref[...], wo_ref[...])


def _xattn(layer, x, g, w_q, k, v, w_o):
    B, S, D = x.shape
    M = k.shape[2]
    ts = XATTN_TILE
    tile = pl.BlockSpec((None, ts, D), lambda b, s: (b, s, 0))
    kv_blk = pl.BlockSpec((None, None, M, D), lambda b, s: (layer, b, 0, 0))
    return pl.pallas_call(
        _xattn_kernel,
        grid=(B, S // ts),
        in_specs=[tile, _layer_spec(g, layer), _layer_spec(w_q, layer), kv_blk, kv_blk,
                  _layer_spec(w_o, layer)],
        out_specs=tile,
        out_shape=jax.ShapeDtypeStruct(x.shape, x.dtype),
        scratch_shapes=[pltpu.VMEM((ts, D), _BF16)],
        compiler_params=pltpu.CompilerParams(
            dimension_semantics=("parallel", "parallel"), vmem_limit_bytes=VMEM_LIMIT),
        name="xattn",
    )(x, g, w_q, k, v, w_o)


def _ffn_kernel(x_ref, g_ref, w_gu_ref, w_down_ref, *rest, cast):
    if cast is not None:
        rest, cast_slice = _split_cast_refs(rest, 1, cast)
        cast_slice()
    (o_ref,) = rest
    x = x_ref[...]
    h = _rms(x, g_ref[...]).astype(_BF16)
    acc = x
    for c0, n in FF_CHUNKS:
        gate = _dot(h, w_gu_ref[:, c0:c0 + n])
        up = _dot(h, w_gu_ref[:, D_FF + c0:D_FF + c0 + n])
        act = (gate * jax.nn.sigmoid(gate) * up).astype(_BF16)
        acc = acc + _dot(act, w_down_ref[c0:c0 + n, :])
    o_ref[...] = acc


def _ffn(layer, ffn_layer, x2, g, w_gu, w_down, job=None):
    T, D = x2.shape
    tm = TOK_TILE
    tile = pl.BlockSpec((tm, D), lambda i: (i, 0))
    in_specs = [tile, _layer_spec(g, layer), _layer_spec(w_gu, ffn_layer),
                _layer_spec(w_down, ffn_layer)]
    args = [x2, g, w_gu, w_down]
    out_specs = [tile]
    out_shape = [jax.ShapeDtypeStruct(x2.shape, x2.dtype)]
    aliases, cast = {}, None
    if job is not None:
        assert T // tm == (job.w_gu.shape[1] * CAST_PARTS) // 2
        extra = _cast_plumbing(job, lambda i: i, len(in_specs), len(out_specs))
        in_specs += extra[0]
        args += extra[1]
        out_specs += extra[2]
        out_shape += extra[3]
        aliases, cast = extra[4], job.gu_bf is not None
    out = pl.pallas_call(
        functools.partial(_ffn_kernel, cast=cast),
        grid=(T // tm,),
        in_specs=in_specs,
        out_specs=out_specs,
        out_shape=out_shape,
        input_output_aliases=aliases,
        compiler_params=pltpu.CompilerParams(
            dimension_semantics=("parallel",), vmem_limit_bytes=VMEM_LIMIT),
        name="ffn",
    )(*args)
    return out[0] if job is None else tuple(out)


def _pack_halves(y):
    half = y.shape[1] // 2
    lo = lax.bitcast_convert_type(y[:, :half].astype(_BF16).astype(_F32), _U32)
    hi = lax.bitcast_convert_type(y[:, half:].astype(_BF16).astype(_F32), _U32)
    return (lo >> 16) | (hi & HIGH_HALF_MASK)


def _unpack_halves(p):
    lo = lax.bitcast_convert_type(p << 16, _F32)
    hi = lax.bitcast_convert_type(p & HIGH_HALF_MASK, _F32)
    return lo, hi


def _route_kernel(x_ref, g_ref, w_r_ref, b_r_ref, earlier_ref, lane_before_ref,
                  meta_ref, cnt_ref, tab_ref, hs_ref,
                  carry_ref, placed_smem, second_smem, sorted_buf, sem,
                  *, region_rows, sort_rows):
    i = pl.program_id(0)
    tm = x_ref.shape[0]

    @pl.when(i == 0)
    def _():
        carry_ref[...] = jnp.zeros_like(carry_ref)
        for e in range(N_EXPERTS):
            placed_smem[e] = 0
        sorted_buf[:, sort_rows:, :] = jnp.zeros(
            (sorted_buf.shape[0], sorted_buf.shape[1] - sort_rows, sorted_buf.shape[2]), _U32)

    hf = _rms(x_ref[...], g_ref[...])

    lane = lax.broadcasted_iota(jnp.int32, (tm, LANES), 1)
    h_hi = hf.astype(_BF16)
    h_lo = (hf - h_hi.astype(_F32)).astype(_BF16)
    w_r = w_r_ref[...]
    w_hi = w_r.astype(_BF16)
    w_lo = (w_r - w_hi.astype(_F32)).astype(_BF16)
    hi_terms = _dot(h_hi, jnp.concatenate([w_hi, w_lo], axis=1))
    logits = hi_terms[:, :LANES] + hi_terms[:, LANES:] + _dot(h_lo, w_hi) + b_r_ref[...]
    v1 = jnp.max(logits, axis=-1, keepdims=True)
    i1 = jnp.min(jnp.where(logits == v1, lane, LANES), axis=-1, keepdims=True)
    rest = jnp.where(lane == i1, NEG_BIG, logits)
    v2 = jnp.max(rest, axis=-1, keepdims=True)
    i2 = jnp.min(jnp.where(rest == v2, lane, LANES), axis=-1, keepdims=True)
    z = jnp.exp(v2 - v1)
    g1 = 1.0 / (1.0 + z)
    g2 = z / (1.0 + z)

    sel1 = lane == i1
    sel2 = lane == i2
    chosen = jnp.where(sel1 | sel2, 1.0, 0.0)
    before = _dot(earlier_ref[...], chosen.astype(_BF16))
    cnt = jnp.sum(chosen, axis=0, keepdims=True)
    cnt_pad = jnp.ceil(cnt * (1.0 / SUBLANES)) * SUBLANES
    group_off = _dot(jnp.broadcast_to(cnt_pad, (SUBLANES, LANES)).astype(_BF16),
                     lane_before_ref[...])[0:1, :]
    local = group_off + before
    loc1 = jnp.sum(jnp.where(sel1, local, 0.0), axis=-1, keepdims=True)
    loc2 = jnp.sum(jnp.where(sel2, local, 0.0), axis=-1, keepdims=True)

    slot = lax.rem(i, 2)
    loc_t = jnp.where(lane == 0, loc1, jnp.where(lane == 1, loc2, -1.0)).T
    row = lax.broadcasted_iota(jnp.int32, (sort_rows, tm), 0).astype(_F32)
    perm = jnp.where((row == loc_t[0:1, :]) | (row == loc_t[1:2, :]), 1.0, 0.0).astype(_BF16)
    sorted_buf[slot, 0:sort_rows, :] = _pack_halves(_dot(perm, h_hi))

    def window(src, dst):
        return pltpu.make_async_copy(sorted_buf.at[slot, pl.ds(src, GROUP_WINDOW)],
                                     hs_ref.at[pl.ds(dst, GROUP_WINDOW)], sem)

    def wait_windows(second_flags):
        for e in range(N_EXPERTS):
            window(0, 0).wait()

            @pl.when(second_flags(e) != 0)
            def _():
                window(0, 0).wait()

    @pl.when(i > 0)
    def _():
        wait_windows(lambda e: second_smem[e])

    off_i = group_off.astype(jnp.int32)
    cnt_i = cnt_pad.astype(jnp.int32)
    for e in range(N_EXPERTS):
        src = pl.multiple_of(off_i[0, e], SUBLANES)
        start = pl.multiple_of(e * region_rows + placed_smem[e], SUBLANES)
        rows = cnt_i[0, e]
        second = (rows > GROUP_WINDOW).astype(jnp.int32)
        window(src, start).start()

        @pl.when(second != 0)
        def _():
            window(src + GROUP_WINDOW, start + GROUP_WINDOW).start()

        second_smem[e] = second
        placed_smem[e] = placed_smem[e] + rows
        tab_ref[0, e] = start
        tab_ref[0, N_EXPERTS + e] = rows
    carry_ref[...] += cnt_pad
    cnt_ref[...] = carry_ref[...]

    @pl.when(i == pl.num_programs(0) - 1)
    def _():
        wait_windows(lambda e: second_smem[e])

        def tail(e, part):
            dst = pl.multiple_of(e * region_rows + placed_smem[e], SUBLANES) + part * GROUP_WINDOW
            return window(sort_rows, dst)

        for e in range(N_EXPERTS):
            for part in range(EXPERT_ROW_TILE // GROUP_WINDOW):
                tail(e, part).start()
        for e in range(N_EXPERTS):
            for part in range(EXPERT_ROW_TILE // GROUP_WINDOW):
                tail(e, part).wait()

    meta = jnp.where(lane == META_G1, g1, 0.0)
    meta = jnp.where(lane == META_G2, g2, meta)
    meta = jnp.where(lane == META_L1, loc1, meta)
    meta = jnp.where(lane == META_L2, loc2, meta)
    meta_ref[...] = meta


def _region_rows(n_tokens):
    pad = (n_tokens // ROUTE_TILE) * (SUBLANES - 1) + 2 * EXPERT_ROW_TILE
    return -(-(n_tokens + pad) // EXPERT_ROW_TILE) * EXPERT_ROW_TILE


def _route(layer, moe_layer, x2, g, w_r, b_r):
    T, D = x2.shape
    tm = ROUTE_TILE
    region_rows = _region_rows(T)
    sort_rows = TOP_K * tm + N_EXPERTS * SUBLANES
    earlier = jnp.asarray(np.tril(np.ones((tm, tm), np.float32), -1), _BF16)
    lane_before = jnp.asarray(np.triu(np.ones((LANES, LANES), np.float32), 1), _BF16)
    const = lambda n: pl.BlockSpec((n, n), lambda i: (0, 0), pipeline_mode=pl.Buffered(1))
    return pl.pallas_call(
        functools.partial(_route_kernel, region_rows=region_rows, sort_rows=sort_rows),
        grid=(T // tm,),
        in_specs=[pl.BlockSpec((tm, D), lambda i: (i, 0)), _layer_spec(g, layer),
                  _layer_spec(w_r, moe_layer), _layer_spec(b_r, moe_layer),
                  const(tm), const(LANES)],
        out_specs=[pl.BlockSpec((tm, LANES), lambda i: (i, 0)),
                   pl.BlockSpec((1, LANES), lambda i: (0, 0)),
                   pl.BlockSpec((None, 1, 2 * N_EXPERTS), lambda i: (i, 0, 0),
                                memory_space=pltpu.SMEM),
                   pl.BlockSpec(memory_space=pl.ANY)],
        out_shape=[jax.ShapeDtypeStruct((T, LANES), _F32),
                   jax.ShapeDtypeStruct((1, LANES), _F32),
                   jax.ShapeDtypeStruct((T // tm, 1, 2 * N_EXPERTS), jnp.int32),
                   jax.ShapeDtypeStruct((N_EXPERTS * region_rows, D // 2), _U32)],
        scratch_shapes=[pltpu.VMEM((1, LANES), _F32),
                        pltpu.SMEM((N_EXPERTS,), jnp.int32),
                        pltpu.SMEM((N_EXPERTS,), jnp.int32),
                        pltpu.VMEM((2, sort_rows + GROUP_WINDOW, D // 2), _U32),
                        pltpu.SemaphoreType.DMA],
        compiler_params=pltpu.CompilerParams(
            dimension_semantics=("arbitrary",), vmem_limit_bytes=VMEM_LIMIT),
        name="moe_route",
    )(x2, g, w_r, b_r, earlier, lane_before)


def _expert_kernel(te_ref, tb_ref, tv_ref, hs_ref, w_gu_ref, w_down_ref, ys_ref):
    del te_ref, tb_ref
    i = pl.program_id(0)

    rt = ys_ref.shape[0]
    half = rt // 2
    rows_used = tv_ref[i]

    def swiglu_rows(n_rows):
        lo, hi = _unpack_halves(hs_ref[0:n_rows, :])
        h = jnp.concatenate([lo.astype(_BF16), hi.astype(_BF16)], axis=1)
        acc = None
        for c0, n in FF_CHUNKS:
            gate = _dot(h, w_gu_ref[:, c0:c0 + n])
            up = _dot(h, w_gu_ref[:, D_FF + c0:D_FF + c0 + n])
            act = (gate * jax.nn.sigmoid(gate) * up).astype(_BF16)
            part = _dot(act, w_down_ref[c0:c0 + n, :])
            acc = part if acc is None else acc + part
        ys_ref[0:n_rows, :] = _pack_halves(acc)

    @pl.when(rows_used == 0)
    def _():
        ys_ref[...] = jnp.zeros_like(ys_ref)

    @pl.when((rows_used > 0) & (rows_used <= half))
    def _():
        swiglu_rows(half)
        ys_ref[half:, :] = jnp.zeros((rt - half, ys_ref.shape[1]), ys_ref.dtype)

    @pl.when(rows_used > half)
    def _():
        swiglu_rows(rt)


def _experts(tile_expert, tile_block, tile_rows, hs, w_gu, w_down):
    n_rows, W = hs.shape
    rt = EXPERT_ROW_TILE
    D = w_gu.shape[1]
    grid_spec = pltpu.PrefetchScalarGridSpec(
        num_scalar_prefetch=3,
        grid=(tile_expert.shape[0],),
        in_specs=[pl.BlockSpec((rt, W), lambda i, te, tb, tv: (jnp.where(tv[i] > 0, tb[i], 0), 0)),
                  pl.BlockSpec((None, D, 2 * D_FF), lambda i, te, tb, tv: (te[i], 0, 0)),
                  pl.BlockSpec((None, D_FF, D), lambda i, te, tb, tv: (te[i], 0, 0))],
        out_specs=pl.BlockSpec((rt, W), lambda i, te, tb, tv: (tb[i], 0)),
    )
    return pl.pallas_call(
        _expert_kernel,
        grid_spec=grid_spec,
        out_shape=jax.ShapeDtypeStruct((n_rows + rt, W), hs.dtype),
        compiler_params=pltpu.CompilerParams(
            dimension_semantics=("arbitrary",), vmem_limit_bytes=VMEM_LIMIT),
        name="moe_experts",
    )(tile_expert, tile_block, tile_rows, hs, w_gu, w_down)


def _combine_kernel(tab_ref, x_ref, meta_ref, g_final_ref, ys_ref, o_ref, ybuf, sem, *, final_norm):
    i = pl.program_id(0)
    n = pl.num_programs(0)
    tm, D = x_ref.shape
    slot = lax.rem(i, 2)

    @pl.when(i == 0)
    def _():
        ybuf[...] = jnp.zeros_like(ybuf)

    def fetch(tile, to_slot):
        off = 0
        for e in range(N_EXPERTS):
            start = tab_ref[tile, e]
            rows = tab_ref[tile, N_EXPERTS + e]
            size = ROUTE_TILE
            while size >= SUBLANES:
                above = rows & ~(2 * size - 1)
                src = pl.multiple_of(start + above, SUBLANES)
                dst = pl.multiple_of(off + above, SUBLANES)

                @pl.when((rows & size) != 0)
                def _():
                    pltpu.make_async_copy(ys_ref.at[pl.ds(src, size)],
                                          ybuf.at[to_slot, pl.ds(dst, size)], sem.at[to_slot]).start()
                size //= 2
            off = off + rows
        return off

    def total_rows(tile):
        total = 0
        for e in range(N_EXPERTS):
            total = total + tab_ref[tile, N_EXPERTS + e]
        return total

    @pl.when(i == 0)
    def _():
        fetch(0, 0)

    @pl.when(i + 1 < n)
    def _():
        fetch(i + 1, 1 - slot)

    landed = pl.multiple_of(total_rows(i), SUBLANES)
    pltpu.make_async_copy(ys_ref.at[pl.ds(0, landed)], ybuf.at[slot, pl.ds(0, landed)],
                          sem.at[slot]).wait()

    lo, hi = _unpack_halves(ybuf[slot])
    y = jnp.concatenate([lo.astype(_BF16), hi.astype(_BF16)], axis=1)
    col = lax.broadcasted_iota(jnp.int32, (tm, ybuf.shape[1]), 1).astype(_F32)
    pick1 = jnp.where(col == meta_ref[:, META_L1:META_L1 + 1], 1.0, 0.0).astype(_BF16)
    pick2 = jnp.where(col == meta_ref[:, META_L2:META_L2 + 1], 1.0, 0.0).astype(_BF16)
    out = (x_ref[...] + meta_ref[:, META_G1:META_G1 + 1] * _dot(pick1, y)
           + meta_ref[:, META_G2:META_G2 + 1] * _dot(pick2, y))
    if final_norm:
        out = _rms(out, g_final_ref[...])
    o_ref[...] = out


def _combine(tab, x2, meta, ys, g_final, final_norm):
    T, D = x2.shape
    tm = ROUTE_TILE
    sort_rows = -(-(TOP_K * tm + N_EXPERTS * SUBLANES) // MXU_DIM) * MXU_DIM
    tile = pl.BlockSpec((tm, D), lambda i, tab: (i, 0))
    grid_spec = pltpu.PrefetchScalarGridSpec(
        num_scalar_prefetch=1,
        grid=(T // tm,),
        in_specs=[tile, pl.BlockSpec((tm, LANES), lambda i, tab: (i, 0)),
                  pl.BlockSpec((1, D), lambda i, tab: (0, 0)), pl.BlockSpec(memory_space=pl.ANY)],
        out_specs=tile,
        scratch_shapes=[pltpu.VMEM((2, sort_rows, D // 2), _U32), pltpu.SemaphoreType.DMA((2,))],
    )
    return pl.pallas_call(
        functools.partial(_combine_kernel, final_norm=final_norm),
        grid_spec=grid_spec,
        out_shape=jax.ShapeDtypeStruct(x2.shape, x2.dtype),
        compiler_params=pltpu.CompilerParams(
            dimension_semantics=("arbitrary",), vmem_limit_bytes=VMEM_LIMIT),
        name="moe_combine",
    )(tab, x2, meta, g_final, ys)


def _moe(layer, moe_layer, x2, g, w_r, b_r, w_gu, w_down, g_final, final_norm):
    T, D = x2.shape
    E = w_gu.shape[0]
    rt = EXPERT_ROW_TILE
    meta, counts, tab, hs = _route(layer, moe_layer, x2, g, w_r, b_r)

    region_tiles = hs.shape[0] // (E * rt)
    max_rows = TOP_K * T + (T // ROUTE_TILE) * E * (SUBLANES - 1)
    n_tiles = -(-max_rows // rt) + E
    counts = counts[0, :E].astype(jnp.int32)
    group_tiles = (counts + rt - 1) // rt
    tile_end = jnp.cumsum(group_tiles)
    tile_ids = jnp.arange(n_tiles, dtype=jnp.int32)
    tile_expert = jnp.minimum(
        jnp.sum((tile_ids[:, None] >= tile_end[None, :]).astype(jnp.int32), axis=1), E - 1)
    tile_valid = (tile_ids < tile_end[E - 1]).astype(jnp.int32)
    own = tile_expert[:, None] == jnp.arange(E, dtype=jnp.int32)[None, :]
    lookup = lambda table: jnp.sum(jnp.where(own, table[None, :], 0), axis=1)
    in_group = tile_ids - lookup(tile_end - group_tiles)
    tile_block = jnp.where(tile_valid != 0, tile_expert * region_tiles + in_group, E * region_tiles)

    tile_rows = jnp.clip(lookup(counts) - in_group * rt, 0, rt) * tile_valid
    ys = _experts(tile_expert, tile_block.astype(jnp.int32), tile_rows.astype(jnp.int32), hs, w_gu, w_down)
    return _combine(tab.reshape(tab.shape[0], tab.shape[2]), x2, meta, ys, g_final, final_norm)


def _final_norm_kernel(x_ref, g_ref, o_ref):
    o_ref[...] = _rms(x_ref[...], g_ref[...])


def _final_norm(x2, g):
    T, D = x2.shape
    tm = TOK_TILE
    tile = pl.BlockSpec((tm, D), lambda i: (i, 0))
    return pl.pallas_call(
        _final_norm_kernel,
        grid=(T // tm,),
        in_specs=[tile, pl.BlockSpec((1, D), lambda i: (0, 0))],
        out_specs=tile,
        out_shape=jax.ShapeDtypeStruct(x2.shape, x2.dtype),
        compiler_params=pltpu.CompilerParams(dimension_semantics=("parallel",)),
        name="final_norm",
    )(x2, g)


def kernel(x, mem, g_mix, w_in, w_pool, pool_scale, w_dw, b_dw, conv_ln_g, conv_ln_b, w_conv_out, w_mix_out, g_xattn, g_mem, w_xq, w_xkv, w_xo, g_ffn, w_ffn_gu, w_ffn_down, w_router, b_router, w_moe_gu, w_moe_down, g_final):
    B, S, D = x.shape
    depth = g_mix.shape[0]
    bf = lambda a: a.astype(_BF16)
    rows = lambda a: a.reshape(a.shape[0], 1, a.shape[1])

    g_mix, pool_scale, b_dw, conv_ln_g, conv_ln_b = map(rows, (g_mix, pool_scale, b_dw, conv_ln_g, conv_ln_b))
    g_xattn, g_mem, g_ffn = map(rows, (g_xattn, g_mem, g_ffn))
    w_dw = w_dw.reshape(depth, CONV_KERNEL, CONV_WIDTH)
    w_in, w_pool, w_conv_out, w_mix_out = map(bf, (w_in, w_pool, w_conv_out, w_mix_out))
    w_xq, w_xkv, w_xo = map(bf, (w_xq, w_xkv, w_xo))
    w_ffn_gu, w_ffn_down = map(bf, (w_ffn_gu, w_ffn_down))
    w_r_pad = jnp.pad(w_router, ((0, 0), (0, 0), (0, LANES - N_EXPERTS)))
    b_r_pad = rows(jnp.pad(b_router, ((0, 0), (0, LANES - N_EXPERTS)), constant_values=NEG_BIG))
    g_final = g_final.reshape(1, D)

    k_all, v_all = _kv_proj(mem, g_mem, w_xkv)
    for i in range(depth):
        last = i == depth - 1
        mixer_args = (i, x, g_mix, w_in, w_pool, pool_scale, w_dw, b_dw, conv_ln_g, conv_ln_b,
                      w_conv_out, w_mix_out)
        if i % 2 == 0 and not last:
            job = _CastJob((i + 1) // 2, 0, w_moe_gu, w_moe_down, None, None)
            x, moe_gu_bf, moe_down_bf = _mixer(*mixer_args, job)
            x = _xattn(i, x, g_xattn, w_xq, k_all, v_all, w_xo)
            x2 = _ffn(i, i // 2, x.reshape(B * S, D), g_ffn, w_ffn_gu, w_ffn_down)
        else:
            x = _mixer(*mixer_args)
            x = _xattn(i, x, g_xattn, w_xq, k_all, v_all, w_xo)
            if i % 2 == 0:
                x2 = _final_norm(_ffn(i, i // 2, x.reshape(B * S, D), g_ffn, w_ffn_gu, w_ffn_down), g_final)
            else:
                x2 = _moe(i, i // 2, x.reshape(B * S, D), g_ffn, w_r_pad, b_r_pad, moe_gu_bf, moe_down_bf,
                          g_final, last)
        x = x2.reshape(B, S, D)
    return x
```

```python
import functools
from typing import NamedTuple, Optional

import jax
import jax.numpy as jnp
import numpy as np
from jax import lax
from jax.experimental import pallas as pl
from jax.experimental.pallas import tpu as pltpu

D_MODEL = 1024
POOL_WIDTH = 512
POOL_GROUPS = 4
POOL_WINDOWS = (2, 4, 8, 16)
POOL_GROUP_IN = POOL_WIDTH // POOL_GROUPS
POOL_GROUP_OUT = D_MODEL // POOL_GROUPS
CONV_WIDTH = 512
CONV_KERNEL = 31
IN_COLS = POOL_WIDTH + 2 * CONV_WIDTH + 2 * D_MODEL
N_XHEADS = 4
XHEAD_DIM = D_MODEL // N_XHEADS
D_FF = 2816
N_EXPERTS = 8
TOP_K = 2
EPS = 1e-6

COL_POOL = 0
COL_GLU_A = POOL_WIDTH
COL_GLU_B = POOL_WIDTH + CONV_WIDTH
COL_GATE_POOL = POOL_WIDTH + 2 * CONV_WIDTH
COL_GATE_CONV = COL_GATE_POOL + D_MODEL

LANES = 128
SUBLANES = 8
POOL_HALO = 16
CONV_HALO = 32
SEQ_TILE = 256
XATTN_TILE = 1024
TOK_TILE = 512
MXU_DIM = 256
FF_CHUNKS = ((0, 6 * MXU_DIM), (6 * MXU_DIM, 5 * MXU_DIM))
assert sum(n for _, n in FF_CHUNKS) == D_FF
CONV_ROW_CHUNK = 64
ROUTE_TILE = 512
EXPERT_ROW_TILE = 512
GROUP_WINDOW = 256
CAST_PARTS = 8
VMEM_LIMIT = 56 * 1024 * 1024
NEG_BIG = -1e30
META_G1, META_G2, META_L1, META_L2 = range(4)

_F32 = jnp.float32
_BF16 = jnp.bfloat16
_U32 = jnp.uint32
HIGH_HALF_MASK = np.uint32(0xFFFF0000)


def _layer_spec(arr, layer):
    tail = (0,) * (arr.ndim - 1)
    return pl.BlockSpec((None,) + arr.shape[1:], lambda *_: (layer,) + tail,
                        pipeline_mode=pl.Buffered(1))


def _rms(x, g):
    return x * lax.rsqrt(jnp.mean(x * x, axis=-1, keepdims=True) + EPS) * g


def _dot(a, b):
    return jnp.dot(a, b, preferred_element_type=_F32)


def _zero_like_bits(v):
    u = lax.bitcast_convert_type(v, _U32)
    return lax.bitcast_convert_type((u >> 16) >> 16, _F32)


def _mixer_kernel(x_ref, g_ref, w_in_ref, w_pool_ref, pscale_ref, w_dw_ref, b_dw_ref,
                  ln_g_ref, ln_b_ref, w_pw_ref, w_out_ref, *rest, cast):
    if cast is not None:
        rest, cast_slice = _split_cast_refs(rest, 1, cast)
        cast_slice()
    o_ref, up_ext, cv_ext, cv_shift, cv_out, mix_ref, gate_ref = rest
    s = pl.program_id(1)
    ts = x_ref.shape[0]

    @pl.when(s == 0)
    def _():
        up_ext[0:POOL_HALO, :] = jnp.zeros((POOL_HALO, POOL_WIDTH), _F32)
        cv_ext[:, 0:CONV_HALO, :] = jnp.zeros((cv_ext.shape[0], CONV_HALO, LANES), _F32)

    @pl.when(s > 0)
    def _():
        up_ext[0:POOL_HALO, :] = up_ext[ts:ts + POOL_HALO, :]
        cv_ext[:, 0:CONV_HALO, :] = cv_ext[:, ts:ts + CONV_HALO, :]

    x = x_ref[...]
    h = _rms(x, g_ref[...]).astype(_BF16)

    glu_a = _dot(h, w_in_ref[:, COL_GLU_A:COL_GLU_A + CONV_WIDTH])
    glu_b = _dot(h, w_in_ref[:, COL_GLU_B:COL_GLU_B + CONV_WIDTH])
    glu = glu_a * jax.nn.sigmoid(glu_b)
    for blk in range(cv_ext.shape[0]):
        cv_ext[blk, CONV_HALO:CONV_HALO + ts, :] = glu[:, blk * LANES:(blk + 1) * LANES]

    shift_rows = cv_shift.shape[2]
    for r in range(1, SUBLANES):
        cv_shift[r - 1] = cv_ext[:, r:r + shift_rows, :]

    tap0 = CONV_HALO - (CONV_KERNEL - 1)
    def side_task(col0, store):
        res = _dot(h, w_in_ref[:, col0:col0 + MXU_DIM])
        store(res)
        return _zero_like_bits(res[ts - 1:ts, MXU_DIM - LANES:])

    def pool_store(c):
        def store(res):
            up_ext[POOL_HALO:POOL_HALO + ts, c:c + MXU_DIM] = res
        return store

    def gate_store(c):
        def store(res):
            gate_ref[:, c:c + MXU_DIM] = jax.nn.sigmoid(res)
        return store

    side_tasks = ([(COL_POOL + c, pool_store(c)) for c in range(0, POOL_WIDTH, MXU_DIM)]
                  + [(COL_GATE_POOL + c, gate_store(c)) for c in range(0, 2 * D_MODEL, MXU_DIM)])
    unit_rows = CONV_ROW_CHUNK
    units = [(c0, q0) for c0 in range(0, CONV_WIDTH, LANES) for q0 in range(0, ts, unit_rows)]
    assert len(side_tasks) <= len(units)
    pace = None
    for u, (c0, q0) in enumerate(units):
        bias = b_dw_ref[:, c0:c0 + LANES]
        if pace is not None:
            bias = bias + pace
        pace = side_task(*side_tasks[u]) if u < len(side_tasks) else None
        for r0 in range(q0, q0 + unit_rows, CONV_ROW_CHUNK):
            acc = jnp.broadcast_to(bias, (CONV_ROW_CHUNK, LANES))
            for k in range(CONV_KERNEL):
                off = tap0 + k
                mis = off % SUBLANES
                row = r0 + off - mis
                if mis == 0:
                    win = cv_ext[c0 // LANES, row:row + CONV_ROW_CHUNK, :]
                else:
                    win = cv_shift[mis - 1, c0 // LANES, row:row + CONV_ROW_CHUNK, :]
                acc = acc + w_dw_ref[k:k + 1, c0:c0 + LANES] * win
            cv_out[r0:r0 + CONV_ROW_CHUNK, c0:c0 + LANES] = acc

    pos = (s * ts + 1 + lax.broadcasted_iota(jnp.int32, (ts, 1), 0)).astype(_F32)
    for g, w in enumerate(POOL_WINDOWS):
        c0 = g * POOL_GROUP_IN
        win = up_ext[:, c0:c0 + POOL_GROUP_IN]
        span = 1
        while span < w:
            win = win + pltpu.roll(win, span, 0)
            span *= 2
        cur = up_ext[POOL_HALO:POOL_HALO + ts, c0:c0 + POOL_GROUP_IN]
        pooled = win[POOL_HALO:, :] / jnp.minimum(pos, float(w)) - cur
        d0 = g * POOL_GROUP_OUT
        mix_ref[:, d0:d0 + POOL_GROUP_OUT] = (
            _dot(pooled.astype(_BF16), w_pool_ref[g]) * pscale_ref[:, d0:d0 + POOL_GROUP_OUT])

    u = cv_out[...]
    mu = jnp.mean(u, axis=-1, keepdims=True)
    uc = u - mu
    var = jnp.mean(uc * uc, axis=-1, keepdims=True)
    un = uc * lax.rsqrt(var + EPS) * ln_g_ref[...] + ln_b_ref[...]
    un = un * jax.nn.sigmoid(un)
    y_conv = _dot(un.astype(_BF16), w_pw_ref[...])
    mix = gate_ref[:, 0:D_MODEL] * mix_ref[...] + gate_ref[:, D_MODEL:] * y_conv

    o_ref[...] = x + _dot(mix.astype(_BF16), w_out_ref[...])


def _mixer(layer, x, g, w_in, w_pool, pscale, w_dw, b_dw, ln_g, ln_b, w_pw, w_out, job=None):
    B, S, D = x.shape
    ts = SEQ_TILE
    n_seq = S // ts
    shift_rows = ts + CONV_HALO - SUBLANES
    tile = pl.BlockSpec((None, ts, D), lambda b, s: (b, s, 0))
    params = (g, w_in, w_pool, pscale, w_dw, b_dw, ln_g, ln_b, w_pw, w_out)
    in_specs = [tile] + [_layer_spec(p, layer) for p in params]
    args = [x, *params]
    out_specs = [tile]
    out_shape = [jax.ShapeDtypeStruct(x.shape, x.dtype)]
    aliases, cast = {}, None
    if job is not None:
        assert B * n_seq == _cast_slices(job)
        extra = _cast_plumbing(job, lambda b, s: b * n_seq + s, len(in_specs), len(out_specs))
        in_specs += extra[0]
        args += extra[1]
        out_specs += extra[2]
        out_shape += extra[3]
        aliases, cast = extra[4], job.gu_bf is not None
    out = pl.pallas_call(
        functools.partial(_mixer_kernel, cast=cast),
        grid=(B, n_seq),
        in_specs=in_specs,
        out_specs=out_specs,
        out_shape=out_shape,
        input_output_aliases=aliases,
        scratch_shapes=[pltpu.VMEM((POOL_HALO + ts, POOL_WIDTH), _F32),
                        pltpu.VMEM((CONV_WIDTH // LANES, CONV_HALO + ts, LANES), _F32),
                        pltpu.VMEM((SUBLANES - 1, CONV_WIDTH // LANES, shift_rows, LANES), _F32),
                        pltpu.VMEM((ts, CONV_WIDTH), _F32),
                        pltpu.VMEM((ts, D), _F32),
                        pltpu.VMEM((ts, 2 * D), _F32)],
        compiler_params=pltpu.CompilerParams(
            dimension_semantics=("parallel", "arbitrary"), vmem_limit_bytes=VMEM_LIMIT),
        name="mixer",
    )(*args)
    return out[0] if job is None else tuple(out)


def _kv_kernel(m_ref, g_ref, w_ref, k_ref, v_ref):
    m = _rms(m_ref[...], g_ref[...]).astype(_BF16)
    k_ref[...] = (_dot(m, w_ref[:, 0:D_MODEL]) * (XHEAD_DIM ** -0.5)).astype(_BF16)
    v_ref[...] = _dot(m, w_ref[:, D_MODEL:2 * D_MODEL]).astype(_BF16)


def _kv_proj(mem, g, w_kv):
    B, M, D = mem.shape
    L = w_kv.shape[0]
    rows = B * M
    out_blk = pl.BlockSpec((None, rows, D), lambda l: (l, 0, 0))
    k, v = pl.pallas_call(
        _kv_kernel,
        grid=(L,),
        in_specs=[pl.BlockSpec((rows, D), lambda l: (0, 0)),
                  pl.BlockSpec((None,) + g.shape[1:], lambda l: (l, 0, 0)),
                  pl.BlockSpec((None,) + w_kv.shape[1:], lambda l: (l, 0, 0))],
        out_specs=[out_blk, out_blk],
        out_shape=[jax.ShapeDtypeStruct((L, rows, D), _BF16)] * 2,
        compiler_params=pltpu.CompilerParams(
            dimension_semantics=("parallel",), vmem_limit_bytes=VMEM_LIMIT),
        name="kv_proj",
    )(mem.reshape(rows, D), g, w_kv)
    return k.reshape(L, B, M, D), v.reshape(L, B, M, D)


class _CastJob(NamedTuple):
    moe_layer: int
    first_slice: int
    w_gu: jax.Array
    w_down: jax.Array
    gu_bf: Optional[jax.Array]
    down_bf: Optional[jax.Array]


def _cast_slices(job):
    return job.w_gu.shape[1] * CAST_PARTS


def _cast_plumbing(job, step_of, n_in, n_out):
    _, _, d, f2 = job.w_gu.shape
    f = job.w_down.shape[2]

    def part(*grid_idx):
        q = jnp.minimum(job.first_slice + step_of(*grid_idx), _cast_slices(job) - 1)
        return q // CAST_PARTS, q % CAST_PARTS

    def src_idx(*grid_idx):
        return (job.moe_layer,) + part(*grid_idx) + (0,)

    def dst_idx(*grid_idx):
        return part(*grid_idx) + (0,)

    in_specs = [pl.BlockSpec((None, None, d // CAST_PARTS, f2), src_idx),
                pl.BlockSpec((None, None, f // CAST_PARTS, d), src_idx)]
    out_specs = [pl.BlockSpec((None, d // CAST_PARTS, f2), dst_idx),
                 pl.BlockSpec((None, f // CAST_PARTS, d), dst_idx)]
    args, aliases = [job.w_gu, job.w_down], {}
    if job.gu_bf is not None:
        in_specs += [pl.BlockSpec(memory_space=pl.ANY)] * 2
        args += [job.gu_bf, job.down_bf]
        aliases = {n_in + 2: n_out, n_in + 3: n_out + 1}
    out_shapes = [jax.ShapeDtypeStruct(job.w_gu.shape[1:], _BF16),
                  jax.ShapeDtypeStruct(job.w_down.shape[1:], _BF16)]
    return in_specs, args, out_specs, out_shapes, aliases


def _split_cast_refs(rest, n_out, cast_fill):
    n_cast_in = 4 if cast_fill else 2
    gu_in, down_in = rest[0], rest[1]
    host_out = rest[n_cast_in:n_cast_in + n_out]
    gu_out, down_out = rest[n_cast_in + n_out], rest[n_cast_in + n_out + 1]

    def cast():
        gu_out[...] = gu_in[...].astype(_BF16)
        down_out[...] = down_in[...].astype(_BF16)

    return tuple(host_out) + tuple(rest[n_cast_in + n_out + 2:]), cast


def _xattn_kernel(x_ref, g_ref, wq_ref, k_ref, v_ref, wo_ref, o_ref, att_ref):
    x = x_ref[...]
    h = _rms(x, g_ref[...]).astype(_BF16)
    q = _dot(h, wq_ref[...]).astype(_BF16)
    for hd in range(N_XHEADS):
        c0 = hd * XHEAD_DIM
        sc = lax.dot_general(q[:, c0:c0 + XHEAD_DIM], k_ref[:, c0:c0 + XHEAD_DIM],
                             (((1,), (1,)), ((), ())), preferred_element_type=_F32)
        p = jnp.exp(sc - jnp.max(sc, axis=-1, keepdims=True))
        denom = jnp.sum(p, axis=-1, keepdims=True)
        att = _dot(p.astype(_BF16), v_ref[:, c0:c0 + XHEAD_DIM]) / denom
        att_ref[:, c0:c0 + XHEAD_DIM] = att.astype(_BF16)
    o_ref[...] = x + _dot(att_ref[...], wo_ref[...])


def _xattn(layer, x, g, w_q, k, v, w_o):
    B, S, D = x.shape
    M = k.shape[2]
    ts = XATTN_TILE
    tile = pl.BlockSpec((None, ts, D), lambda b, s: (b, s, 0))
    kv_blk = pl.BlockSpec((None, None, M, D), lambda b, s: (layer, b, 0, 0))
    return pl.pallas_call(
        _xattn_kernel,
        grid=(B, S // ts),
        in_specs=[tile, _layer_spec(g, layer), _layer_spec(w_q, layer), kv_blk, kv_blk,
                  _layer_spec(w_o, layer)],
        out_specs=tile,
        out_shape=jax.ShapeDtypeStruct(x.shape, x.dtype),
        scratch_shapes=[pltpu.VMEM((ts, D), _BF16)],
        compiler_params=pltpu.CompilerParams(
            dimension_semantics=("parallel", "parallel"), vmem_limit_bytes=VMEM_LIMIT),
        name="xattn",
    )(x, g, w_q, k, v, w_o)


def _ffn_kernel(x_ref, g_ref, w_gu_ref, w_down_ref, o_ref):
    x = x_ref[...]
    h = _rms(x, g_ref[...]).astype(_BF16)
    acc = x
    for c0, n in FF_CHUNKS:
        gate = _dot(h, w_gu_ref[:, c0:c0 + n])
        up = _dot(h, w_gu_ref[:, D_FF + c0:D_FF + c0 + n])
        act = (gate * jax.nn.sigmoid(gate) * up).astype(_BF16)
        acc = acc + _dot(act, w_down_ref[c0:c0 + n, :])
    o_ref[...] = acc


def _ffn(layer, ffn_layer, x2, g, w_gu, w_down):
    T, D = x2.shape
    tm = TOK_TILE
    tile = pl.BlockSpec((tm, D), lambda i: (i, 0))
    return pl.pallas_call(
        _ffn_kernel,
        grid=(T // tm,),
        in_specs=[tile, _layer_spec(g, layer), _layer_spec(w_gu, ffn_layer),
                  _layer_spec(w_down, ffn_layer)],
        out_specs=tile,
        out_shape=jax.ShapeDtypeStruct(x2.shape, x2.dtype),
        compiler_params=pltpu.CompilerParams(
            dimension_semantics=("parallel",), vmem_limit_bytes=VMEM_LIMIT),
        name="ffn",
    )(x2, g, w_gu, w_down)


def _pack_halves(y):
    half = y.shape[1] // 2
    lo = lax.bitcast_convert_type(y[:, :half].astype(_BF16).astype(_F32), _U32)
    hi = lax.bitcast_convert_type(y[:, half:].astype(_BF16).astype(_F32), _U32)
    return (lo >> 16) | (hi & HIGH_HALF_MASK)


def _unpack_halves(p):
    lo = lax.bitcast_convert_type(p << 16, _F32)
    hi = lax.bitcast_convert_type(p & HIGH_HALF_MASK, _F32)
    return lo, hi


def _route_kernel(x_ref, g_ref, w_r_ref, b_r_ref, earlier_ref, lane_before_ref,
                  meta_ref, cnt_ref, tab_ref, hs_ref,
                  carry_ref, placed_smem, second_smem, sorted_buf, sem,
                  *, region_rows, sort_rows):
    i = pl.program_id(0)
    tm = x_ref.shape[0]

    @pl.when(i == 0)
    def _():
        carry_ref[...] = jnp.zeros_like(carry_ref)
        for e in range(N_EXPERTS):
            placed_smem[e] = 0
        sorted_buf[:, sort_rows:, :] = jnp.zeros(
            (sorted_buf.shape[0], sorted_buf.shape[1] - sort_rows, sorted_buf.shape[2]), _U32)

    hf = _rms(x_ref[...], g_ref[...])

    lane = lax.broadcasted_iota(jnp.int32, (tm, LANES), 1)
    h_hi = hf.astype(_BF16)
    h_lo = (hf - h_hi.astype(_F32)).astype(_BF16)
    w_r = w_r_ref[...]
    w_hi = w_r.astype(_BF16)
    w_lo = (w_r - w_hi.astype(_F32)).astype(_BF16)
    hi_terms = _dot(h_hi, jnp.concatenate([w_hi, w_lo], axis=1))
    logits = hi_terms[:, :LANES] + hi_terms[:, LANES:] + _dot(h_lo, w_hi) + b_r_ref[...]
    v1 = jnp.max(logits, axis=-1, keepdims=True)
    i1 = jnp.min(jnp.where(logits == v1, lane, LANES), axis=-1, keepdims=True)
    rest = jnp.where(lane == i1, NEG_BIG, logits)
    v2 = jnp.max(rest, axis=-1, keepdims=True)
    i2 = jnp.min(jnp.where(rest == v2, lane, LANES), axis=-1, keepdims=True)
    z = jnp.exp(v2 - v1)
    g1 = 1.0 / (1.0 + z)
    g2 = z / (1.0 + z)

    sel1 = lane == i1
    sel2 = lane == i2
    chosen = jnp.where(sel1 | sel2, 1.0, 0.0)
    before = _dot(earlier_ref[...], chosen.astype(_BF16))
    cnt = jnp.sum(chosen, axis=0, keepdims=True)
    cnt_pad = jnp.ceil(cnt * (1.0 / SUBLANES)) * SUBLANES
    group_off = _dot(jnp.broadcast_to(cnt_pad, (SUBLANES, LANES)).astype(_BF16),
                     lane_before_ref[...])[0:1, :]
    local = group_off + before
    loc1 = jnp.sum(jnp.where(sel1, local, 0.0), axis=-1, keepdims=True)
    loc2 = jnp.sum(jnp.where(sel2, local, 0.0), axis=-1, keepdims=True)

    slot = lax.rem(i, 2)
    loc_t = jnp.where(lane == 0, loc1, jnp.where(lane == 1, loc2, -1.0)).T
    row = lax.broadcasted_iota(jnp.int32, (sort_rows, tm), 0).astype(_F32)
    perm = jnp.where((row == loc_t[0:1, :]) | (row == loc_t[1:2, :]), 1.0, 0.0).astype(_BF16)
    sorted_buf[slot, 0:sort_rows, :] = _pack_halves(_dot(perm, h_hi))

    def window(src, dst):
        return pltpu.make_async_copy(sorted_buf.at[slot, pl.ds(src, GROUP_WINDOW)],
                                     hs_ref.at[pl.ds(dst, GROUP_WINDOW)], sem)

    def wait_windows(second_flags):
        for e in range(N_EXPERTS):
            window(0, 0).wait()

            @pl.when(second_flags(e) != 0)
            def _():
                window(0, 0).wait()

    @pl.when(i > 0)
    def _():
        wait_windows(lambda e: second_smem[e])

    off_i = group_off.astype(jnp.int32)
    cnt_i = cnt_pad.astype(jnp.int32)
    for e in range(N_EXPERTS):
        src = pl.multiple_of(off_i[0, e], SUBLANES)
        start = pl.multiple_of(e * region_rows + placed_smem[e], SUBLANES)
        rows = cnt_i[0, e]
        second = (rows > GROUP_WINDOW).astype(jnp.int32)
        window(src, start).start()

        @pl.when(second != 0)
        def _():
            window(src + GROUP_WINDOW, start + GROUP_WINDOW).start()

        second_smem[e] = second
        placed_smem[e] = placed_smem[e] + rows
        tab_ref[0, e] = start
        tab_ref[0, N_EXPERTS + e] = rows
    carry_ref[...] += cnt_pad
    cnt_ref[...] = carry_ref[...]

    @pl.when(i == pl.num_programs(0) - 1)
    def _():
        wait_windows(lambda e: second_smem[e])

        def tail(e, part):
            dst = pl.multiple_of(e * region_rows + placed_smem[e], SUBLANES) + part * GROUP_WINDOW
            return window(sort_rows, dst)

        for e in range(N_EXPERTS):
            for part in range(EXPERT_ROW_TILE // GROUP_WINDOW):
                tail(e, part).start()
        for e in range(N_EXPERTS):
            for part in range(EXPERT_ROW_TILE // GROUP_WINDOW):
                tail(e, part).wait()

    meta = jnp.where(lane == META_G1, g1, 0.0)
    meta = jnp.where(lane == META_G2, g2, meta)
    meta = jnp.where(lane == META_L1, loc1, meta)
    meta = jnp.where(lane == META_L2, loc2, meta)
    meta_ref[...] = meta


def _region_rows(n_tokens):
    pad = (n_tokens // ROUTE_TILE) * (SUBLANES - 1) + 2 * EXPERT_ROW_TILE
    return -(-(n_tokens + pad) // EXPERT_ROW_TILE) * EXPERT_ROW_TILE


def _route(layer, moe_layer, x2, g, w_r, b_r):
    T, D = x2.shape
    tm = ROUTE_TILE
    region_rows = _region_rows(T)
    sort_rows = TOP_K * tm + N_EXPERTS * SUBLANES
    earlier = jnp.asarray(np.tril(np.ones((tm, tm), np.float32), -1), _BF16)
    lane_before = jnp.asarray(np.triu(np.ones((LANES, LANES), np.float32), 1), _BF16)
    const = lambda n: pl.BlockSpec((n, n), lambda i: (0, 0), pipeline_mode=pl.Buffered(1))
    return pl.pallas_call(
        functools.partial(_route_kernel, region_rows=region_rows, sort_rows=sort_rows),
        grid=(T // tm,),
        in_specs=[pl.BlockSpec((tm, D), lambda i: (i, 0)), _layer_spec(g, layer),
                  _layer_spec(w_r, moe_layer), _layer_spec(b_r, moe_layer),
                  const(tm), const(LANES)],
        out_specs=[pl.BlockSpec((tm, LANES), lambda i: (i, 0)),
                   pl.BlockSpec((1, LANES), lambda i: (0, 0)),
                   pl.BlockSpec((None, 1, 2 * N_EXPERTS), lambda i: (i, 0, 0),
                                memory_space=pltpu.SMEM),
                   pl.BlockSpec(memory_space=pl.ANY)],
        out_shape=[jax.ShapeDtypeStruct((T, LANES), _F32),
                   jax.ShapeDtypeStruct((1, LANES), _F32),
                   jax.ShapeDtypeStruct((T // tm, 1, 2 * N_EXPERTS), jnp.int32),
                   jax.ShapeDtypeStruct((N_EXPERTS * region_rows, D // 2), _U32)],
        scratch_shapes=[pltpu.VMEM((1, LANES), _F32),
                        pltpu.SMEM((N_EXPERTS,), jnp.int32),
                        pltpu.SMEM((N_EXPERTS,), jnp.int32),
                        pltpu.VMEM((2, sort_rows + GROUP_WINDOW, D // 2), _U32),
                        pltpu.SemaphoreType.DMA],
        compiler_params=pltpu.CompilerParams(
            dimension_semantics=("arbitrary",), vmem_limit_bytes=VMEM_LIMIT),
        name="moe_route",
    )(x2, g, w_r, b_r, earlier, lane_before)


def _expert_kernel(te_ref, tb_ref, tv_ref, hs_ref, w_gu_ref, w_down_ref, ys_ref):
    del te_ref, tb_ref
    i = pl.program_id(0)

    rt = ys_ref.shape[0]
    half = rt // 2
    rows_used = tv_ref[i]

    def swiglu_rows(n_rows):
        lo, hi = _unpack_halves(hs_ref[0:n_rows, :])
        h = jnp.concatenate([lo.astype(_BF16), hi.astype(_BF16)], axis=1)
        acc = None
        for c0, n in FF_CHUNKS:
            gate = _dot(h, w_gu_ref[:, c0:c0 + n])
            up = _dot(h, w_gu_ref[:, D_FF + c0:D_FF + c0 + n])
            act = (gate * jax.nn.sigmoid(gate) * up).astype(_BF16)
            part = _dot(act, w_down_ref[c0:c0 + n, :])
            acc = part if acc is None else acc + part
        ys_ref[0:n_rows, :] = _pack_halves(acc)

    @pl.when(rows_used == 0)
    def _():
        ys_ref[...] = jnp.zeros_like(ys_ref)

    @pl.when((rows_used > 0) & (rows_used <= half))
    def _():
        swiglu_rows(half)
        ys_ref[half:, :] = jnp.zeros((rt - half, ys_ref.shape[1]), ys_ref.dtype)

    @pl.when(rows_used > half)
    def _():
        swiglu_rows(rt)


def _experts(tile_expert, tile_block, tile_rows, hs, w_gu, w_down):
    n_rows, W = hs.shape
    rt = EXPERT_ROW_TILE
    D = w_gu.shape[1]
    grid_spec = pltpu.PrefetchScalarGridSpec(
        num_scalar_prefetch=3,
        grid=(tile_expert.shape[0],),
        in_specs=[pl.BlockSpec((rt, W), lambda i, te, tb, tv: (jnp.where(tv[i] > 0, tb[i], 0), 0)),
                  pl.BlockSpec((None, D, 2 * D_FF), lambda i, te, tb, tv: (te[i], 0, 0)),
                  pl.BlockSpec((None, D_FF, D), lambda i, te, tb, tv: (te[i], 0, 0))],
        out_specs=pl.BlockSpec((rt, W), lambda i, te, tb, tv: (tb[i], 0)),
    )
    return pl.pallas_call(
        _expert_kernel,
        grid_spec=grid_spec,
        out_shape=jax.ShapeDtypeStruct((n_rows + rt, W), hs.dtype),
        compiler_params=pltpu.CompilerParams(
            dimension_semantics=("arbitrary",), vmem_limit_bytes=VMEM_LIMIT),
        name="moe_experts",
    )(tile_expert, tile_block, tile_rows, hs, w_gu, w_down)


def _combine_kernel(tab_ref, x_ref, meta_ref, g_final_ref, ys_ref, o_ref, ybuf, sem, *, final_norm):
    i = pl.program_id(0)
    n = pl.num_programs(0)
    tm, D = x_ref.shape
    slot = lax.rem(i, 2)

    @pl.when(i == 0)
    def _():
        ybuf[...] = jnp.zeros_like(ybuf)

    def fetch(tile, to_slot):
        off = 0
        for e in range(N_EXPERTS):
            start = tab_ref[tile, e]
            rows = tab_ref[tile, N_EXPERTS + e]
            size = ROUTE_TILE
            while size >= SUBLANES:
                above = rows & ~(2 * size - 1)
                src = pl.multiple_of(start + above, SUBLANES)
                dst = pl.multiple_of(off + above, SUBLANES)

                @pl.when((rows & size) != 0)
                def _():
                    pltpu.make_async_copy(ys_ref.at[pl.ds(src, size)],
                                          ybuf.at[to_slot, pl.ds(dst, size)], sem.at[to_slot]).start()
                size //= 2
            off = off + rows
        return off

    def total_rows(tile):
        total = 0
        for e in range(N_EXPERTS):
            total = total + tab_ref[tile, N_EXPERTS + e]
        return total

    @pl.when(i == 0)
    def _():
        fetch(0, 0)

    @pl.when(i + 1 < n)
    def _():
        fetch(i + 1, 1 - slot)

    landed = pl.multiple_of(total_rows(i), SUBLANES)
    pltpu.make_async_copy(ys_ref.at[pl.ds(0, landed)], ybuf.at[slot, pl.ds(0, landed)],
                          sem.at[slot]).wait()

    lo, hi = _unpack_halves(ybuf[slot])
    y = jnp.concatenate([lo.astype(_BF16), hi.astype(_BF16)], axis=1)
    col = lax.broadcasted_iota(jnp.int32, (tm, ybuf.shape[1]), 1).astype(_F32)
    pick1 = jnp.where(col == meta_ref[:, META_L1:META_L1 + 1], 1.0, 0.0).astype(_BF16)
    pick2 = jnp.where(col == meta_ref[:, META_L2:META_L2 + 1], 1.0, 0.0).astype(_BF16)
    out = (x_ref[...] + meta_ref[:, META_G1:META_G1 + 1] * _dot(pick1, y)
           + meta_ref[:, META_G2:META_G2 + 1] * _dot(pick2, y))
    if final_norm:
        out = _rms(out, g_final_ref[...])
    o_ref[...] = out


def _combine(tab, x2, meta, ys, g_final, final_norm):
    T, D = x2.shape
    tm = ROUTE_TILE
    sort_rows = -(-(TOP_K * tm + N_EXPERTS * SUBLANES) // MXU_DIM) * MXU_DIM
    tile = pl.BlockSpec((tm, D), lambda i, tab: (i, 0))
    grid_spec = pltpu.PrefetchScalarGridSpec(
        num_scalar_prefetch=1,
        grid=(T // tm,),
        in_specs=[tile, pl.BlockSpec((tm, LANES), lambda i, tab: (i, 0)),
                  pl.BlockSpec((1, D), lambda i, tab: (0, 0)), pl.BlockSpec(memory_space=pl.ANY)],
        out_specs=tile,
        scratch_shapes=[pltpu.VMEM((2, sort_rows, D // 2), _U32), pltpu.SemaphoreType.DMA((2,))],
    )
    return pl.pallas_call(
        functools.partial(_combine_kernel, final_norm=final_norm),
        grid_spec=grid_spec,
        out_shape=jax.ShapeDtypeStruct(x2.shape, x2.dtype),
        compiler_params=pltpu.CompilerParams(
            dimension_semantics=("arbitrary",), vmem_limit_bytes=VMEM_LIMIT),
        name="moe_combine",
    )(tab, x2, meta, g_final, ys)


def _moe(layer, moe_layer, x2, g, w_r, b_r, w_gu, w_down, g_final, final_norm):
    T, D = x2.shape
    E = w_gu.shape[0]
    rt = EXPERT_ROW_TILE
    meta, counts, tab, hs = _route(layer, moe_layer, x2, g, w_r, b_r)

    region_tiles = hs.shape[0] // (E * rt)
    max_rows = TOP_K * T + (T // ROUTE_TILE) * E * (SUBLANES - 1)
    n_tiles = -(-max_rows // rt) + E
    counts = counts[0, :E].astype(jnp.int32)
    group_tiles = (counts + rt - 1) // rt
    tile_end = jnp.cumsum(group_tiles)
    tile_ids = jnp.arange(n_tiles, dtype=jnp.int32)
    tile_expert = jnp.minimum(
        jnp.sum((tile_ids[:, None] >= tile_end[None, :]).astype(jnp.int32), axis=1), E - 1)
    tile_valid = (tile_ids < tile_end[E - 1]).astype(jnp.int32)
    own = tile_expert[:, None] == jnp.arange(E, dtype=jnp.int32)[None, :]
    lookup = lambda table: jnp.sum(jnp.where(own, table[None, :], 0), axis=1)
    in_group = tile_ids - lookup(tile_end - group_tiles)
    tile_block = jnp.where(tile_valid != 0, tile_expert * region_tiles + in_group, E * region_tiles)

    tile_rows = jnp.clip(lookup(counts) - in_group * rt, 0, rt) * tile_valid
    ys = _experts(tile_expert, tile_block.astype(jnp.int32), tile_rows.astype(jnp.int32), hs, w_gu, w_down)
    return _combine(tab.reshape(tab.shape[0], tab.shape[2]), x2, meta, ys, g_final, final_norm)


def _final_norm_kernel(x_ref, g_ref, o_ref):
    o_ref[...] = _rms(x_ref[...], g_ref[...])


def _final_norm(x2, g):
    T, D = x2.shape
    tm = TOK_TILE
    tile = pl.BlockSpec((tm, D), lambda i: (i, 0))
    return pl.pallas_call(
        _final_norm_kernel,
        grid=(T // tm,),
        in_specs=[tile, pl.BlockSpec((1, D), lambda i: (0, 0))],
        out_specs=tile,
        out_shape=jax.ShapeDtypeStruct(x2.shape, x2.dtype),
        compiler_params=pltpu.CompilerParams(dimension_semantics=("parallel",)),
        name="final_norm",
    )(x2, g)


def kernel(x, mem, g_mix, w_in, w_pool, pool_scale, w_dw, b_dw, conv_ln_g, conv_ln_b, w_conv_out, w_mix_out, g_xattn, g_mem, w_xq, w_xkv, w_xo, g_ffn, w_ffn_gu, w_ffn_down, w_router, b_router, w_moe_gu, w_moe_down, g_final):
    B, S, D = x.shape
    depth = g_mix.shape[0]
    bf = lambda a: a.astype(_BF16)
    rows = lambda a: a.reshape(a.shape[0], 1, a.shape[1])

    g_mix, pool_scale, b_dw, conv_ln_g, conv_ln_b = map(rows, (g_mix, pool_scale, b_dw, conv_ln_g, conv_ln_b))
    g_xattn, g_mem, g_ffn = map(rows, (g_xattn, g_mem, g_ffn))
    w_dw = w_dw.reshape(depth, CONV_KERNEL, CONV_WIDTH)
    w_in, w_pool, w_conv_out, w_mix_out = map(bf, (w_in, w_pool, w_conv_out, w_mix_out))
    w_xq, w_xkv, w_xo = map(bf, (w_xq, w_xkv, w_xo))
    w_ffn_gu, w_ffn_down = map(bf, (w_ffn_gu, w_ffn_down))
    w_r_pad = jnp.pad(w_router, ((0, 0), (0, 0), (0, LANES - N_EXPERTS)))
    b_r_pad = rows(jnp.pad(b_router, ((0, 0), (0, LANES - N_EXPERTS)), constant_values=NEG_BIG))
    g_final = g_final.reshape(1, D)

    k_all, v_all = _kv_proj(mem, g_mem, w_xkv)
    for i in range(depth):
        last = i == depth - 1
        mixer_args = (i, x, g_mix, w_in, w_pool, pool_scale, w_dw, b_dw, conv_ln_g, conv_ln_b,
                      w_conv_out, w_mix_out)
        if i % 2 == 0 and not last:
            job = _CastJob((i + 1) // 2, 0, w_moe_gu, w_moe_down, None, None)
            x, moe_gu_bf, moe_down_bf = _mixer(*mixer_args, job)
            x = _xattn(i, x, g_xattn, w_xq, k_all, v_all, w_xo)
            x2 = _ffn(i, i // 2, x.reshape(B * S, D), g_ffn, w_ffn_gu, w_ffn_down)
        else:
            x = _mixer(*mixer_args)
            x = _xattn(i, x, g_xattn, w_xq, k_all, v_all, w_xo)
            if i % 2 == 0:
                x2 = _final_norm(_ffn(i, i // 2, x.reshape(B * S, D), g_ffn, w_ffn_gu, w_ffn_down), g_final)
            else:
                x2 = _moe(i, i // 2, x.reshape(B * S, D), g_ffn, w_r_pad, b_r_pad, moe_gu_bf, moe_down_bf,
                          g_final, last)
        x = x2.reshape(B, S, D)
    return x
```
